```python
import math
import jax, jax.numpy as jnp
from jax import lax
import numpy as np

D_MODEL = 2048
BATCH = 32
SEQ = 256
DEPTH = 4
DEC_BATCH = 2
DEC_SEQ = 4096
PAST_LEN = 512

GRID_W = 64
EPS = 1e-6
N_MOD = 6
N_BRANCH = 3
DA_HEADS = 8
DA_HEAD = 64
DA_QK = 2 * DA_HEAD
DA_VDIM = 2 * DA_HEAD
DA_WIDTH = DA_HEADS * DA_VDIM
ROPE_BASE = 10000.0
AX_DIM = DA_HEAD // 2
Q_BLOCK = 128
GLA_HEADS = 4
GLA_DK = 128
GLA_DV = 256
GLA_KTOT = GLA_HEADS * GLA_DK
GLA_WIDTH = GLA_HEADS * GLA_DV
GLA_RANK = 16
GLA_TAU = 16.0
GLA_CHUNK = 64
S5_WIDTH = 1024
S5_CH = 16
S5_GROUPS = S5_WIDTH // S5_CH
S5_STATE = 64
N_EXPERTS = 16
N_EXPERT_GROUPS = 4
EXPERTS_PER_GROUP = N_EXPERTS // N_EXPERT_GROUPS
TOPK_GROUP = 1
TOP_K = 2
D_FF_EXPERT = 1408
MOE_BLOCK = 128
IN_SIZES = (DA_HEADS * DA_QK, DA_HEADS * DA_QK, DA_WIDTH, GLA_KTOT, GLA_KTOT, GLA_WIDTH, GLA_WIDTH, GLA_RANK, S5_WIDTH)
IN_COLS = 2 * DA_HEADS * DA_QK + DA_WIDTH + 2 * GLA_KTOT + 2 * GLA_WIDTH + GLA_RANK + S5_WIDTH

kernel_name = 'hybrid_diffusion_prefix_trunk_step'

F32 = jnp.float32


def rmsnorm(x, g):
    xf = x.astype(F32)
    y = xf * lax.rsqrt(jnp.mean(xf * xf, axis=-1, keepdims=True) + EPS)
    return (y * g.astype(F32)).astype(x.dtype)


def split_columns(p):
    outs = []
    start = 0
    for size in IN_SIZES:
        outs.append(p[..., start:start + size])
        start += size
    return outs


def axial_rope(x):
    n_tokens = x.shape[1]
    rows = n_tokens // GRID_W
    row = jnp.repeat(jnp.arange(rows, dtype=F32), GRID_W)
    col = jnp.tile(jnp.arange(GRID_W, dtype=F32), rows)
    inv = ROPE_BASE ** (-jnp.arange(0, AX_DIM, 2, dtype=F32) / AX_DIM)
    ar = row[:, None] * inv[None, :]
    ac = col[:, None] * inv[None, :]
    ang = jnp.concatenate([ar, ar, ac, ac], axis=-1)[None, :, None, None, :]
    cos, sin = jnp.cos(ang), jnp.sin(ang)
    xf = x.astype(F32)
    r1, r2, c1, c2 = jnp.split(xf, 4, axis=-1)
    rot = jnp.concatenate([-r2, r1, -c2, c1], axis=-1)
    return (xf * cos + rot * sin).astype(x.dtype)


def diff_lambda(lq, lk, layer_idx):
    lam_init = 0.8 - 0.6 * math.exp(-0.3 * layer_idx)
    lqf = lq.astype(F32)
    lkf = lk.astype(F32)
    lam = jnp.exp(jnp.sum(lqf[0] * lkf[0])) - jnp.exp(jnp.sum(lqf[1] * lkf[1])) + lam_init
    return lam, lam_init


def diff_attention(q, k, v, lam):
    B, T, H = q.shape[:3]
    nb = T // Q_BLOCK
    qb = q.reshape(B, nb, Q_BLOCK, H, 2, DA_HEAD).transpose(1, 0, 2, 3, 4, 5)
    scale = DA_HEAD ** -0.5

    def one_block(qblk):
        s = jnp.einsum('bqhcd,bkhcd->bhcqk', qblk, k, preferred_element_type=F32) * scale
        p = jax.nn.softmax(s, axis=-1)
        pd = p[:, :, 0] - lam * p[:, :, 1]
        return jnp.einsum('bhqk,bkhv->bqhv', pd.astype(v.dtype), v)

    o = lax.map(one_block, qb)
    return o.transpose(1, 0, 2, 3, 4).reshape(B, T, H, v.shape[-1])


def gla_chunked(q, k, v, log_a, s0):
    B, T, H, dk = q.shape
    dv = v.shape[-1]
    C = GLA_CHUNK
    N = T // C
    q = q.reshape(B, N, C, H, dk)
    k = k.reshape(B, N, C, H, dk)
    v = v.reshape(B, N, C, H, dv)
    b = jnp.cumsum(log_a.reshape(B, N, C, H, dk), axis=2)
    b_last = b[:, :, -1]
    q_t = q * jnp.exp(b)
    k_t = k * jnp.exp(-b)
    k_end = k * jnp.exp(b_last[:, :, None] - b)
    mask = jnp.tril(jnp.ones((C, C), dtype=bool))
    att = jnp.where(mask, jnp.einsum('bnchd,bnshd->bnhcs', q_t, k_t), 0.0)
    o_intra = jnp.einsum('bnhcs,bnshv->bnchv', att, v)

    def step(S, xs):
        qn, kn, vn, bl = xs
        o = jnp.einsum('bchd,bhdv->bchv', qn, S)
        S = jnp.exp(bl)[..., None] * S + jnp.einsum('bchd,bchv->bhdv', kn, vn)
        return S, o

    s_fin, o_inter = lax.scan(step, s0, (q_t.transpose(1, 0, 2, 3, 4), k_end.transpose(1, 0, 2, 3, 4),
                                         v.transpose(1, 0, 2, 3, 4), b_last.transpose(1, 0, 2, 3)))
    o = o_intra + o_inter.transpose(1, 0, 2, 3, 4)
    return o.reshape(B, T, H, dv), s_fin


def gla_bidir(q, k, v, a_lr, w_alpha, b_alpha, s0):
    B, T = q.shape[:2]
    qf, kf, vf = q.astype(F32), k.astype(F32), v.astype(F32)
    a_f = a_lr.astype(F32)
    outs = []
    finals = []
    for d in range(2):
        la = jax.nn.log_sigmoid(a_f @ w_alpha[d].astype(F32) + b_alpha[d].astype(F32)) / GLA_TAU
        la = la.reshape(B, T, GLA_HEADS, GLA_DK)
        if d == 0:
            o, s = gla_chunked(qf, kf, vf, la, s0[:, 0].astype(F32))
        else:
            o, s = gla_chunked(jnp.flip(qf, 1), jnp.flip(kf, 1), jnp.flip(vf, 1), jnp.flip(la, 1), s0[:, 1].astype(F32))
            o = jnp.flip(o, 1)
        outs.append(o)
        finals.append(s)
    return outs[0] + outs[1], jnp.stack(finals, axis=1)


def s5_scan(lam_dt, lam_bar, bu, x0):
    T = bu.shape[1]
    a = jnp.broadcast_to(lam_bar, bu.shape)

    def combine(e1, e2):
        a1, b1 = e1
        a2, b2 = e2
        return a2 * a1, a2 * b1 + b2

    _, xs = lax.associative_scan(combine, (a, bu), axis=1)
    powers = jnp.exp(lam_dt[None] * jnp.arange(1, T + 1, dtype=F32)[:, None, None])
    xs = xs + powers[None] * x0[:, None]
    return xs, xs[:, -1]


def s5_bidir(u, a_re, a_im, log_dt, b_re, b_im, c_re, c_im, d_skip, re0, im0):
    B, T, _ = u.shape
    uf = u.astype(F32)
    ug = uf.reshape(B, T, S5_GROUPS, S5_CH).astype(jnp.complex64)
    b_c = lax.complex(b_re.astype(F32), b_im.astype(F32))
    c_c = lax.complex(c_re.astype(F32), c_im.astype(F32))
    x0 = lax.complex(re0.astype(F32), im0.astype(F32))
    states = []
    finals = []
    for d in range(2):
        lam = lax.complex(a_re[d].astype(F32), a_im[d].astype(F32))
        lam_dt = lam * jnp.exp(log_dt[d].astype(F32))[:, None]
        lam_bar = jnp.exp(lam_dt)
        b_bar = ((lam_bar - 1.0) / lam)[..., None] * b_c
        bu = jnp.einsum('gpc,btgc->btgp', b_bar, ug)
        if d == 1:
            bu = jnp.flip(bu, 1)
        xs, last = s5_scan(lam_dt, lam_bar, bu, x0[:, d])
        if d == 1:
            xs = jnp.flip(xs, 1)
        states.append(xs)
        finals.append(last)
    y = jnp.real(jnp.einsum('gcp,btgp->btgc', c_c, states[0] + states[1])).reshape(B, T, S5_WIDTH)
    y = y + d_skip.astype(F32) * uf
    fin = jnp.stack(finals, axis=1)
    return y, jnp.real(fin), jnp.imag(fin)


def route(hf, router_w, router_bias):
    n = hf.shape[0]
    scores = jax.nn.sigmoid(hf.astype(F32) @ router_w.astype(F32))
    sel = scores + router_bias.astype(F32)
    grp = sel.reshape(n, N_EXPERT_GROUPS, EXPERTS_PER_GROUP)
    grp_score = jnp.sum(lax.top_k(grp, 2)[0], axis=-1)
    _, gidx = lax.top_k(grp_score, TOPK_GROUP)
    gmask = jnp.any(jnp.arange(N_EXPERT_GROUPS)[None, None, :] == gidx[:, :, None], axis=1)
    emask = jnp.repeat(gmask, EXPERTS_PER_GROUP, axis=1)
    _, eidx = lax.top_k(jnp.where(emask, sel, -jnp.inf), TOP_K)
    w = jnp.take_along_axis(scores, eidx, axis=1)
    w = w / jnp.sum(w, axis=-1, keepdims=True)
    return eidx, w


def moe_ffn(hf, eidx, w, w1, w3, w2):
    n, d = hf.shape
    nk = n * TOP_K
    flat_e = eidx.reshape(-1)
    flat_tok = jnp.repeat(jnp.arange(n, dtype=jnp.int32), TOP_K)
    flat_w = w.reshape(-1)
    order = jnp.argsort(flat_e)
    se, st, sw = flat_e[order], flat_tok[order], flat_w[order]
    counts = jnp.zeros((N_EXPERTS,), jnp.int32).at[flat_e].add(1)
    padded = ((counts + MOE_BLOCK - 1) // MOE_BLOCK) * MOE_BLOCK
    starts = jnp.cumsum(counts) - counts
    pends = jnp.cumsum(padded)
    pstarts = pends - padded
    dest = pstarts[se] + jnp.arange(nk, dtype=jnp.int32) - starts[se]
    cap = ((nk + MOE_BLOCK - 1) // MOE_BLOCK + N_EXPERTS) * MOE_BLOCK
    nblk = cap // MOE_BLOCK
    buf_tok = jnp.full((cap,), n, jnp.int32).at[dest].set(st)
    buf_w = jnp.zeros((cap,), F32).at[dest].set(sw)
    blk_e = jnp.minimum(jnp.searchsorted(pends, jnp.arange(nblk, dtype=jnp.int32) * MOE_BLOCK, side='right'), N_EXPERTS - 1)
    x_pad = jnp.concatenate([hf, jnp.zeros((1, d), hf.dtype)], axis=0)
    xin = x_pad[buf_tok].reshape(nblk, MOE_BLOCK, d)

    def expert_block(args):
        xb, e = args
        hid = jax.nn.silu(xb @ w1[e]) * (xb @ w3[e])
        return hid @ w2[e]

    y = lax.map(expert_block, (xin, blk_e)).reshape(cap, d).astype(F32) * buf_w[:, None]
    out = jnp.zeros((n + 1, d), F32).at[buf_tok].add(y)[:n]
    return out.astype(hf.dtype)


def token_mixers(h, lp, layer_idx, ctx):
    B, T, _ = h.shape
    dtype = h.dtype
    da_q, da_k, da_v, g_q, g_k, g_v, g_r, g_a, s5_u = split_columns(h @ lp['w_in'])
    q = da_q.reshape(B, T, DA_HEADS, 2, DA_HEAD)
    k = da_k.reshape(B, T, DA_HEADS, 2, DA_HEAD)
    v = da_v.reshape(B, T, DA_HEADS, DA_VDIM)
    if ctx is None:
        keys, vals = k, v
        gla_s0 = jnp.zeros((B, 2, GLA_HEADS, GLA_DK, GLA_DV), F32)
        s5_re0 = jnp.zeros((B, 2, S5_GROUPS, S5_STATE), F32)
        s5_im0 = jnp.zeros((B, 2, S5_GROUPS, S5_STATE), F32)
    else:
        ck, cv, gla_s0, s5_re0, s5_im0 = ctx
        n_ctx = ck.shape[1]
        q = axial_rope(q)
        keys = jnp.concatenate([ck.reshape(B, n_ctx, DA_HEADS, 2, DA_HEAD).astype(dtype), axial_rope(k)], axis=1)
        vals = jnp.concatenate([cv.astype(dtype), v], axis=1)
    lam, lam_init = diff_lambda(lp['da_lambda_q'], lp['da_lambda_k'], layer_idx)
    o_da = diff_attention(q, keys, vals, lam)
    o_da = (rmsnorm(o_da, lp['da_subln_g']) * (1.0 - lam_init)).reshape(B, T, DA_WIDTH)
    gq = g_q.reshape(B, T, GLA_HEADS, GLA_DK) * (GLA_DK ** -0.5)
    gk = g_k.reshape(B, T, GLA_HEADS, GLA_DK)
    gv = g_v.reshape(B, T, GLA_HEADS, GLA_DV)
    o_gla, gla_state = gla_bidir(gq, gk, gv, g_a, lp['gla_w_alpha'], lp['gla_b_alpha'], gla_s0)
    o_gla = rmsnorm(o_gla.astype(dtype), lp['gla_norm_g']) * jax.nn.silu(g_r.reshape(B, T, GLA_HEADS, GLA_DV))
    o_gla = o_gla.reshape(B, T, GLA_WIDTH)
    y, s5_re, s5_im = s5_bidir(s5_u, lp['s5_a_re'], lp['s5_a_im'], lp['s5_log_dt'], lp['s5_b_re'], lp['s5_b_im'],
                               lp['s5_c_re'], lp['s5_c_im'], lp['s5_d'], s5_re0, s5_im0)
    y = jax.nn.gelu(y.astype(dtype))
    o_s5 = y * jax.nn.sigmoid(y @ lp['s5_w_glu'] + lp['s5_b_glu'])
    gates = jax.nn.sigmoid(h @ lp['w_gate'] + lp['b_gate']).reshape(B, T, N_BRANCH, D_MODEL)
    wb = lp['w_branch']
    merged = gates[:, :, 0] * (o_da @ wb[0]) + gates[:, :, 1] * (o_gla @ wb[1]) + gates[:, :, 2] * (o_s5 @ wb[2])
    out = (merged @ lp['w_out']).astype(dtype)
    state = (k.reshape(B, T, DA_HEADS, DA_QK), v, gla_state, s5_re, s5_im)
    return out, state


def trunk_layer(x, cvec, lp, layer_idx, ctx, router_w, router_bias):
    m = (jax.nn.silu(cvec) @ lp['w_mod'] + lp['b_mod']).reshape(cvec.shape[0], 1, N_MOD, D_MODEL)
    shift1, scale1, gate1 = m[:, :, 0], m[:, :, 1], m[:, :, 2]
    shift2, scale2, gate2 = m[:, :, 3], m[:, :, 4], m[:, :, 5]
    h = rmsnorm(x, lp['g_norm'][0]) * (1.0 + scale1) + shift1
    mix, state = token_mixers(h, lp, layer_idx, ctx)
    x = x + gate1 * mix
    h2 = rmsnorm(x, lp['g_norm'][1]) * (1.0 + scale2) + shift2
    hf = h2.reshape(-1, D_MODEL)
    eidx, w = route(hf, router_w, router_bias)
    moe = moe_ffn(hf, eidx, w, lp['moe_w1'], lp['moe_w3'], lp['moe_w2'])
    x = x + gate2 * moe.reshape(x.shape)
    return x, state


def setup_inputs(seed: int = 0) -> dict:
    key = jax.random.key(seed)
    ks = jax.random.split(key, 40)
    D = D_MODEL

    def nrm(k, shape, scale):
        return jax.random.normal(k, shape, F32) * scale

    inp = {}
    inp['x_prompt'] = nrm(ks[0], (BATCH, SEQ, D), 1.0)
    inp['x_sample'] = nrm(ks[1], (DEC_BATCH, DEC_SEQ, D), 1.0)
    inp['cache_k'] = nrm(ks[2], (DEC_BATCH, DEPTH, PAST_LEN, DA_HEADS, DA_QK), 1.0)
    inp['cache_v'] = nrm(ks[3], (DEC_BATCH, DEPTH, PAST_LEN, DA_HEADS, DA_VDIM), 1.0)
    inp['state_gla'] = nrm(ks[4], (DEC_BATCH, DEPTH, 2, GLA_HEADS, GLA_DK, GLA_DV), 1.0)
    inp['state_s5_re'] = nrm(ks[5], (DEC_BATCH, DEPTH, 2, S5_GROUPS, S5_STATE), 0.5)
    inp['state_s5_im'] = nrm(ks[6], (DEC_BATCH, DEPTH, 2, S5_GROUPS, S5_STATE), 0.5)
    inp['c'] = nrm(ks[7], (DEC_BATCH, D), 1.0)
    inp['c_ctx'] = nrm(ks[8], (D,), 1.0)
    inp['w_mod'] = nrm(ks[9], (DEPTH, D, N_MOD * D), 0.3 * D ** -0.5)
    inp['b_mod'] = nrm(ks[10], (DEPTH, N_MOD * D), 0.01)
    inp['g_norm'] = 1.0 + nrm(ks[11], (DEPTH, 2, D), 0.02)
    inp['w_in'] = nrm(ks[12], (DEPTH, D, IN_COLS), D ** -0.5)
    inp['da_lambda_q'] = nrm(ks[13], (DEPTH, 2, DA_HEAD), 0.1)
    inp['da_lambda_k'] = nrm(ks[14], (DEPTH, 2, DA_HEAD), 0.1)
    inp['da_subln_g'] = 1.0 + nrm(ks[15], (DEPTH, DA_VDIM), 0.02)
    inp['gla_w_alpha'] = nrm(ks[16], (DEPTH, 2, GLA_RANK, GLA_KTOT), GLA_RANK ** -0.5)
    inp['gla_b_alpha'] = nrm(ks[17], (DEPTH, 2, GLA_KTOT), 0.1)
    inp['gla_norm_g'] = 1.0 + nrm(ks[18], (DEPTH, GLA_DV), 0.02)
    inp['s5_a_re'] = -0.5 + nrm(ks[19], (DEPTH, 2, S5_GROUPS, S5_STATE), 0.01)
    inp['s5_a_im'] = jnp.pi * jnp.arange(S5_STATE, dtype=F32) + nrm(ks[20], (DEPTH, 2, S5_GROUPS, S5_STATE), 0.01)
    inp['s5_log_dt'] = jax.random.uniform(ks[21], (DEPTH, 2, S5_GROUPS), F32, minval=math.log(1e-3), maxval=math.log(1e-1))
    inp['s5_b_re'] = nrm(ks[22], (DEPTH, S5_GROUPS, S5_STATE, S5_CH), (2.0 * S5_CH) ** -0.5)
    inp['s5_b_im'] = nrm(ks[23], (DEPTH, S5_GROUPS, S5_STATE, S5_CH), (2.0 * S5_CH) ** -0.5)
    inp['s5_c_re'] = nrm(ks[24], (DEPTH, S5_GROUPS, S5_CH, S5_STATE), (2.0 * S5_STATE) ** -0.5)
    inp['s5_c_im'] = nrm(ks[25], (DEPTH, S5_GROUPS, S5_CH, S5_STATE), (2.0 * S5_STATE) ** -0.5)
    inp['s5_d'] = nrm(ks[26], (DEPTH, S5_WIDTH), 1.0)
    inp['s5_w_glu'] = nrm(ks[27], (DEPTH, S5_WIDTH, S5_WIDTH), S5_WIDTH ** -0.5)
    inp['s5_b_glu'] = nrm(ks[28], (DEPTH, S5_WIDTH), 0.01)
    inp['w_branch'] = nrm(ks[29], (DEPTH, N_BRANCH, DA_WIDTH, D), DA_WIDTH ** -0.5)
    inp['w_gate'] = nrm(ks[30], (DEPTH, D, N_BRANCH * D), D ** -0.5)
    inp['b_gate'] = nrm(ks[31], (DEPTH, N_BRANCH * D), 0.01)
    inp['w_out'] = nrm(ks[32], (DEPTH, D, D), D ** -0.5)
    inp['router_w'] = nrm(ks[33], (D, N_EXPERTS), D ** -0.5)
    inp['router_bias'] = nrm(ks[34], (N_EXPERTS,), 0.01)
    inp['moe_w1'] = nrm(ks[35], (DEPTH, N_EXPERTS, D, D_FF_EXPERT), D ** -0.5)
    inp['moe_w3'] = nrm(ks[36], (DEPTH, N_EXPERTS, D, D_FF_EXPERT), D ** -0.5)
    inp['moe_w2'] = nrm(ks[37], (DEPTH, N_EXPERTS, D_FF_EXPERT, D), D_FF_EXPERT ** -0.5)
    inp['g_final'] = 1.0 + nrm(ks[38], (D,), 0.02)
    return inp


def reference(x_prompt, x_sample, cache_k, cache_v, state_gla, state_s5_re, state_s5_im, c, c_ctx,
              w_mod, b_mod, g_norm, w_in, da_lambda_q, da_lambda_k, da_subln_g, gla_w_alpha, gla_b_alpha,
              gla_norm_g, s5_a_re, s5_a_im, s5_log_dt, s5_b_re, s5_b_im, s5_c_re, s5_c_im, s5_d, s5_w_glu,
              s5_b_glu, w_branch, w_gate, b_gate, w_out, router_w, router_bias, moe_w1, moe_w3, moe_w2, g_final):
    lps = []
    for l in range(DEPTH):
        lps.append({
            'w_mod': w_mod[l], 'b_mod': b_mod[l], 'g_norm': g_norm[l], 'w_in': w_in[l],
            'da_lambda_q': da_lambda_q[l], 'da_lambda_k': da_lambda_k[l], 'da_subln_g': da_subln_g[l],
            'gla_w_alpha': gla_w_alpha[l], 'gla_b_alpha': gla_b_alpha[l], 'gla_norm_g': gla_norm_g[l],
            's5_a_re': s5_a_re[l], 's5_a_im': s5_a_im[l], 's5_log_dt': s5_log_dt[l],
            's5_b_re': s5_b_re[l], 's5_b_im': s5_b_im[l], 's5_c_re': s5_c_re[l], 's5_c_im': s5_c_im[l],
            's5_d': s5_d[l], 's5_w_glu': s5_w_glu[l], 's5_b_glu': s5_b_glu[l],
            'w_branch': w_branch[l], 'w_gate': w_gate[l], 'b_gate': b_gate[l], 'w_out': w_out[l],
            'moe_w1': moe_w1[l], 'moe_w3': moe_w3[l], 'moe_w2': moe_w2[l],
        })
    xp = x_prompt
    k_list, v_list, g_list, re_list, im_list = [], [], [], [], []
    for l in range(DEPTH):
        xp, (k_l, v_l, g_l, re_l, im_l) = trunk_layer(xp, c_ctx[None, :], lps[l], l, None, router_w, router_bias)
        k_list.append(k_l)
        v_list.append(v_l)
        g_list.append(g_l)
        re_list.append(re_l)
        im_list.append(im_l)
    y_prompt = rmsnorm(xp, g_final)
    xs = x_sample
    for l in range(DEPTH):
        ctx = (cache_k[:, l], cache_v[:, l], state_gla[:, l], state_s5_re[:, l], state_s5_im[:, l])
        xs, _ = trunk_layer(xs, c, lps[l], l, ctx, router_w, router_bias)
    y_sample = rmsnorm(xs, g_final)
    new_cache_k = jnp.stack(k_list, axis=1)
    new_cache_v = jnp.stack(v_list, axis=1)
    new_state_gla = jnp.stack(g_list, axis=1)
    new_state_s5_re = jnp.stack(re_list, axis=1)
    new_state_s5_im = jnp.stack(im_list, axis=1)
    return (y_prompt, y_sample, new_cache_k, new_cache_v, new_state_gla, new_state_s5_re, new_state_s5_im)
```

```python
import functools
import math

import jax
import jax.numpy as jnp
from jax import lax
from jax.experimental import pallas as pl
from jax.experimental.pallas import tpu as pltpu

F32 = jnp.float32
BF16 = jnp.bfloat16

D_MODEL = 2048
GRID_W = 64
EPS = 1e-6
N_MOD = 6
N_BRANCH = 3
DA_HEADS = 8
DA_HEAD = 64
DA_VDIM = 128
ROPE_BASE = 10000.0
AX_DIM = DA_HEAD // 2
GLA_HEADS = 4
GLA_DK = 128
GLA_DV = 256
GLA_KTOT = GLA_HEADS * GLA_DK
GLA_WIDTH = GLA_HEADS * GLA_DV
GLA_RANK = 16
GLA_TAU = 16.0
GLA_CHUNK = 64
S5_WIDTH = 1024
S5_CH = 16
S5_GROUPS = S5_WIDTH // S5_CH
S5_STATE = 64
N_EXPERTS = 16
N_EXPERT_GROUPS = 4
EXPERTS_PER_GROUP = N_EXPERTS // N_EXPERT_GROUPS
TOPK_GROUP = 1
TOP_K = 2
D_FF_EXPERT = 1408

LANE = 128
C_Q, C_K, C_V, C_GQ, C_GK, C_GV, C_GR, C_S5, C_GA = 0, 1024, 2048, 3072, 3584, 4096, 5120, 6144, 7168
P_COLS = 7680
GLA_ROWS = 256
S5_L = 16
S5_PAIRS = S5_GROUPS // 2
S5_PAIRS_PER_STEP = 4
MOE_TM = 256
VMEM_MB = 1024 * 1024


def _cp(sem, vmem_mb=48):
    return pltpu.CompilerParams(dimension_semantics=sem, vmem_limit_bytes=vmem_mb * VMEM_MB)


def _pick(pref, *ns):
    t = pref
    while any(n % t for n in ns):
        t //= 2
    return t


def _mod_index(i, tm, n_ctx_tok, dec_seq):
    t0 = i * tm
    return jnp.where(t0 < n_ctx_tok, 0, 1 + (t0 - n_ctx_tok) // dec_seq)


def _mod_kernel(c_ref, w_ref, b_ref, o_ref):
    c = c_ref[...]
    s = (c * jax.nn.sigmoid(c)).astype(BF16)
    o_ref[0] = jnp.dot(s, w_ref[0].astype(BF16), preferred_element_type=F32) + b_ref[0]


def _mod(cv, w_mod, b_mod):
    L, D, M = w_mod.shape
    tn = 1024
    return pl.pallas_call(
        _mod_kernel,
        grid=(L, M // tn),
        in_specs=[pl.BlockSpec((8, D), lambda l, j: (0, 0)),
                  pl.BlockSpec((1, D, tn), lambda l, j: (l, 0, j)),
                  pl.BlockSpec((1, 1, tn), lambda l, j: (l, 0, j))],
        out_specs=pl.BlockSpec((1, 8, tn), lambda l, j: (l, 0, j)),
        out_shape=jax.ShapeDtypeStruct((L, 8, M), F32),
        compiler_params=_cp(("parallel", "parallel")),
        name="mod_vectors",
    )(cv, w_mod, b_mod.reshape(L, 1, M))


def _norm_core(x_ref, g_ref, sc_ref, sh_ref):
    x = x_ref[...]
    y = x * lax.rsqrt(jnp.mean(x * x, axis=-1, keepdims=True) + EPS)
    return (y * g_ref[...]) * (1.0 + sc_ref[0]) + sh_ref[0]


def _norm_kernel(x_ref, g_ref, sc_ref, sh_ref, o_ref):
    o_ref[...] = _norm_core(x_ref, g_ref, sc_ref, sh_ref).astype(o_ref.dtype)


def _norm_router_kernel(x_ref, g_ref, sc_ref, sh_ref, rh_ref, rl_ref, o_ref, lg_ref):
    h = _norm_core(x_ref, g_ref, sc_ref, sh_ref)
    hh = h.astype(BF16)
    hl = (h - hh.astype(F32)).astype(BF16)
    o_ref[...] = hh
    lg_ref[...] = (jnp.dot(hh, rh_ref[...], preferred_element_type=F32)
                   + jnp.dot(hl, rh_ref[...], preferred_element_type=F32)
                   + jnp.dot(hh, rl_ref[...], preferred_element_type=F32))


def _norm(x, g, sc, sh, midx, tm, out_dtype, router=None):
    N, D = x.shape
    in_specs = [pl.BlockSpec((tm, D), lambda i: (i, 0)),
                pl.BlockSpec((1, D), lambda i: (0, 0)),
                pl.BlockSpec((1, 1, D), lambda i: (midx(i), 0, 0)),
                pl.BlockSpec((1, 1, D), lambda i: (midx(i), 0, 0))]
    if router is None:
        return pl.pallas_call(
            _norm_kernel, grid=(N // tm,), in_specs=in_specs,
            out_specs=pl.BlockSpec((tm, D), lambda i: (i, 0)),
            out_shape=jax.ShapeDtypeStruct((N, D), out_dtype),
            compiler_params=_cp(("parallel",)), name="norm_mod",
        )(x, g, sc, sh)
    rh, rl = router
    in_specs += [pl.BlockSpec((D, LANE), lambda i: (0, 0)), pl.BlockSpec((D, LANE), lambda i: (0, 0))]
    return pl.pallas_call(
        _norm_router_kernel, grid=(N // tm,), in_specs=in_specs,
        out_specs=[pl.BlockSpec((tm, D), lambda i: (i, 0)), pl.BlockSpec((tm, LANE), lambda i: (i, 0))],
        out_shape=[jax.ShapeDtypeStruct((N, D), BF16), jax.ShapeDtypeStruct((N, LANE), F32)],
        compiler_params=_cp(("parallel",)), name="norm_mod_router",
    )(x, g, sc, sh, rh, rl)


def _mm_kernel(a_ref, w_ref, o_ref):
    o_ref[...] = jnp.dot(a_ref[...], w_ref[...], preferred_element_type=F32).astype(o_ref.dtype)


def _matmul(a, w, tm, tn, out_dtype, name):
    M, K = a.shape
    Nc = w.shape[1]
    return pl.pallas_call(
        _mm_kernel, grid=(Nc // tn, M // tm),
        in_specs=[pl.BlockSpec((tm, K), lambda j, i: (i, 0)), pl.BlockSpec((K, tn), lambda j, i: (0, j))],
        out_specs=pl.BlockSpec((tm, tn), lambda j, i: (i, j)),
        out_shape=jax.ShapeDtypeStruct((M, Nc), out_dtype),
        compiler_params=_cp(("parallel", "parallel")), name=name,
    )(a, w)


def _rope(x, cos, sin_signed):
    lane = lax.broadcasted_iota(jnp.int32, (1, LANE), 1)
    first = (lane % (2 * (AX_DIM // 2))) < (AX_DIM // 2)
    xr = jnp.where(first, pltpu.roll(x, LANE - AX_DIM // 2, 1), pltpu.roll(x, AX_DIM // 2, 1))
    return x * cos + xr * sin_signed


def _attn_kernel(*refs, rope, ctx):
    refs = list(refs)
    q_ref, k_ref, v_ref = refs[:3]
    refs = refs[3:]
    if ctx:
        ck_ref, cv_ref = refs[:2]
        refs = refs[2:]
    if rope:
        cq_ref, sq_ref, ckk_ref, skk_ref = refs[:4]
        refs = refs[4:]
    lam_ref, g_ref, o_ref, kb_ref, vb_ref = refs

    @pl.when(pl.program_id(2) == 0)
    def _():
        k = k_ref[...]
        if rope:
            k = _rope(k, ckk_ref[...], skk_ref[...])
        kb_ref[...] = k.astype(BF16)
        vb_ref[...] = v_ref[...].astype(BF16)

    q = q_ref[...]
    if rope:
        q = _rope(q, cq_ref[...], sq_ref[...])
    q = q * (DA_HEAD ** -0.5)
    lane = lax.broadcasted_iota(jnp.int32, (1, LANE), 1)
    lo = lane < DA_HEAD
    q0 = jnp.where(lo, q, 0.0).astype(BF16)
    q1 = jnp.where(lo, 0.0, q).astype(BF16)
    nt = (((1,), (1,)), ((), ()))
    kb = kb_ref[...]
    s0 = lax.dot_general(q0, kb, nt, preferred_element_type=F32)
    s1 = lax.dot_general(q1, kb, nt, preferred_element_type=F32)
    m0 = jnp.max(s0, axis=-1, keepdims=True)
    m1 = jnp.max(s1, axis=-1, keepdims=True)
    if ctx:
        ckb = ck_ref[...].astype(BF16)
        c0 = lax.dot_general(q0, ckb, nt, preferred_element_type=F32)
        c1 = lax.dot_general(q1, ckb, nt, preferred_element_type=F32)
        m0 = jnp.maximum(m0, jnp.max(c0, axis=-1, keepdims=True))
        m1 = jnp.maximum(m1, jnp.max(c1, axis=-1, keepdims=True))
    e0 = jnp.exp(s0 - m0)
    e1 = jnp.exp(s1 - m1)
    l0 = jnp.sum(e0, axis=-1, keepdims=True)
    l1 = jnp.sum(e1, axis=-1, keepdims=True)
    if ctx:
        ec0 = jnp.exp(c0 - m0)
        ec1 = jnp.exp(c1 - m1)
        l0 = l0 + jnp.sum(ec0, axis=-1, keepdims=True)
        l1 = l1 + jnp.sum(ec1, axis=-1, keepdims=True)
    a0 = 1.0 / l0
    a1 = lam_ref[:, :1] / l1
    o = jnp.dot((e0 * a0 - e1 * a1).astype(BF16), vb_ref[...], preferred_element_type=F32)
    if ctx:
        o = o + jnp.dot((ec0 * a0 - ec1 * a1).astype(BF16), cv_ref[...].astype(BF16), preferred_element_type=F32)
    y = o * lax.rsqrt(jnp.mean(o * o, axis=-1, keepdims=True) + EPS)
    o_ref[...] = (y * g_ref[...]).astype(o_ref.dtype)


def _attention(p, lam_row, g_row, row0, B, T, tq, rope_tabs=None, ctx=None):
    nq = T // tq
    rb0 = row0 // tq
    kb0 = row0 // T
    hq, hk, hv = C_Q // LANE, C_K // LANE, C_V // LANE
    in_specs = [pl.BlockSpec((tq, LANE), lambda b, h, i: (rb0 + b * nq + i, hq + h)),
                pl.BlockSpec((T, LANE), lambda b, h, i: (kb0 + b, hk + h)),
                pl.BlockSpec((T, LANE), lambda b, h, i: (kb0 + b, hv + h))]
    args = [p, p, p]
    if ctx is not None:
        ck, cv, layer = ctx
        past = ck.shape[2]
        in_specs += [pl.BlockSpec((None, None, past, LANE), lambda b, h, i: (b, layer, 0, h)),
                     pl.BlockSpec((None, None, past, LANE), lambda b, h, i: (b, layer, 0, h))]
        args += [ck, cv]
    if rope_tabs is not None:
        cos, sin = rope_tabs
        in_specs += [pl.BlockSpec((tq, LANE), lambda b, h, i: (i, 0)), pl.BlockSpec((tq, LANE), lambda b, h, i: (i, 0)),
                     pl.BlockSpec((T, LANE), lambda b, h, i: (0, 0)), pl.BlockSpec((T, LANE), lambda b, h, i: (0, 0))]
        args += [cos, sin, cos, sin]
    in_specs += [pl.BlockSpec((1, LANE), lambda b, h, i: (0, 0)), pl.BlockSpec((1, LANE), lambda b, h, i: (0, 0))]
    args += [lam_row, g_row]
    return pl.pallas_call(
        functools.partial(_attn_kernel, rope=rope_tabs is not None, ctx=ctx is not None),
        grid=(B, DA_HEADS, nq),
        in_specs=in_specs,
        out_specs=pl.BlockSpec((tq, LANE), lambda b, h, i: (b * nq + i, h)),
        out_shape=jax.ShapeDtypeStruct((B * T, DA_HEADS * DA_VDIM), BF16),
        scratch_shapes=[pltpu.VMEM((T, LANE), BF16), pltpu.VMEM((T, LANE), BF16)],
        compiler_params=_cp(("parallel", "parallel", "arbitrary"), 56),
        name="diff_attention_latent" if ctx is not None else "diff_attention_context",
    )(*args)


def _log_sigmoid(z):
    return jnp.minimum(z, 0.0) - jnp.log1p(jnp.exp(-jnp.abs(z)))


def _gla_direction(d, q_ref, k_ref, v_ref, ga_ref, wa_ref, ba_ref, tri_ref, o_ref, st_ref):
    C = GLA_CHUNK
    nsub = GLA_ROWS // C
    z = jnp.dot(ga_ref[...].astype(BF16), wa_ref[0, d], preferred_element_type=F32) + ba_ref[0, d]
    la = _log_sigmoid(z) / GLA_TAU
    hi = la.astype(BF16)
    r1 = la - hi.astype(F32)
    mid = r1.astype(BF16)
    low = (r1 - mid.astype(F32)).astype(BF16)
    tri = tri_ref[d]
    bsum = (jnp.dot(tri, hi, preferred_element_type=F32) + jnp.dot(tri, mid, preferred_element_type=F32)
            + jnp.dot(tri, low, preferred_element_type=F32))
    row = lax.broadcasted_iota(jnp.int32, (C, C), 0)
    col = lax.broadcasted_iota(jnp.int32, (C, C), 1)
    mask = (row >= col) if d == 0 else (row <= col)
    nt = (((1,), (1,)), ((), ()))
    tn = (((0,), (0,)), ((), ()))
    subs = range(nsub) if d == 0 else range(nsub - 1, -1, -1)
    for h in range(GLA_HEADS):
        hk = slice(h * GLA_DK, (h + 1) * GLA_DK)
        hv = slice(h * GLA_DV, (h + 1) * GLA_DV)
        st = st_ref[d, h]
        for s in subs:
            rs = slice(s * C, (s + 1) * C)
            end = s * C + C - 1 if d == 0 else s * C
            bb = bsum[rs, hk]
            bl = bsum[end:end + 1, hk]
            q = q_ref[rs, hk] * (GLA_DK ** -0.5)
            k = k_ref[rs, hk]
            v = v_ref[rs, hv].astype(BF16)
            qt = (q * jnp.exp(bb)).astype(BF16)
            kt = (k * jnp.exp(-bb)).astype(BF16)
            kend = (k * jnp.exp(bl - bb)).astype(BF16)
            att = jnp.where(mask, lax.dot_general(qt, kt, nt, preferred_element_type=F32), 0.0).astype(BF16)
            o = jnp.dot(att, v, preferred_element_type=F32)
            o = o + lax.dot_general(qt, st.astype(BF16), nt, preferred_element_type=F32)
            o_ref[rs, hv] = o
            st = st * jnp.exp(bl) + lax.dot_general(v, kend, tn, preferred_element_type=F32)
        st_ref[d, h] = st


def _gla_kernel(qf, kf, vf, gaf, qb, kb, vb, gab, wa_ref, ba_ref, tri_ref, s0_ref, of_ref, ob_ref, fin_ref, st_ref,
                *, seq_pos):
    first, last = seq_pos(pl.program_id(0))

    @pl.when(first)
    def _():
        for d in range(2):
            for h in range(GLA_HEADS):
                st_ref[d, h] = s0_ref[0, d, h].T

    _gla_direction(0, qf, kf, vf, gaf, wa_ref, ba_ref, tri_ref, of_ref, st_ref)
    _gla_direction(1, qb, kb, vb, gab, wa_ref, ba_ref, tri_ref, ob_ref, st_ref)

    @pl.when(last)
    def _():
        for d in range(2):
            for h in range(GLA_HEADS):
                fin_ref[0, d, h] = st_ref[d, h].T


def _gla(p, wa, ba, tri, s0_all, seqs):
    N = p.shape[0]
    R = GLA_ROWS
    (n1, t1), (n2, t2) = seqs
    nb1, nb2 = t1 // R, t2 // R
    steps1 = n1 * nb1

    def locate(s):
        in1 = s < steps1
        s2 = s - steps1
        nb = jnp.where(in1, nb1, nb2)
        seq = jnp.where(in1, s // nb1, n1 + s2 // nb2)
        n = jnp.where(in1, s % nb1, s2 % nb2)
        base = jnp.where(in1, (s // nb1) * nb1, steps1 + (s2 // nb2) * nb2)
        return seq, n, nb, base

    def fwd_blk(s):
        _, n, _, base = locate(s)
        return base + n

    def bwd_blk(s):
        _, n, nb, base = locate(s)
        return base + nb - 1 - n

    def seq_pos(s):
        _, n, nb, _ = locate(s)
        return n == 0, n == nb - 1

    def specs(blk):
        return [pl.BlockSpec((R, GLA_KTOT), lambda s: (blk(s), C_GQ // GLA_KTOT)),
                pl.BlockSpec((R, GLA_KTOT), lambda s: (blk(s), C_GK // GLA_KTOT)),
                pl.BlockSpec((R, GLA_WIDTH), lambda s: (blk(s), C_GV // GLA_WIDTH)),
                pl.BlockSpec((R, LANE), lambda s: (blk(s), C_GA // LANE))]

    state_spec = pl.BlockSpec((1, 2, GLA_HEADS, GLA_DK, GLA_DV), lambda s: (locate(s)[0], 0, 0, 0, 0))
    nseq = n1 + n2
    return pl.pallas_call(
        functools.partial(_gla_kernel, seq_pos=seq_pos),
        grid=(N // R,),
        in_specs=specs(fwd_blk) + specs(bwd_blk) + [
            pl.BlockSpec((1, 2, LANE, GLA_KTOT), lambda s: (0, 0, 0, 0)),
            pl.BlockSpec((1, 2, 1, GLA_KTOT), lambda s: (0, 0, 0, 0)),
            pl.BlockSpec((2, R, R), lambda s: (0, 0, 0)),
            state_spec],
        out_specs=[pl.BlockSpec((R, GLA_WIDTH), lambda s: (fwd_blk(s), 0)),
                   pl.BlockSpec((R, GLA_WIDTH), lambda s: (bwd_blk(s), 0)),
                   state_spec],
        out_shape=[jax.ShapeDtypeStruct((N, GLA_WIDTH), F32), jax.ShapeDtypeStruct((N, GLA_WIDTH), F32),
                   jax.ShapeDtypeStruct((nseq, 2, GLA_HEADS, GLA_DK, GLA_DV), F32)],
        scratch_shapes=[pltpu.VMEM((2, GLA_HEADS, GLA_DV, GLA_DK), F32)],
        compiler_params=_cp(("arbitrary",)),
        name="gla_bidir",
    )(p, p, p, p, p, p, p, p, wa, ba, tri, s0_all)


def _gla_post_kernel(of_ref, ob_ref, r_ref, g_ref, o_ref):
    for h in range(GLA_HEADS):
        hv = slice(h * GLA_DV, (h + 1) * GLA_DV)
        o = of_ref[:, hv] + ob_ref[:, hv]
        y = o * lax.rsqrt(jnp.mean(o * o, axis=-1, keepdims=True) + EPS) * g_ref[...]
        r = r_ref[:, hv]
        o_ref[:, hv] = (y * (r * jax.nn.sigmoid(r))).astype(o_ref.dtype)


def _gla_post(of, ob, p, g, tm):
    N = of.shape[0]
    return pl.pallas_call(
        _gla_post_kernel, grid=(N // tm,),
        in_specs=[pl.BlockSpec((tm, GLA_WIDTH), lambda i: (i, 0)), pl.BlockSpec((tm, GLA_WIDTH), lambda i: (i, 0)),
                  pl.BlockSpec((tm, GLA_WIDTH), lambda i: (i, C_GR // GLA_WIDTH)),
                  pl.BlockSpec((1, GLA_DV), lambda i: (0, 0))],
        out_specs=pl.BlockSpec((tm, GLA_WIDTH), lambda i: (i, 0)),
        out_shape=jax.ShapeDtypeStruct((N, GLA_WIDTH), BF16),
        compiler_params=_cp(("parallel",)), name="gla_norm_gate",
    )(of, ob, p, g)


def _s5_matrices(a_re, a_im, log_dt, b_re, b_im, c_re, c_im):
    hp = lax.Precision.HIGHEST
    L, G, P, CH = S5_L, S5_GROUPS, S5_STATE, S5_CH
    b_c = lax.complex(b_re, b_im)
    c_c = lax.complex(c_re, c_im)
    taus = jnp.arange(L + 1, dtype=F32)
    intra = jnp.zeros((G, L, CH, L, CH), F32)
    ii = jnp.arange(L)
    dist = ii[None, :] - ii[:, None]
    state, carry, decay = [], [], []
    for d in range(2):
        lam = lax.complex(a_re[d], a_im[d])
        lam_dt = lam * jnp.exp(log_dt[d])[:, None]
        lam_bar = jnp.exp(lam_dt)
        b_bar = ((lam_bar - 1.0) / lam)[..., None] * b_c
        pw = jnp.exp(lam_dt[None] * taus[:, None, None])
        kern = jnp.real(jnp.einsum('gop,tgp,gpc->tgco', c_c, pw[:L], b_bar, precision=hp))
        dd = dist if d == 0 else -dist
        sel = kern[jnp.clip(dd, 0, L - 1)]
        sel = jnp.where((dd >= 0)[:, :, None, None, None], sel, 0.0)
        intra = intra + sel.transpose(2, 0, 3, 1, 4)
        e = (L - 1 - ii) if d == 0 else ii
        st = pw[e][:, :, :, None] * b_bar[None]
        st = st.transpose(1, 0, 3, 2).reshape(G, L * CH, P)
        state.append(st)
        e = (ii + 1) if d == 0 else (L - ii)
        cw = c_c[None] * pw[e][:, :, None, :]
        cw = cw.transpose(1, 3, 0, 2).reshape(G, P, L * CH)
        carry.append(cw)
        decay.append(pw[L])
    intra = intra.reshape(G, L * CH, L * CH)

    def pair_cols(m):
        K = m.shape[1]
        m = m.reshape(S5_PAIRS, 2, K, P)
        z = jnp.zeros((S5_PAIRS, K, P), m.dtype)
        top = jnp.concatenate([m[:, 0], z], axis=2)
        bot = jnp.concatenate([z, m[:, 1]], axis=2)
        return jnp.concatenate([top, bot], axis=1)

    def pair_rows(m):
        K = m.shape[2]
        m = m.reshape(S5_PAIRS, 2, P, K)
        z = jnp.zeros((S5_PAIRS, P, K), m.dtype)
        top = jnp.concatenate([m[:, 0], z], axis=2)
        bot = jnp.concatenate([z, m[:, 1]], axis=2)
        return jnp.concatenate([top, bot], axis=1)

    ms = jnp.stack([jnp.stack([pair_cols(jnp.real(s)), pair_cols(jnp.imag(s))]) for s in state])
    mc = jnp.stack([jnp.stack([pair_rows(jnp.real(c)), pair_rows(-jnp.imag(c))]) for c in carry])
    dec = jnp.stack([jnp.stack([jnp.real(x), jnp.imag(x)]) for x in decay])
    dec = dec.reshape(2, 2, S5_PAIRS, 1, 2 * P)
    return intra.astype(BF16), ms.astype(BF16), mc.astype(BF16), dec


def _s5_kernel(u_ref, mi_ref, ms_ref, mc_ref, dec_ref, x0_ref, y_ref, fin_ref, z_ref, *, segments):
    PP = S5_PAIRS_PER_STEP
    W = 2 * S5_L * S5_CH
    for pp in range(PP):
        u = u_ref[:, pp * W:(pp + 1) * W]
        for d in range(2):
            for c in range(2):
                z_ref[pp, d, c] = jnp.dot(u, ms_ref[d, c, pp], preferred_element_type=F32)

    chains = [(pp, d) for pp in range(PP) for d in range(2)]
    dec = {(pp, d): (dec_ref[d, 0, pp], dec_ref[d, 1, pp]) for (pp, d) in chains}
    seq_base = 0
    for (row0, n_seq, n_chunk) in segments:
        def seq_loop(sq, carry, row0=row0, n_chunk=n_chunk, seq_base=seq_base):
            r0 = row0 + sq * n_chunk
            sid = seq_base + sq
            x_init = tuple((x0_ref[0, pp, pl.ds(2 * sid + d, 1), :], x0_ref[1, pp, pl.ds(2 * sid + d, 1), :])
                           for (pp, d) in chains)

            def step(n, xs):
                out = []
                for (pp, d), (xr, xi) in zip(chains, xs):
                    ar, ai = dec[(pp, d)]
                    r = r0 + (n if d == 0 else n_chunk - 1 - n)
                    zr = z_ref[pp, d, 0, pl.ds(r, 1), :]
                    zi = z_ref[pp, d, 1, pl.ds(r, 1), :]
                    z_ref[pp, d, 0, pl.ds(r, 1), :] = xr
                    z_ref[pp, d, 1, pl.ds(r, 1), :] = xi
                    out.append((ar * xr - ai * xi + zr, ar * xi + ai * xr + zi))
                return tuple(out)

            xs = lax.fori_loop(0, n_chunk, step, x_init)
            for (pp, d), (xr, xi) in zip(chains, xs):
                fin_ref[0, pp, pl.ds(2 * sid + d, 1), :] = xr
                fin_ref[1, pp, pl.ds(2 * sid + d, 1), :] = xi
            return carry

        lax.fori_loop(0, n_seq, seq_loop, 0)
        seq_base += n_seq

    G2 = S5_L * S5_CH
    for pp in range(PP):
        y = jnp.dot(z_ref[pp, 0, 0].astype(BF16), mc_ref[0, 0, pp], preferred_element_type=F32)
        y = y + jnp.dot(z_ref[pp, 0, 1].astype(BF16), mc_ref[0, 1, pp], preferred_element_type=F32)
        y = y + jnp.dot(z_ref[pp, 1, 0].astype(BF16), mc_ref[1, 0, pp], preferred_element_type=F32)
        y = y + jnp.dot(z_ref[pp, 1, 1].astype(BF16), mc_ref[1, 1, pp], preferred_element_type=F32)
        for g in range(2):
            cs = slice(pp * W + g * G2, pp * W + (g + 1) * G2)
            yi = jnp.dot(u_ref[:, cs], mi_ref[2 * pp + g], preferred_element_type=F32)
            y_ref[:, cs] = yi + y[:, g * G2:(g + 1) * G2]


def _s5(uc, mats, x0, segments):
    mi, ms, mc, dec = mats
    NC = uc.shape[0]
    PP = S5_PAIRS_PER_STEP
    W = 2 * S5_L * S5_CH
    G2 = S5_L * S5_CH
    nrow = x0.shape[2]
    state_spec = pl.BlockSpec((2, PP, nrow, LANE), lambda j: (0, j, 0, 0))
    return pl.pallas_call(
        functools.partial(_s5_kernel, segments=segments),
        grid=(S5_PAIRS // PP,),
        in_specs=[pl.BlockSpec((NC, PP * W), lambda j: (0, j)),
                  pl.BlockSpec((2 * PP, G2, G2), lambda j: (j, 0, 0)),
                  pl.BlockSpec((2, 2, PP, W, LANE), lambda j: (0, 0, j, 0, 0)),
                  pl.BlockSpec((2, 2, PP, LANE, W), lambda j: (0, 0, j, 0, 0)),
                  pl.BlockSpec((2, 2, PP, 1, LANE), lambda j: (0, 0, j, 0, 0)),
                  state_spec],
        out_specs=[pl.BlockSpec((NC, PP * W), lambda j: (0, j)), state_spec],
        out_shape=[jax.ShapeDtypeStruct((NC, S5_GROUPS * G2), F32),
                   jax.ShapeDtypeStruct((2, S5_PAIRS, nrow, LANE), F32)],
        scratch_shapes=[pltpu.VMEM((PP, 2, 2, NC, LANE), F32)],
        compiler_params=_cp(("parallel",), 56),
        name="s5_bidir",
    )(uc, mi, ms, mc, dec, x0)


def _s5_post_kernel(y_ref, u_ref, d_ref, w_ref, b_ref, o_ref):
    yg = jax.nn.gelu(y_ref[...] + d_ref[...] * u_ref[...])
    z = jnp.dot(yg.astype(BF16), w_ref[...], preferred_element_type=F32) + b_ref[...]
    o_ref[...] = (yg * jax.nn.sigmoid(z)).astype(o_ref.dtype)


def _s5_post(y, p, d, w, b, tm):
    N = y.shape[0]
    W = S5_WIDTH
    return pl.pallas_call(
        _s5_post_kernel, grid=(N // tm,),
        in_specs=[pl.BlockSpec((tm, W), lambda i: (i, 0)), pl.BlockSpec((tm, W), lambda i: (i, C_S5 // W)),
                  pl.BlockSpec((1, W), lambda i: (0, 0)), pl.BlockSpec((W, W), lambda i: (0, 0)),
                  pl.BlockSpec((1, W), lambda i: (0, 0))],
        out_specs=pl.BlockSpec((tm, W), lambda i: (i, 0)),
        out_shape=jax.ShapeDtypeStruct((N, W), BF16),
        compiler_params=_cp(("parallel",)), name="s5_glu",
    )(y, p, d, w, b)


def _merge_kernel(h_ref, a_ref, b_ref, c_ref, wg0, wg1, wg2, bg0, bg1, bg2, wb_ref, o_ref):
    h = h_ref[...]
    acc = None
    for i, (br, wg, bg) in enumerate(((a_ref, wg0, bg0), (b_ref, wg1, bg1), (c_ref, wg2, bg2))):
        gate = jax.nn.sigmoid(jnp.dot(h, wg[...], preferred_element_type=F32) + bg[...])
        t = gate * jnp.dot(br[...], wb_ref[i], preferred_element_type=F32)
        acc = t if acc is None else acc + t
    o_ref[...] = acc.astype(o_ref.dtype)


def _merge(h, o_da, o_gla, o_s5, w_gate, b_gate, w_branch, tm, tn):
    N, D = h.shape
    W = o_da.shape[1]
    nt = D // tn
    act = pl.BlockSpec((tm, W), lambda j, i: (i, 0))
    in_specs = [pl.BlockSpec((tm, D), lambda j, i: (i, 0)), act, act, act]
    in_specs += [pl.BlockSpec((D, tn), lambda j, i, br=br: (0, br * nt + j)) for br in range(N_BRANCH)]
    in_specs += [pl.BlockSpec((1, tn), lambda j, i, br=br: (0, br * nt + j)) for br in range(N_BRANCH)]
    in_specs += [pl.BlockSpec((N_BRANCH, W, tn), lambda j, i: (0, 0, j))]
    return pl.pallas_call(
        _merge_kernel, grid=(nt, N // tm), in_specs=in_specs,
        out_specs=pl.BlockSpec((tm, tn), lambda j, i: (i, j)),
        out_shape=jax.ShapeDtypeStruct((N, D), BF16),
        compiler_params=_cp(("parallel", "parallel"), 56), name="gated_merge",
    )(h, o_da, o_gla, o_s5, w_gate, w_gate, w_gate, b_gate, b_gate, b_gate, w_branch)


def _wout_kernel(m_ref, w_ref, x_ref, g_ref, o_ref):
    o_ref[...] = x_ref[...] + g_ref[0] * jnp.dot(m_ref[...], w_ref[...], preferred_element_type=F32)


def _wout(merged, w, x, gate, midx, tm, tn):
    N, D = x.shape
    return pl.pallas_call(
        _wout_kernel, grid=(D // tn, N // tm),
        in_specs=[pl.BlockSpec((tm, D), lambda j, i: (i, 0)), pl.BlockSpec((D, tn), lambda j, i: (0, j)),
                  pl.BlockSpec((tm, tn), lambda j, i: (i, j)), pl.BlockSpec((1, 1, tn), lambda j, i: (midx(i), 0, j))],
        out_specs=pl.BlockSpec((tm, tn), lambda j, i: (i, j)),
        out_shape=jax.ShapeDtypeStruct((N, D), F32),
        compiler_params=_cp(("parallel", "parallel")), name="out_proj_residual",
    )(merged, w, x, gate)


def _route(logits, router_bias):
    n = logits.shape[0]
    scores = jax.nn.sigmoid(logits)
    sel = scores + router_bias.astype(F32)
    grp = sel.reshape(n, N_EXPERT_GROUPS, EXPERTS_PER_GROUP)
    grp_score = jnp.sum(lax.top_k(grp, 2)[0], axis=-1)
    _, gidx = lax.top_k(grp_score, TOPK_GROUP)
    gmask = jnp.any(jnp.arange(N_EXPERT_GROUPS)[None, None, :] == gidx[:, :, None], axis=1)
    emask = jnp.repeat(gmask, EXPERTS_PER_GROUP, axis=1)
    _, eidx = lax.top_k(jnp.where(emask, sel, -jnp.inf), TOP_K)
    w = jnp.take_along_axis(scores, eidx, axis=1)
    return eidx, w / jnp.sum(w, axis=-1, keepdims=True)


def _dispatch(eidx, w):
    n = eidx.shape[0]
    nk = n * TOP_K
    flat_e = eidx.reshape(-1).astype(jnp.int32)
    flat_tok = jnp.repeat(jnp.arange(n, dtype=jnp.int32), TOP_K)
    flat_w = w.reshape(-1)
    order = jnp.argsort(flat_e)
    se, st, sw = flat_e[order], flat_tok[order], flat_w[order]
    counts = jnp.zeros((N_EXPERTS,), jnp.int32).at[flat_e].add(1)
    padded = ((counts + MOE_TM - 1) // MOE_TM) * MOE_TM
    starts = jnp.cumsum(counts) - counts
    pends = jnp.cumsum(padded)
    pstarts = pends - padded
    dest = pstarts[se] + jnp.arange(nk, dtype=jnp.int32) - starts[se]
    ntile = nk // MOE_TM + N_EXPERTS
    cap = ntile * MOE_TM
    buf_tok = jnp.full((cap,), n, jnp.int32).at[dest].set(st)
    buf_w = jnp.zeros((cap,), F32).at[dest].set(sw)
    tile_start = jnp.arange(ntile, dtype=jnp.int32) * MOE_TM
    tile_e = jnp.minimum(jnp.searchsorted(pends, tile_start, side='right'), N_EXPERTS - 1).astype(jnp.int32)
    tile_valid = tile_start < pends[-1]
    last_e = tile_e[jnp.maximum(pends[-1] // MOE_TM - 1, 0)]
    tile_e = jnp.where(tile_valid, tile_e, last_e)
    slot = jnp.zeros((nk,), jnp.int32).at[order].set(dest).reshape(n, TOP_K)
    return buf_tok, buf_w, tile_e, tile_valid.astype(jnp.int32), slot


def _moe_kernel(te_ref, tv_ref, x_ref, w1_ref, w3_ref, w2_ref, bw_ref, o_ref):
    valid = tv_ref[pl.program_id(0)] != 0

    @pl.when(valid)
    def _():
        x = x_ref[...]
        a = jnp.dot(x, w1_ref[0], preferred_element_type=F32)
        b = jnp.dot(x, w3_ref[0], preferred_element_type=F32)
        hid = ((a * jax.nn.sigmoid(a)) * b).astype(BF16)
        o_ref[...] = jnp.dot(hid, w2_ref[0], preferred_element_type=F32) * bw_ref[...]

    @pl.when(jnp.logical_not(valid))
    def _():
        o_ref[...] = jnp.zeros(o_ref.shape, o_ref.dtype)


def _moe(xin, w1, w3, w2, buf_w, tile_e, tile_valid):
    cap, D = xin.shape
    F = w1.shape[2]
    TM = MOE_TM
    grid_spec = pltpu.PrefetchScalarGridSpec(
        num_scalar_prefetch=2, grid=(cap // TM,),
        in_specs=[pl.BlockSpec((TM, D), lambda i, te, tv: (i, 0)),
                  pl.BlockSpec((1, D, F), lambda i, te, tv: (te[i], 0, 0)),
                  pl.BlockSpec((1, D, F), lambda i, te, tv: (te[i], 0, 0)),
                  pl.BlockSpec((1, F, D), lambda i, te, tv: (te[i], 0, 0)),
                  pl.BlockSpec((TM, 1), lambda i, te, tv: (i, 0))],
        out_specs=pl.BlockSpec((TM, D), lambda i, te, tv: (i, 0)))
    return pl.pallas_call(
        _moe_kernel, grid_spec=grid_spec,
        out_shape=jax.ShapeDtypeStruct((cap, D), F32),
        compiler_params=_cp(("arbitrary",), 56), name="moe_experts",
    )(tile_e, tile_valid, xin, w1, w3, w2, buf_w.reshape(cap, 1))


def _combine_kernel(x_ref, y0_ref, y1_ref, g_ref, o_ref):
    o_ref[...] = x_ref[...] + g_ref[0] * (y0_ref[...] + y1_ref[...])


def _combine(x, y0, y1, gate, midx, tm):
    N, D = x.shape
    blk = pl.BlockSpec((tm, D), lambda i: (i, 0))
    return pl.pallas_call(
        _combine_kernel, grid=(N // tm,),
        in_specs=[blk, blk, blk, pl.BlockSpec((1, 1, D), lambda i: (midx(i), 0, 0))],
        out_specs=blk, out_shape=jax.ShapeDtypeStruct((N, D), F32),
        compiler_params=_cp(("parallel",)), name="moe_residual",
    )(x, y0, y1, gate)


def _rope_tables(n_tokens):
    rows = n_tokens // GRID_W
    row = jnp.repeat(jnp.arange(rows, dtype=F32), GRID_W)
    col = jnp.tile(jnp.arange(GRID_W, dtype=F32), rows)
    inv = ROPE_BASE ** (-jnp.arange(0, AX_DIM, 2, dtype=F32) / AX_DIM)
    ar = row[:, None] * inv[None, :]
    ac = col[:, None] * inv[None, :]
    ang = jnp.tile(jnp.concatenate([ar, ar, ac, ac], axis=-1), (1, LANE // DA_HEAD))
    first = (jnp.arange(LANE) % (AX_DIM)) < (AX_DIM // 2)
    return jnp.cos(ang), jnp.where(first[None, :], -jnp.sin(ang), jnp.sin(ang))


def kernel(x_prompt, x_sample, cache_k, cache_v, state_gla, state_s5_re, state_s5_im, c, c_ctx, w_mod, b_mod, g_norm, w_in, da_lambda_q, da_lambda_k, da_subln_g, gla_w_alpha, gla_b_alpha, gla_norm_g, s5_a_re, s5_a_im, s5_log_dt, s5_b_re, s5_b_im, s5_c_re, s5_c_im, s5_d, s5_w_glu, s5_b_glu, w_branch, w_gate, b_gate, w_out, router_w, router_bias, moe_w1, moe_w3, moe_w2, g_final):
    B, T, D = x_prompt.shape
    DB, DT, _ = x_sample.shape
    L = w_in.shape[0]
    PAST = cache_k.shape[2]
    NP, NS = B * T, DB * DT
    N = NP + NS
    nseq = B + DB
    assert D == D_MODEL and 1 + DB <= 8 and NP % DT == 0 and T % GLA_ROWS == 0 and DT % GLA_ROWS == 0
    tm = _pick(512, NP, DT)
    midx = functools.partial(_mod_index, tm=tm, n_ctx_tok=NP, dec_seq=DT)

    x = jnp.concatenate([x_prompt.reshape(NP, D), x_sample.reshape(NS, D)], axis=0)
    cv = jnp.zeros((8, D), F32).at[0].set(c_ctx).at[1:1 + DB].set(c)
    mods = _mod(cv, w_mod, b_mod).reshape(L, 8, N_MOD, 1, D)
    zero_mod = jnp.zeros((8, 1, D), F32)

    rope_tabs = _rope_tables(DT)
    ck = cache_k.reshape(DB, L, PAST, DA_HEADS * 2 * DA_HEAD)
    cvv = cache_v.reshape(DB, L, PAST, DA_HEADS * DA_VDIM)
    r = jnp.arange(GLA_ROWS)
    same = (r[:, None] // GLA_CHUNK) == (r[None, :] // GLA_CHUNK)
    tri = jnp.stack([same & (r[None, :] <= r[:, None]), same & (r[None, :] >= r[:, None])]).astype(BF16)
    rw = jnp.zeros((D, LANE), F32).at[:, :N_EXPERTS].set(router_w)
    rw_hi = rw.astype(BF16)
    rw_lo = (rw - rw_hi.astype(F32)).astype(BF16)
    tq_ctx = _pick(256, T)
    tq_lat = _pick(256, DT)
    s5_segments = ((0, B, T // S5_L), (NP // S5_L, DB, DT // S5_L))

    k_list, v_list, g_list, re_list, im_list = [], [], [], [], []
    for l in range(L):
        m = mods[l]
        shift1, scale1, gate1, shift2, scale2, gate2 = (m[:, i] for i in range(N_MOD))
        wi = w_in[l]
        w_in_p = jnp.concatenate([wi[:, :C_S5], wi[:, C_S5 + GLA_RANK:], wi[:, C_S5:C_S5 + GLA_RANK],
                                  jnp.zeros((D, P_COLS - C_GA - GLA_RANK), F32)], axis=1).astype(BF16)
        h = _norm(x, g_norm[l, 0][None], scale1, shift1, midx, tm, BF16)
        p = _matmul(h, w_in_p, tm, 1536, F32, "in_proj")

        lam_init = 0.8 - 0.6 * math.exp(-0.3 * l)
        lq, lk = da_lambda_q[l], da_lambda_k[l]
        lam = jnp.exp(jnp.sum(lq[0] * lk[0])) - jnp.exp(jnp.sum(lq[1] * lk[1])) + lam_init
        lam_row = jnp.full((1, LANE), lam, F32)
        g_row = (da_subln_g[l] * (1.0 - lam_init))[None]
        o_da = jnp.concatenate([
            _attention(p, lam_row, g_row, 0, B, T, tq_ctx),
            _attention(p, lam_row, g_row, NP, DB, DT, tq_lat, rope_tabs=rope_tabs, ctx=(ck, cvv, l))], axis=0)

        wa = jnp.zeros((1, 2, LANE, GLA_KTOT), F32).at[0, :, :GLA_RANK].set(gla_w_alpha[l]).astype(BF16)
        ba = gla_b_alpha[l].reshape(1, 2, 1, GLA_KTOT)
        s0_all = jnp.concatenate([jnp.zeros((B, 2, GLA_HEADS, GLA_DK, GLA_DV), F32), state_gla[:, l]], axis=0)
        o_f, o_b, gla_fin = _gla(p, wa, ba, tri, s0_all, ((B, T), (DB, DT)))
        o_gla = _gla_post(o_f, o_b, p, gla_norm_g[l][None], tm)

        uc = p[:, C_S5:C_S5 + S5_WIDTH].reshape(N // S5_L, S5_L, S5_GROUPS, S5_CH).transpose(0, 2, 1, 3)
        uc = uc.reshape(N // S5_L, S5_GROUPS * S5_L * S5_CH).astype(BF16)
        mats = _s5_matrices(s5_a_re[l], s5_a_im[l], s5_log_dt[l], s5_b_re[l], s5_b_im[l], s5_c_re[l], s5_c_im[l])

        def pack_state(s_lat):
            s = jnp.concatenate([jnp.zeros((B, 2, S5_GROUPS, S5_STATE), F32), s_lat], axis=0)
            return s.reshape(2 * nseq, S5_PAIRS, 2 * S5_STATE).transpose(1, 0, 2)

        x0 = jnp.stack([pack_state(state_s5_re[:, l]), pack_state(state_s5_im[:, l])])
        y, s5_fin = _s5(uc, mats, x0, s5_segments)
        y = y.reshape(N // S5_L, S5_GROUPS, S5_L, S5_CH).transpose(0, 2, 1, 3).reshape(N, S5_WIDTH)
        o_s5 = _s5_post(y, p, s5_d[l][None], s5_w_glu[l].astype(BF16), s5_b_glu[l][None], tm)
        s5_fin = s5_fin.transpose(0, 2, 1, 3).reshape(2, nseq, 2, S5_GROUPS, S5_STATE)[:, :B]

        merged = _merge(h, o_da, o_gla, o_s5, w_gate[l].astype(BF16), b_gate[l][None], w_branch[l].astype(BF16),
                        tm, 512)
        x = _wout(merged, w_out[l].astype(BF16), x, gate1, midx, tm, 1024)

        h2, logits = _norm(x, g_norm[l, 1][None], scale2, shift2, midx, tm, BF16, router=(rw_hi, rw_lo))
        eidx, wts = _route(logits[:, :N_EXPERTS], router_bias)
        buf_tok, buf_w, tile_e, tile_valid, slot = _dispatch(eidx, wts)
        xin = jnp.take(h2, buf_tok, axis=0, mode='fill', fill_value=0)
        ye = _moe(xin, moe_w1[l].astype(BF16), moe_w3[l].astype(BF16), moe_w2[l].astype(BF16), buf_w, tile_e,
                  tile_valid)
        x = _combine(x, jnp.take(ye, slot[:, 0], axis=0), jnp.take(ye, slot[:, 1], axis=0), gate2, midx, tm)

        k_list.append(p[:NP, C_K:C_K + 1024].reshape(B, T, DA_HEADS, 2 * DA_HEAD))
        v_list.append(p[:NP, C_V:C_V + 1024].reshape(B, T, DA_HEADS, DA_VDIM))
        g_list.append(gla_fin[:B])
        re_list.append(s5_fin[0])
        im_list.append(s5_fin[1])

    y_all = _norm(x, g_final[None], zero_mod, zero_mod, midx, tm, F32)
    return (y_all[:NP].reshape(B, T, D), y_all[NP:].reshape(DB, DT, D),
            jnp.stack(k_list, axis=1), jnp.stack(v_list, axis=1), jnp.stack(g_list, axis=1),
            jnp.stack(re_list, axis=1), jnp.stack(im_list, axis=1))
```

```python
import functools
import math

import jax
import jax.numpy as jnp
from jax import lax
from jax.experimental import pallas as pl
from jax.experimental.pallas import tpu as pltpu

F32 = jnp.float32
BF16 = jnp.bfloat16

D_MODEL = 2048
GRID_W = 64
EPS = 1e-6
N_MOD = 6
N_BRANCH = 3
DA_HEADS = 8
DA_HEAD = 64
DA_VDIM = 128
ROPE_BASE = 10000.0
AX_DIM = DA_HEAD // 2
GLA_HEADS = 4
GLA_DK = 128
GLA_DV = 256
GLA_KTOT = GLA_HEADS * GLA_DK
GLA_WIDTH = GLA_HEADS * GLA_DV
GLA_RANK = 16
GLA_TAU = 16.0
GLA_CHUNK = 64
S5_WIDTH = 1024
S5_CH = 16
S5_GROUPS = S5_WIDTH // S5_CH
S5_STATE = 64
N_EXPERTS = 16
N_EXPERT_GROUPS = 4
EXPERTS_PER_GROUP = N_EXPERTS // N_EXPERT_GROUPS
TOPK_GROUP = 1
TOP_K = 2
D_FF_EXPERT = 1408

LANE = 128
C_Q, C_K, C_V, C_GQ, C_GK, C_GV, C_GR, C_S5, C_GA = 0, 1024, 2048, 3072, 3584, 4096, 5120, 6144, 7168
P_COLS = 7680
GLA_ROWS = 256
S5_L = 16
S5_PAIRS = S5_GROUPS // 2
S5_PAIRS_PER_STEP = 4
MOE_TM = 256
VMEM_MB = 1024 * 1024


def _cp(sem, vmem_mb=48):
    return pltpu.CompilerParams(dimension_semantics=sem, vmem_limit_bytes=vmem_mb * VMEM_MB)


def _pick(pref, *ns):
    t = pref
    while any(n % t for n in ns):
        t //= 2
    return t


def _mod_index(i, tm, n_ctx_tok, dec_seq):
    t0 = i * tm
    return jnp.where(t0 < n_ctx_tok, 0, 1 + (t0 - n_ctx_tok) // dec_seq)


def _mod_kernel(c_ref, w_ref, b_ref, o_ref):
    c = c_ref[...]
    s = (c * jax.nn.sigmoid(c)).astype(BF16)
    o_ref[0] = jnp.dot(s, w_ref[0].astype(BF16), preferred_element_type=F32) + b_ref[0]


def _mod(cv, w_mod, b_mod):
    L, D, M = w_mod.shape
    tn = 1024
    return pl.pallas_call(
        _mod_kernel,
        grid=(L, M // tn),
        in_specs=[pl.BlockSpec((8, D), lambda l, j: (0, 0)),
                  pl.BlockSpec((1, D, tn), lambda l, j: (l, 0, j)),
                  pl.BlockSpec((1, 1, tn), lambda l, j: (l, 0, j))],
        out_specs=pl.BlockSpec((1, 8, tn), lambda l, j: (l, 0, j)),
        out_shape=jax.ShapeDtypeStruct((L, 8, M), F32),
        compiler_params=_cp(("parallel", "parallel")),
        name="mod_vectors",
    )(cv, w_mod, b_mod.reshape(L, 1, M))


def _norm_core(x_ref, g_ref, sc_ref, sh_ref):
    x = x_ref[...]
    y = x * lax.rsqrt(jnp.mean(x * x, axis=-1, keepdims=True) + EPS)
    return (y * g_ref[...]) * (1.0 + sc_ref[0]) + sh_ref[0]


def _norm_kernel(x_ref, g_ref, sc_ref, sh_ref, o_ref):
    o_ref[...] = _norm_core(x_ref, g_ref, sc_ref, sh_ref).astype(o_ref.dtype)


def _norm_router_kernel(x_ref, g_ref, sc_ref, sh_ref, rh_ref, rl_ref, o_ref, lg_ref):
    h = _norm_core(x_ref, g_ref, sc_ref, sh_ref)
    hh = h.astype(BF16)
    hl = (h - hh.astype(F32)).astype(BF16)
    o_ref[...] = hh
    nt = (((1,), (1,)), ((), ()))
    lg_ref[...] = (lax.dot_general(rh_ref[...], hh, nt, preferred_element_type=F32)
                   + lax.dot_general(rh_ref[...], hl, nt, preferred_element_type=F32)
                   + lax.dot_general(rl_ref[...], hh, nt, preferred_element_type=F32))


def _norm(x, g, sc, sh, midx, tm, out_dtype, router=None):
    N, D = x.shape
    in_specs = [pl.BlockSpec((tm, D), lambda i: (i, 0)),
                pl.BlockSpec((1, D), lambda i: (0, 0)),
                pl.BlockSpec((1, 1, D), lambda i: (midx(i), 0, 0)),
                pl.BlockSpec((1, 1, D), lambda i: (midx(i), 0, 0))]
    if router is None:
        return pl.pallas_call(
            _norm_kernel, grid=(N // tm,), in_specs=in_specs,
            out_specs=pl.BlockSpec((tm, D), lambda i: (i, 0)),
            out_shape=jax.ShapeDtypeStruct((N, D), out_dtype),
            compiler_params=_cp(("parallel",)), name="norm_mod",
        )(x, g, sc, sh)
    rh, rl = router
    E = rh.shape[0]
    in_specs += [pl.BlockSpec((E, D), lambda i: (0, 0)), pl.BlockSpec((E, D), lambda i: (0, 0))]
    return pl.pallas_call(
        _norm_router_kernel, grid=(N // tm,), in_specs=in_specs,
        out_specs=[pl.BlockSpec((tm, D), lambda i: (i, 0)), pl.BlockSpec((E, tm), lambda i: (0, i))],
        out_shape=[jax.ShapeDtypeStruct((N, D), BF16), jax.ShapeDtypeStruct((E, N), F32)],
        compiler_params=_cp(("parallel",)), name="norm_mod_router",
    )(x, g, sc, sh, rh, rl)


def _mm_kernel(a_ref, w_ref, o_ref):
    o_ref[...] = jnp.dot(a_ref[...], w_ref[...], preferred_element_type=F32).astype(o_ref.dtype)


def _matmul(a, w, tm, tn, out_dtype, name):
    M, K = a.shape
    Nc = w.shape[1]
    return pl.pallas_call(
        _mm_kernel, grid=(Nc // tn, M // tm),
        in_specs=[pl.BlockSpec((tm, K), lambda j, i: (i, 0)), pl.BlockSpec((K, tn), lambda j, i: (0, j))],
        out_specs=pl.BlockSpec((tm, tn), lambda j, i: (i, j)),
        out_shape=jax.ShapeDtypeStruct((M, Nc), out_dtype),
        compiler_params=_cp(("parallel", "parallel")), name=name,
    )(a, w)


def _rope(x, cos, sin_signed):
    lane = lax.broadcasted_iota(jnp.int32, (1, LANE), 1)
    first = (lane % (2 * (AX_DIM // 2))) < (AX_DIM // 2)
    xr = jnp.where(first, pltpu.roll(x, LANE - AX_DIM // 2, 1), pltpu.roll(x, AX_DIM // 2, 1))
    return x * cos + xr * sin_signed


def _attn_kernel(*refs, rope, ctx):
    refs = list(refs)
    q_ref, k_ref, v_ref = refs[:3]
    refs = refs[3:]
    if ctx:
        ck_ref, cv_ref = refs[:2]
        refs = refs[2:]
    if rope:
        cq_ref, sq_ref, ckk_ref, skk_ref = refs[:4]
        refs = refs[4:]
    lam_ref, g_ref, o_ref, kb_ref, vb_ref = refs

    @pl.when(pl.program_id(2) == 0)
    def _():
        k = k_ref[...]
        if rope:
            k = _rope(k, ckk_ref[...], skk_ref[...])
        kb_ref[...] = k.astype(BF16)
        vb_ref[...] = v_ref[...].astype(BF16)

    q = q_ref[...]
    if rope:
        q = _rope(q, cq_ref[...], sq_ref[...])
    q = q * (DA_HEAD ** -0.5)
    lane = lax.broadcasted_iota(jnp.int32, (1, LANE), 1)
    lo = lane < DA_HEAD
    q0 = jnp.where(lo, q, 0.0).astype(BF16)
    q1 = jnp.where(lo, 0.0, q).astype(BF16)
    nt = (((1,), (1,)), ((), ()))
    kb = kb_ref[...]
    s0 = lax.dot_general(q0, kb, nt, preferred_element_type=F32)
    s1 = lax.dot_general(q1, kb, nt, preferred_element_type=F32)
    m0 = jnp.max(s0, axis=-1, keepdims=True)
    m1 = jnp.max(s1, axis=-1, keepdims=True)
    if ctx:
        ckb = ck_ref[...].astype(BF16)
        c0 = lax.dot_general(q0, ckb, nt, preferred_element_type=F32)
        c1 = lax.dot_general(q1, ckb, nt, preferred_element_type=F32)
        m0 = jnp.maximum(m0, jnp.max(c0, axis=-1, keepdims=True))
        m1 = jnp.maximum(m1, jnp.max(c1, axis=-1, keepdims=True))
    e0 = jnp.exp(s0 - m0)
    e1 = jnp.exp(s1 - m1)
    l0 = jnp.sum(e0, axis=-1, keepdims=True)
    l1 = jnp.sum(e1, axis=-1, keepdims=True)
    if ctx:
        ec0 = jnp.exp(c0 - m0)
        ec1 = jnp.exp(c1 - m1)
        l0 = l0 + jnp.sum(ec0, axis=-1, keepdims=True)
        l1 = l1 + jnp.sum(ec1, axis=-1, keepdims=True)
    a0 = 1.0 / l0
    a1 = lam_ref[:, :1] / l1
    o = jnp.dot((e0 * a0 - e1 * a1).astype(BF16), vb_ref[...], preferred_element_type=F32)
    if ctx:
        o = o + jnp.dot((ec0 * a0 - ec1 * a1).astype(BF16), cv_ref[...].astype(BF16), preferred_element_type=F32)
    y = o * lax.rsqrt(jnp.mean(o * o, axis=-1, keepdims=True) + EPS)
    o_ref[...] = (y * g_ref[...]).astype(o_ref.dtype)


def _attention(p, lam_row, g_row, row0, B, T, tq, rope_tabs=None, ctx=None):
    nq = T // tq
    rb0 = row0 // tq
    kb0 = row0 // T
    hq, hk, hv = C_Q // LANE, C_K // LANE, C_V // LANE
    in_specs = [pl.BlockSpec((tq, LANE), lambda b, h, i: (rb0 + b * nq + i, hq + h)),
                pl.BlockSpec((T, LANE), lambda b, h, i: (kb0 + b, hk + h)),
                pl.BlockSpec((T, LANE), lambda b, h, i: (kb0 + b, hv + h))]
    args = [p, p, p]
    if ctx is not None:
        ck, cv, layer = ctx
        past = ck.shape[2]
        in_specs += [pl.BlockSpec((None, None, past, LANE), lambda b, h, i: (b, layer, 0, h)),
                     pl.BlockSpec((None, None, past, LANE), lambda b, h, i: (b, layer, 0, h))]
        args += [ck, cv]
    if rope_tabs is not None:
        cos, sin = rope_tabs
        in_specs += [pl.BlockSpec((tq, LANE), lambda b, h, i: (i, 0)), pl.BlockSpec((tq, LANE), lambda b, h, i: (i, 0)),
                     pl.BlockSpec((T, LANE), lambda b, h, i: (0, 0)), pl.BlockSpec((T, LANE), lambda b, h, i: (0, 0))]
        args += [cos, sin, cos, sin]
    in_specs += [pl.BlockSpec((1, LANE), lambda b, h, i: (0, 0)), pl.BlockSpec((1, LANE), lambda b, h, i: (0, 0))]
    args += [lam_row, g_row]
    return pl.pallas_call(
        functools.partial(_attn_kernel, rope=rope_tabs is not None, ctx=ctx is not None),
        grid=(B, DA_HEADS, nq),
        in_specs=in_specs,
        out_specs=pl.BlockSpec((tq, LANE), lambda b, h, i: (b * nq + i, h)),
        out_shape=jax.ShapeDtypeStruct((B * T, DA_HEADS * DA_VDIM), BF16),
        scratch_shapes=[pltpu.VMEM((T, LANE), BF16), pltpu.VMEM((T, LANE), BF16)],
        compiler_params=_cp(("parallel", "parallel", "arbitrary"), 56),
        name="diff_attention_latent" if ctx is not None else "diff_attention_context",
    )(*args)


def _log_sigmoid(z):
    return jnp.minimum(z, 0.0) - jnp.log1p(jnp.exp(-jnp.abs(z)))


def _gla_direction(d, q_ref, k_ref, v_ref, ga_ref, wa_ref, ba_ref, tri_ref, o_ref, st_ref):
    C = GLA_CHUNK
    nsub = GLA_ROWS // C
    z = jnp.dot(ga_ref[...].astype(BF16), wa_ref[0, d], preferred_element_type=F32) + ba_ref[0, d]
    la = _log_sigmoid(z) / GLA_TAU
    hi = la.astype(BF16)
    r1 = la - hi.astype(F32)
    mid = r1.astype(BF16)
    low = (r1 - mid.astype(F32)).astype(BF16)
    tri = tri_ref[d]
    bsum = (jnp.dot(tri, hi, preferred_element_type=F32) + jnp.dot(tri, mid, preferred_element_type=F32)
            + jnp.dot(tri, low, preferred_element_type=F32))
    row = lax.broadcasted_iota(jnp.int32, (C, C), 0)
    col = lax.broadcasted_iota(jnp.int32, (C, C), 1)
    mask = (row >= col) if d == 0 else (row <= col)
    nt = (((1,), (1,)), ((), ()))
    tn = (((0,), (0,)), ((), ()))
    subs = range(nsub) if d == 0 else range(nsub - 1, -1, -1)
    for h in range(GLA_HEADS):
        hk = slice(h * GLA_DK, (h + 1) * GLA_DK)
        hv = slice(h * GLA_DV, (h + 1) * GLA_DV)
        st = st_ref[d, h]
        for s in subs:
            rs = slice(s * C, (s + 1) * C)
            end = s * C + C - 1 if d == 0 else s * C
            bb = bsum[rs, hk]
            bl = bsum[end:end + 1, hk]
            q = q_ref[rs, hk] * (GLA_DK ** -0.5)
            k = k_ref[rs, hk]
            v = v_ref[rs, hv].astype(BF16)
            qt = (q * jnp.exp(bb)).astype(BF16)
            kt = (k * jnp.exp(-bb)).astype(BF16)
            kend = (k * jnp.exp(bl - bb)).astype(BF16)
            att = jnp.where(mask, lax.dot_general(qt, kt, nt, preferred_element_type=F32), 0.0).astype(BF16)
            o = jnp.dot(att, v, preferred_element_type=F32)
            o = o + lax.dot_general(qt, st.astype(BF16), nt, preferred_element_type=F32)
            o_ref[rs, hv] = o
            st = st * jnp.exp(bl) + lax.dot_general(v, kend, tn, preferred_element_type=F32)
        st_ref[d, h] = st


def _gla_kernel(qf, kf, vf, gaf, qb, kb, vb, gab, wa_ref, ba_ref, tri_ref, s0_ref, of_ref, ob_ref, fin_ref, st_ref,
                *, seq_pos):
    first, last = seq_pos(pl.program_id(0))

    @pl.when(first)
    def _():
        for d in range(2):
            for h in range(GLA_HEADS):
                st_ref[d, h] = s0_ref[0, d, h].T

    _gla_direction(0, qf, kf, vf, gaf, wa_ref, ba_ref, tri_ref, of_ref, st_ref)
    _gla_direction(1, qb, kb, vb, gab, wa_ref, ba_ref, tri_ref, ob_ref, st_ref)

    @pl.when(last)
    def _():
        for d in range(2):
            for h in range(GLA_HEADS):
                fin_ref[0, d, h] = st_ref[d, h].T


def _gla(p, wa, ba, tri, s0_all, seqs):
    N = p.shape[0]
    R = GLA_ROWS
    (n1, t1), (n2, t2) = seqs
    nb1, nb2 = t1 // R, t2 // R
    steps1 = n1 * nb1

    def locate(s):
        in1 = s < steps1
        s2 = s - steps1
        nb = jnp.where(in1, nb1, nb2)
        seq = jnp.where(in1, s // nb1, n1 + s2 // nb2)
        n = jnp.where(in1, s % nb1, s2 % nb2)
        base = jnp.where(in1, (s // nb1) * nb1, steps1 + (s2 // nb2) * nb2)
        return seq, n, nb, base

    def fwd_blk(s):
        _, n, _, base = locate(s)
        return base + n

    def bwd_blk(s):
        _, n, nb, base = locate(s)
        return base + nb - 1 - n

    def seq_pos(s):
        _, n, nb, _ = locate(s)
        return n == 0, n == nb - 1

    def specs(blk):
        return [pl.BlockSpec((R, GLA_KTOT), lambda s: (blk(s), C_GQ // GLA_KTOT)),
                pl.BlockSpec((R, GLA_KTOT), lambda s: (blk(s), C_GK // GLA_KTOT)),
                pl.BlockSpec((R, GLA_WIDTH), lambda s: (blk(s), C_GV // GLA_WIDTH)),
                pl.BlockSpec((R, LANE), lambda s: (blk(s), C_GA // LANE))]

    state_spec = pl.BlockSpec((1, 2, GLA_HEADS, GLA_DK, GLA_DV), lambda s: (locate(s)[0], 0, 0, 0, 0))
    nseq = n1 + n2
    return pl.pallas_call(
        functools.partial(_gla_kernel, seq_pos=seq_pos),
        grid=(N // R,),
        in_specs=specs(fwd_blk) + specs(bwd_blk) + [
            pl.BlockSpec((1, 2, LANE, GLA_KTOT), lambda s: (0, 0, 0, 0)),
            pl.BlockSpec((1, 2, 1, GLA_KTOT), lambda s: (0, 0, 0, 0)),
            pl.BlockSpec((2, R, R), lambda s: (0, 0, 0)),
            state_spec],
        out_specs=[pl.BlockSpec((R, GLA_WIDTH), lambda s: (fwd_blk(s), 0)),
                   pl.BlockSpec((R, GLA_WIDTH), lambda s: (bwd_blk(s), 0)),
                   state_spec],
        out_shape=[jax.ShapeDtypeStruct((N, GLA_WIDTH), F32), jax.ShapeDtypeStruct((N, GLA_WIDTH), F32),
                   jax.ShapeDtypeStruct((nseq, 2, GLA_HEADS, GLA_DK, GLA_DV), F32)],
        scratch_shapes=[pltpu.VMEM((2, GLA_HEADS, GLA_DV, GLA_DK), F32)],
        compiler_params=_cp(("arbitrary",)),
        name="gla_bidir",
    )(p, p, p, p, p, p, p, p, wa, ba, tri, s0_all)


def _gla_post_kernel(of_ref, ob_ref, r_ref, g_ref, o_ref):
    for h in range(GLA_HEADS):
        hv = slice(h * GLA_DV, (h + 1) * GLA_DV)
        o = of_ref[:, hv] + ob_ref[:, hv]
        y = o * lax.rsqrt(jnp.mean(o * o, axis=-1, keepdims=True) + EPS) * g_ref[...]
        r = r_ref[:, hv]
        o_ref[:, hv] = (y * (r * jax.nn.sigmoid(r))).astype(o_ref.dtype)


def _gla_post(of, ob, p, g, tm):
    N = of.shape[0]
    return pl.pallas_call(
        _gla_post_kernel, grid=(N // tm,),
        in_specs=[pl.BlockSpec((tm, GLA_WIDTH), lambda i: (i, 0)), pl.BlockSpec((tm, GLA_WIDTH), lambda i: (i, 0)),
                  pl.BlockSpec((tm, GLA_WIDTH), lambda i: (i, C_GR // GLA_WIDTH)),
                  pl.BlockSpec((1, GLA_DV), lambda i: (0, 0))],
        out_specs=pl.BlockSpec((tm, GLA_WIDTH), lambda i: (i, 0)),
        out_shape=jax.ShapeDtypeStruct((N, GLA_WIDTH), BF16),
        compiler_params=_cp(("parallel",)), name="gla_norm_gate",
    )(of, ob, p, g)


def _s5_matrices(a_re, a_im, log_dt, b_re, b_im, c_re, c_im):
    hp = lax.Precision.HIGHEST
    L, G, P, CH = S5_L, S5_GROUPS, S5_STATE, S5_CH
    taus = jnp.arange(L + 1, dtype=F32)
    intra = jnp.zeros((G, L, CH, L, CH), F32)
    ii = jnp.arange(L)
    dist = ii[None, :] - ii[:, None]
    state, carry, decay = [], [], []
    for d in range(2):
        dt = jnp.exp(log_dt[d])[:, None]
        ld_re, ld_im = a_re[d] * dt, a_im[d] * dt
        mag = jnp.exp(ld_re)
        lb_re, lb_im = mag * jnp.cos(ld_im), mag * jnp.sin(ld_im)
        den = a_re[d] * a_re[d] + a_im[d] * a_im[d]
        q_re = ((lb_re - 1.0) * a_re[d] + lb_im * a_im[d]) / den
        q_im = (lb_im * a_re[d] - (lb_re - 1.0) * a_im[d]) / den
        bb_re = q_re[..., None] * b_re - q_im[..., None] * b_im
        bb_im = q_re[..., None] * b_im + q_im[..., None] * b_re
        pmag = jnp.exp(ld_re[None] * taus[:, None, None])
        pw_re = pmag * jnp.cos(ld_im[None] * taus[:, None, None])
        pw_im = pmag * jnp.sin(ld_im[None] * taus[:, None, None])
        cp_re = c_re[None] * pw_re[:L, :, None, :] - c_im[None] * pw_im[:L, :, None, :]
        cp_im = c_re[None] * pw_im[:L, :, None, :] + c_im[None] * pw_re[:L, :, None, :]
        kern = (jnp.einsum('tgop,gpc->tgco', cp_re, bb_re, precision=hp)
                - jnp.einsum('tgop,gpc->tgco', cp_im, bb_im, precision=hp))
        dd = dist if d == 0 else -dist
        sel = kern[jnp.clip(dd, 0, L - 1)]
        sel = jnp.where((dd >= 0)[:, :, None, None, None], sel, 0.0)
        intra = intra + sel.transpose(2, 0, 3, 1, 4)
        e = (L - 1 - ii) if d == 0 else ii
        st_re = pw_re[e][:, :, :, None] * bb_re[None] - pw_im[e][:, :, :, None] * bb_im[None]
        st_im = pw_re[e][:, :, :, None] * bb_im[None] + pw_im[e][:, :, :, None] * bb_re[None]
        state.append(tuple(s.transpose(1, 0, 3, 2).reshape(G, L * CH, P) for s in (st_re, st_im)))
        e = (ii + 1) if d == 0 else (L - ii)
        cw_re = c_re[None] * pw_re[e][:, :, None, :] - c_im[None] * pw_im[e][:, :, None, :]
        cw_im = c_re[None] * pw_im[e][:, :, None, :] + c_im[None] * pw_re[e][:, :, None, :]
        carry.append(tuple(c.transpose(1, 3, 0, 2).reshape(G, P, L * CH) for c in (cw_re, -cw_im)))
        decay.append((pw_re[L], pw_im[L]))
    intra = intra.reshape(G, L * CH, L * CH)

    def pair_cols(m):
        K = m.shape[1]
        m = m.reshape(S5_PAIRS, 2, K, P)
        z = jnp.zeros((S5_PAIRS, K, P), m.dtype)
        top = jnp.concatenate([m[:, 0], z], axis=2)
        bot = jnp.concatenate([z, m[:, 1]], axis=2)
        return jnp.concatenate([top, bot], axis=1)

    def pair_rows(m):
        K = m.shape[2]
        m = m.reshape(S5_PAIRS, 2, P, K)
        z = jnp.zeros((S5_PAIRS, P, K), m.dtype)
        top = jnp.concatenate([m[:, 0], z], axis=2)
        bot = jnp.concatenate([z, m[:, 1]], axis=2)
        return jnp.concatenate([top, bot], axis=1)

    ms = jnp.stack([jnp.stack([pair_cols(s[0]), pair_cols(s[1])]) for s in state])
    mc = jnp.stack([jnp.stack([pair_rows(c[0]), pair_rows(c[1])]) for c in carry])
    dec = jnp.stack([jnp.stack([x[0], x[1]]) for x in decay])
    dec = dec.reshape(2, 2, S5_PAIRS, 1, 2 * P)
    return intra.astype(BF16), ms.astype(BF16), mc.astype(BF16), dec


def _s5_kernel(u_ref, mi_ref, ms_ref, mc_ref, dec_ref, x0_ref, y_ref, fin_ref, z_ref, *, segments):
    PP = S5_PAIRS_PER_STEP
    W = 2 * S5_L * S5_CH
    for pp in range(PP):
        u = u_ref[:, pp * W:(pp + 1) * W]
        for d in range(2):
            for c in range(2):
                z_ref[pp, d, c] = jnp.dot(u, ms_ref[d, c, pp], preferred_element_type=F32)

    chains = [(pp, d) for pp in range(PP) for d in range(2)]
    dec = {(pp, d): (dec_ref[d, 0, pp], dec_ref[d, 1, pp]) for (pp, d) in chains}
    seq_base = 0
    for (row0, n_seq, n_chunk) in segments:
        def seq_loop(sq, carry, row0=row0, n_chunk=n_chunk, seq_base=seq_base):
            r0 = row0 + sq * n_chunk
            sid = seq_base + sq
            x_init = tuple((x0_ref[0, pp, pl.ds(2 * sid + d, 1), :], x0_ref[1, pp, pl.ds(2 * sid + d, 1), :])
                           for (pp, d) in chains)

            def step(n, xs):
                out = []
                for (pp, d), (xr, xi) in zip(chains, xs):
                    ar, ai = dec[(pp, d)]
                    r = r0 + (n if d == 0 else n_chunk - 1 - n)
                    zr = z_ref[pp, d, 0, pl.ds(r, 1), :]
                    zi = z_ref[pp, d, 1, pl.ds(r, 1), :]
                    z_ref[pp, d, 0, pl.ds(r, 1), :] = xr
                    z_ref[pp, d, 1, pl.ds(r, 1), :] = xi
                    out.append((ar * xr - ai * xi + zr, ar * xi + ai * xr + zi))
                return tuple(out)

            xs = lax.fori_loop(0, n_chunk, step, x_init)
            for (pp, d), (xr, xi) in zip(chains, xs):
                fin_ref[0, pp, pl.ds(2 * sid + d, 1), :] = xr
                fin_ref[1, pp, pl.ds(2 * sid + d, 1), :] = xi
            return carry

        lax.fori_loop(0, n_seq, seq_loop, 0)
        seq_base += n_seq

    G2 = S5_L * S5_CH
    for pp in range(PP):
        y = jnp.dot(z_ref[pp, 0, 0].astype(BF16), mc_ref[0, 0, pp], preferred_element_type=F32)
        y = y + jnp.dot(z_ref[pp, 0, 1].astype(BF16), mc_ref[0, 1, pp], preferred_element_type=F32)
        y = y + jnp.dot(z_ref[pp, 1, 0].astype(BF16), mc_ref[1, 0, pp], preferred_element_type=F32)
        y = y + jnp.dot(z_ref[pp, 1, 1].astype(BF16), mc_ref[1, 1, pp], preferred_element_type=F32)
        for g in range(2):
            cs = slice(pp * W + g * G2, pp * W + (g + 1) * G2)
            yi = jnp.dot(u_ref[:, cs], mi_ref[2 * pp + g], preferred_element_type=F32)
            y_ref[:, cs] = yi + y[:, g * G2:(g + 1) * G2]


def _s5(uc, mats, x0, segments):
    mi, ms, mc, dec = mats
    NC = uc.shape[0]
    PP = S5_PAIRS_PER_STEP
    W = 2 * S5_L * S5_CH
    G2 = S5_L * S5_CH
    nrow = x0.shape[2]
    state_spec = pl.BlockSpec((2, PP, nrow, LANE), lambda j: (0, j, 0, 0))
    return pl.pallas_call(
        functools.partial(_s5_kernel, segments=segments),
        grid=(S5_PAIRS // PP,),
        in_specs=[pl.BlockSpec((NC, PP * W), lambda j: (0, j)),
                  pl.BlockSpec((2 * PP, G2, G2), lambda j: (j, 0, 0)),
                  pl.BlockSpec((2, 2, PP, W, LANE), lambda j: (0, 0, j, 0, 0)),
                  pl.BlockSpec((2, 2, PP, LANE, W), lambda j: (0, 0, j, 0, 0)),
                  pl.BlockSpec((2, 2, PP, 1, LANE), lambda j: (0, 0, j, 0, 0)),
                  state_spec],
        out_specs=[pl.BlockSpec((NC, PP * W), lambda j: (0, j)), state_spec],
        out_shape=[jax.ShapeDtypeStruct((NC, S5_GROUPS * G2), F32),
                   jax.ShapeDtypeStruct((2, S5_PAIRS, nrow, LANE), F32)],
        scratch_shapes=[pltpu.VMEM((PP, 2, 2, NC, LANE), F32)],
        compiler_params=_cp(("parallel",), 56),
        name="s5_bidir",
    )(uc, mi, ms, mc, dec, x0)


def _s5_post_kernel(y_ref, u_ref, d_ref, w_ref, b_ref, o_ref):
    yg = jax.nn.gelu(y_ref[...] + d_ref[...] * u_ref[...])
    z = jnp.dot(yg.astype(BF16), w_ref[...], preferred_element_type=F32) + b_ref[...]
    o_ref[...] = (yg * jax.nn.sigmoid(z)).astype(o_ref.dtype)


def _s5_post(y, p, d, w, b, tm):
    N = y.shape[0]
    W = S5_WIDTH
    return pl.pallas_call(
        _s5_post_kernel, grid=(N // tm,),
        in_specs=[pl.BlockSpec((tm, W), lambda i: (i, 0)), pl.BlockSpec((tm, W), lambda i: (i, C_S5 // W)),
                  pl.BlockSpec((1, W), lambda i: (0, 0)), pl.BlockSpec((W, W), lambda i: (0, 0)),
                  pl.BlockSpec((1, W), lambda i: (0, 0))],
        out_specs=pl.BlockSpec((tm, W), lambda i: (i, 0)),
        out_shape=jax.ShapeDtypeStruct((N, W), BF16),
        compiler_params=_cp(("parallel",)), name="s5_glu",
    )(y, p, d, w, b)


def _merge_kernel(h_ref, a_ref, b_ref, c_ref, wg0, wg1, wg2, bg0, bg1, bg2, wb_ref, o_ref):
    h = h_ref[...]
    acc = None
    for i, (br, wg, bg) in enumerate(((a_ref, wg0, bg0), (b_ref, wg1, bg1), (c_ref, wg2, bg2))):
        gate = jax.nn.sigmoid(jnp.dot(h, wg[...], preferred_element_type=F32) + bg[...])
        t = gate * jnp.dot(br[...], wb_ref[i], preferred_element_type=F32)
        acc = t if acc is None else acc + t
    o_ref[...] = acc.astype(o_ref.dtype)


def _merge(h, o_da, o_gla, o_s5, w_gate, b_gate, w_branch, tm, tn):
    N, D = h.shape
    W = o_da.shape[1]
    nt = D // tn
    act = pl.BlockSpec((tm, W), lambda j, i: (i, 0))
    in_specs = [pl.BlockSpec((tm, D), lambda j, i: (i, 0)), act, act, act]
    in_specs += [pl.BlockSpec((D, tn), lambda j, i, br=br: (0, br * nt + j)) for br in range(N_BRANCH)]
    in_specs += [pl.BlockSpec((1, tn), lambda j, i, br=br: (0, br * nt + j)) for br in range(N_BRANCH)]
    in_specs += [pl.BlockSpec((N_BRANCH, W, tn), lambda j, i: (0, 0, j))]
    return pl.pallas_call(
        _merge_kernel, grid=(nt, N // tm), in_specs=in_specs,
        out_specs=pl.BlockSpec((tm, tn), lambda j, i: (i, j)),
        out_shape=jax.ShapeDtypeStruct((N, D), BF16),
        compiler_params=_cp(("parallel", "parallel"), 56), name="gated_merge",
    )(h, o_da, o_gla, o_s5, w_gate, w_gate, w_gate, b_gate, b_gate, b_gate, w_branch)


def _wout_kernel(m_ref, w_ref, x_ref, g_ref, o_ref):
    o_ref[...] = x_ref[...] + g_ref[0] * jnp.dot(m_ref[...], w_ref[...], preferred_element_type=F32)


def _wout(merged, w, x, gate, midx, tm, tn):
    N, D = x.shape
    return pl.pallas_call(
        _wout_kernel, grid=(D // tn, N // tm),
        in_specs=[pl.BlockSpec((tm, D), lambda j, i: (i, 0)), pl.BlockSpec((D, tn), lambda j, i: (0, j)),
                  pl.BlockSpec((tm, tn), lambda j, i: (i, j)), pl.BlockSpec((1, 1, tn), lambda j, i: (midx(i), 0, j))],
        out_specs=pl.BlockSpec((tm, tn), lambda j, i: (i, j)),
        out_shape=jax.ShapeDtypeStruct((N, D), F32),
        compiler_params=_cp(("parallel", "parallel")), name="out_proj_residual",
    )(merged, w, x, gate)


def _route_kernel(lg_ref, bias_ref, asg_ref, tile_ref, rank_ref, *, tl):
    E, n_tok = lg_ref.shape
    EPG = EXPERTS_PER_GROUP
    ninf = -jnp.inf
    eio = lax.broadcasted_iota(jnp.int32, (E, tl), 0)
    upper = jnp.where(lax.broadcasted_iota(jnp.int32, (tl, tl), 0) < lax.broadcasted_iota(jnp.int32, (tl, tl), 1),
                      1.0, 0.0).astype(BF16)
    ecol = lax.broadcasted_iota(jnp.int32, (E, 1), 0)

    def select(j, carry):
        ls = pl.ds(pl.multiple_of(j * tl, tl), tl)
        s = jax.nn.sigmoid(lg_ref[:, ls])
        sel = s + bias_ref[...]
        rows = [sel[e:e + 1, :] for e in range(E)]
        srow = [s[e:e + 1, :] for e in range(E)]
        gi = jnp.zeros((1, tl), jnp.int32)
        best = None
        for g in range(N_EXPERT_GROUPS):
            a, b, c, d = rows[g * EPG:(g + 1) * EPG]
            hi1, lo1, hi2, lo2 = jnp.maximum(a, b), jnp.minimum(a, b), jnp.maximum(c, d), jnp.minimum(c, d)
            gs = jnp.maximum(hi1, hi2) + jnp.maximum(jnp.minimum(hi1, hi2), jnp.maximum(lo1, lo2))
            if best is None:
                best = gs
            else:
                upd = gs > best
                gi = jnp.where(upd, g, gi)
                best = jnp.where(upd, gs, best)

        def pick(rs, i):
            out = rs[(N_EXPERT_GROUPS - 1) * EPG + i]
            for g in range(N_EXPERT_GROUPS - 2, -1, -1):
                out = jnp.where(gi == g, rs[g * EPG + i], out)
            return out

        v = [pick(rows, i) for i in range(EPG)]
        sv = [pick(srow, i) for i in range(EPG)]
        i1, b1, s1 = jnp.zeros((1, tl), jnp.int32), v[0], sv[0]
        for i in range(1, EPG):
            upd = v[i] > b1
            i1, b1, s1 = jnp.where(upd, i, i1), jnp.where(upd, v[i], b1), jnp.where(upd, sv[i], s1)
        i2, b2, s2 = jnp.zeros((1, tl), jnp.int32), jnp.full((1, tl), ninf, F32), sv[0]
        for i in range(EPG):
            cand = jnp.where(i1 == i, ninf, v[i])
            upd = cand > b2
            i2, b2, s2 = jnp.where(upd, i, i2), jnp.where(upd, cand, b2), jnp.where(upd, sv[i], s2)
        e0 = gi * EPG + i1
        e1 = gi * EPG + i2
        den = s1 + s2
        asg_ref[2:3, ls] = s1 / den
        asg_ref[3:4, ls] = s2 / den
        asg_ref[4:5, ls] = e0.astype(F32)
        asg_ref[5:6, ls] = e1.astype(F32)
        m = jnp.where((eio == e0) | (eio == e1), 1.0, 0.0)
        rank_ref[:, ls] = jnp.dot(m.astype(BF16), upper, preferred_element_type=F32) + carry
        return carry + jnp.sum(m, axis=1, keepdims=True)

    counts = lax.fori_loop(0, n_tok // tl, select, jnp.zeros((E, 1), F32))
    ptiles = jnp.floor((counts + (MOE_TM - 1)) * (1.0 / MOE_TM))
    run = jnp.zeros((1, 1), F32)
    pstart_t = jnp.zeros((E, 1), F32)
    for e in range(E):
        pstart_t = jnp.where(ecol == e, run, pstart_t)
        run = run + ptiles[e:e + 1, :]
    pstart = pstart_t * MOE_TM

    def place(j, _):
        ls = pl.ds(pl.multiple_of(j * tl, tl), tl)
        dest = rank_ref[:, ls] + pstart
        e0 = asg_ref[4:5, ls].astype(jnp.int32)
        e1 = asg_ref[5:6, ls].astype(jnp.int32)
        asg_ref[0:1, ls] = jnp.sum(jnp.where(eio == e0, dest, 0.0), axis=0, keepdims=True)
        asg_ref[1:2, ls] = jnp.sum(jnp.where(eio == e1, dest, 0.0), axis=0, keepdims=True)
        asg_ref[6:8, ls] = jnp.zeros((2, tl), F32)
        return 0

    lax.fori_loop(0, n_tok // tl, place, 0)

    ntp = tile_ref.shape[1]
    pend_t = pstart_t + ptiles
    ti = lax.broadcasted_iota(jnp.int32, (1, ntp), 1).astype(F32)
    te = jnp.sum(jnp.where(pend_t <= ti, 1.0, 0.0), axis=0, keepdims=True)
    used = ti < run
    last_e = jnp.max(jnp.where(counts > 0.0, ecol.astype(F32), 0.0), axis=0, keepdims=True)
    te = jnp.where(used, jnp.minimum(te, E - 1.0), last_e)
    tile_ref[0:1, :] = te.astype(jnp.int32)
    tile_ref[1:2, :] = jnp.where(used, 1, 0).astype(jnp.int32)
    tile_ref[2:8, :] = jnp.zeros((6, ntp), jnp.int32)


def _route(logits_t, router_bias, ntile):
    E, N = logits_t.shape
    assert MOE_TM & (MOE_TM - 1) == 0
    ntp = -(-ntile // LANE) * LANE
    full = lambda shape: pl.BlockSpec(shape, lambda i: (0, 0))
    return pl.pallas_call(
        functools.partial(_route_kernel, tl=_pick(512, N)), grid=(1,),
        in_specs=[full((E, N)), full((E, 1))],
        out_specs=[full((8, N)), full((8, ntp))],
        out_shape=[jax.ShapeDtypeStruct((8, N), F32), jax.ShapeDtypeStruct((8, ntp), jnp.int32)],
        scratch_shapes=[pltpu.VMEM((E, N), F32)],
        compiler_params=_cp(("arbitrary",)), name="moe_route",
    )(logits_t, router_bias.astype(F32).reshape(E, 1))


def _moe_kernel(te_ref, tv_ref, x_ref, w1_ref, w3_ref, w2_ref, o_ref):
    valid = tv_ref[pl.program_id(0)] != 0

    @pl.when(valid)
    def _():
        x = x_ref[...]
        a = jnp.dot(x, w1_ref[0], preferred_element_type=F32)
        b = jnp.dot(x, w3_ref[0], preferred_element_type=F32)
        hid = ((a * jax.nn.sigmoid(a)) * b).astype(BF16)
        o_ref[...] = jnp.dot(hid, w2_ref[0], preferred_element_type=F32)

    @pl.when(jnp.logical_not(valid))
    def _():
        o_ref[...] = jnp.zeros(o_ref.shape, o_ref.dtype)


def _moe(xin, w1, w3, w2, tile_e, tile_valid):
    cap, D = xin.shape
    F = w1.shape[2]
    TM = MOE_TM
    grid_spec = pltpu.PrefetchScalarGridSpec(
        num_scalar_prefetch=2, grid=(cap // TM,),
        in_specs=[pl.BlockSpec((TM, D), lambda i, te, tv: (i, 0)),
                  pl.BlockSpec((1, D, F), lambda i, te, tv: (te[i], 0, 0)),
                  pl.BlockSpec((1, D, F), lambda i, te, tv: (te[i], 0, 0)),
                  pl.BlockSpec((1, F, D), lambda i, te, tv: (te[i], 0, 0))],
        out_specs=pl.BlockSpec((TM, D), lambda i, te, tv: (i, 0)))
    return pl.pallas_call(
        _moe_kernel, grid_spec=grid_spec,
        out_shape=jax.ShapeDtypeStruct((cap, D), F32),
        compiler_params=_cp(("arbitrary",), 56), name="moe_experts",
    )(tile_e, tile_valid, xin, w1, w3, w2)


def _combine_kernel(x_ref, y0_ref, y1_ref, w0_ref, w1_ref, g_ref, o_ref):
    o_ref[...] = x_ref[...] + g_ref[0] * (y0_ref[...] * w0_ref[...] + y1_ref[...] * w1_ref[...])


def _combine(x, y0, y1, w0, w1, gate, midx, tm):
    N, D = x.shape
    blk = pl.BlockSpec((tm, D), lambda i: (i, 0))
    col = pl.BlockSpec((tm, 1), lambda i: (i, 0))
    return pl.pallas_call(
        _combine_kernel, grid=(N // tm,),
        in_specs=[blk, blk, blk, col, col, pl.BlockSpec((1, 1, D), lambda i: (midx(i), 0, 0))],
        out_specs=blk, out_shape=jax.ShapeDtypeStruct((N, D), F32),
        compiler_params=_cp(("parallel",)), name="moe_residual",
    )(x, y0, y1, w0, w1, gate)


def _rope_tables(n_tokens):
    rows = n_tokens // GRID_W
    row = jnp.repeat(jnp.arange(rows, dtype=F32), GRID_W)
    col = jnp.tile(jnp.arange(GRID_W, dtype=F32), rows)
    inv = ROPE_BASE ** (-jnp.arange(0, AX_DIM, 2, dtype=F32) / AX_DIM)
    ar = row[:, None] * inv[None, :]
    ac = col[:, None] * inv[None, :]
    ang = jnp.tile(jnp.concatenate([ar, ar, ac, ac], axis=-1), (1, LANE // DA_HEAD))
    first = (jnp.arange(LANE) % (AX_DIM)) < (AX_DIM // 2)
    return jnp.cos(ang), jnp.where(first[None, :], -jnp.sin(ang), jnp.sin(ang))


def kernel(x_prompt, x_sample, cache_k, cache_v, state_gla, state_s5_re, state_s5_im, c, c_ctx, w_mod, b_mod, g_norm, w_in, da_lambda_q, da_lambda_k, da_subln_g, gla_w_alpha, gla_b_alpha, gla_norm_g, s5_a_re, s5_a_im, s5_log_dt, s5_b_re, s5_b_im, s5_c_re, s5_c_im, s5_d, s5_w_glu, s5_b_glu, w_branch, w_gate, b_gate, w_out, router_w, router_bias, moe_w1, moe_w3, moe_w2, g_final):
    B, T, D = x_prompt.shape
    DB, DT, _ = x_sample.shape
    L = w_in.shape[0]
    PAST = cache_k.shape[2]
    NP, NS = B * T, DB * DT
    N = NP + NS
    nseq = B + DB
    assert D == D_MODEL and 1 + DB <= 8 and NP % DT == 0 and T % GLA_ROWS == 0 and DT % GLA_ROWS == 0
    tm = _pick(512, NP, DT)
    midx = functools.partial(_mod_index, tm=tm, n_ctx_tok=NP, dec_seq=DT)

    x = jnp.concatenate([x_prompt.reshape(NP, D), x_sample.reshape(NS, D)], axis=0)
    cv = jnp.zeros((8, D), F32).at[0].set(c_ctx).at[1:1 + DB].set(c)
    mods = _mod(cv, w_mod, b_mod).reshape(L, 8, N_MOD, 1, D)
    zero_mod = jnp.zeros((8, 1, D), F32)

    rope_tabs = _rope_tables(DT)
    ck = cache_k.reshape(DB, L, PAST, DA_HEADS * 2 * DA_HEAD)
    cvv = cache_v.reshape(DB, L, PAST, DA_HEADS * DA_VDIM)
    r = jnp.arange(GLA_ROWS)
    same = (r[:, None] // GLA_CHUNK) == (r[None, :] // GLA_CHUNK)
    tri = jnp.stack([same & (r[None, :] <= r[:, None]), same & (r[None, :] >= r[:, None])]).astype(BF16)
    rw = router_w.astype(F32).T
    rw_hi = rw.astype(BF16)
    rw_lo = (rw - rw_hi.astype(F32)).astype(BF16)
    ntile = N * TOP_K // MOE_TM + N_EXPERTS
    tok2 = jnp.tile(jnp.arange(N, dtype=jnp.int32), TOP_K)
    tq_ctx = _pick(256, T)
    tq_lat = _pick(256, DT)
    s5_segments = ((0, B, T // S5_L), (NP // S5_L, DB, DT // S5_L))

    k_list, v_list, g_list, re_list, im_list = [], [], [], [], []
    for l in range(L):
        m = mods[l]
        shift1, scale1, gate1, shift2, scale2, gate2 = (m[:, i] for i in range(N_MOD))
        wi = w_in[l]
        w_in_p = jnp.concatenate([wi[:, :C_S5], wi[:, C_S5 + GLA_RANK:], wi[:, C_S5:C_S5 + GLA_RANK],
                                  jnp.zeros((D, P_COLS - C_GA - GLA_RANK), F32)], axis=1).astype(BF16)
        h = _norm(x, g_norm[l, 0][None], scale1, shift1, midx, tm, BF16)
        p = _matmul(h, w_in_p, tm, 1536, F32, "in_proj")

        lam_init = 0.8 - 0.6 * math.exp(-0.3 * l)
        lq, lk = da_lambda_q[l], da_lambda_k[l]
        lam = jnp.exp(jnp.sum(lq[0] * lk[0])) - jnp.exp(jnp.sum(lq[1] * lk[1])) + lam_init
        lam_row = jnp.full((1, LANE), lam, F32)
        g_row = (da_subln_g[l] * (1.0 - lam_init))[None]
        o_da = jnp.concatenate([
            _attention(p, lam_row, g_row, 0, B, T, tq_ctx),
            _attention(p, lam_row, g_row, NP, DB, DT, tq_lat, rope_tabs=rope_tabs, ctx=(ck, cvv, l))], axis=0)

        wa = jnp.zeros((1, 2, LANE, GLA_KTOT), F32).at[0, :, :GLA_RANK].set(gla_w_alpha[l]).astype(BF16)
        ba = gla_b_alpha[l].reshape(1, 2, 1, GLA_KTOT)
        s0_all = jnp.concatenate([jnp.zeros((B, 2, GLA_HEADS, GLA_DK, GLA_DV), F32), state_gla[:, l]], axis=0)
        o_f, o_b, gla_fin = _gla(p, wa, ba, tri, s0_all, ((B, T), (DB, DT)))
        o_gla = _gla_post(o_f, o_b, p, gla_norm_g[l][None], tm)

        uc = p[:, C_S5:C_S5 + S5_WIDTH].reshape(N // S5_L, S5_L, S5_GROUPS, S5_CH).transpose(0, 2, 1, 3)
        uc = uc.reshape(N // S5_L, S5_GROUPS * S5_L * S5_CH).astype(BF16)
        mats = _s5_matrices(s5_a_re[l], s5_a_im[l], s5_log_dt[l], s5_b_re[l], s5_b_im[l], s5_c_re[l], s5_c_im[l])

        def pack_state(s_lat):
            s = jnp.concatenate([jnp.zeros((B, 2, S5_GROUPS, S5_STATE), F32), s_lat], axis=0)
            return s.reshape(2 * nseq, S5_PAIRS, 2 * S5_STATE).transpose(1, 0, 2)

        x0 = jnp.stack([pack_state(state_s5_re[:, l]), pack_state(state_s5_im[:, l])])
        y, s5_fin = _s5(uc, mats, x0, s5_segments)
        y = y.reshape(N // S5_L, S5_GROUPS, S5_L, S5_CH).transpose(0, 2, 1, 3).reshape(N, S5_WIDTH)
        o_s5 = _s5_post(y, p, s5_d[l][None], s5_w_glu[l].astype(BF16), s5_b_glu[l][None], tm)
        s5_fin = s5_fin.transpose(0, 2, 1, 3).reshape(2, nseq, 2, S5_GROUPS, S5_STATE)[:, :B]

        merged = _merge(h, o_da, o_gla, o_s5, w_gate[l].astype(BF16), b_gate[l][None], w_branch[l].astype(BF16),
                        tm, 512)
        x = _wout(merged, w_out[l].astype(BF16), x, gate1, midx, tm, 1024)

        h2, logits_t = _norm(x, g_norm[l, 1][None], scale2, shift2, midx, tm, BF16, router=(rw_hi, rw_lo))
        asg, tiles = _route(logits_t, router_bias, ntile)
        slot = asg[0:2].astype(jnp.int32)
        buf_tok = jnp.zeros((ntile * MOE_TM,), jnp.int32).at[slot.reshape(-1)].set(tok2, unique_indices=True)
        xin = jnp.take(h2, buf_tok, axis=0)
        ye = _moe(xin, moe_w1[l].astype(BF16), moe_w3[l].astype(BF16), moe_w2[l].astype(BF16), tiles[0, :ntile],
                  tiles[1, :ntile])
        x = _combine(x, jnp.take(ye, slot[0], axis=0), jnp.take(ye, slot[1], axis=0), asg[2].reshape(N, 1),
                     asg[3].reshape(N, 1), gate2, midx, tm)

        k_list.append(p[:NP, C_K:C_K + 1024].reshape(B, T, DA_HEADS, 2 * DA_HEAD))
        v_list.append(p[:NP, C_V:C_V + 1024].reshape(B, T, DA_HEADS, DA_VDIM))
        g_list.append(gla_fin[:B])
        re_list.append(s5_fin[0])
        im_list.append(s5_fin[1])

    y_all = _norm(x, g_final[None], zero_mod, zero_mod, midx, tm, F32)
    return (y_all[:NP].reshape(B, T, D), y_all[NP:].reshape(DB, DT, D),
            jnp.stack(k_list, axis=1), jnp.stack(v_list, axis=1), jnp.stack(g_list, axis=1),
            jnp.stack(re_list, axis=1), jnp.stack(im_list, axis=1))
```

```python
import functools
import math

import jax
import jax.numpy as jnp
from jax import lax
from jax.experimental import pallas as pl
from jax.experimental.pallas import tpu as pltpu

F32 = jnp.float32
BF16 = jnp.bfloat16

D_MODEL = 2048
GRID_W = 64
EPS = 1e-6
N_MOD = 6
N_BRANCH = 3
DA_HEADS = 8
DA_HEAD = 64
DA_VDIM = 128
ROPE_BASE = 10000.0
AX_DIM = DA_HEAD // 2
GLA_HEADS = 4
GLA_DK = 128
GLA_DV = 256
GLA_KTOT = GLA_HEADS * GLA_DK
GLA_WIDTH = GLA_HEADS * GLA_DV
GLA_RANK = 16
GLA_TAU = 16.0
GLA_CHUNK = 64
S5_WIDTH = 1024
S5_CH = 16
S5_GROUPS = S5_WIDTH // S5_CH
S5_STATE = 64
N_EXPERTS = 16
N_EXPERT_GROUPS = 4
EXPERTS_PER_GROUP = N_EXPERTS // N_EXPERT_GROUPS
TOPK_GROUP = 1
TOP_K = 2
D_FF_EXPERT = 1408

LANE = 128
C_Q, C_K, C_V, C_GQ, C_GK, C_GV, C_GR, C_S5, C_GA = 0, 1024, 2048, 3072, 3584, 4096, 5120, 6144, 7168
P_COLS = 7680
GLA_ROWS = 256
S5_L = 16
S5_SG = S5_WIDTH // LANE
S5_W = S5_L * LANE
MOE_TM = 512
MOE_FC = 512
ATT_KB = 512
VMEM_MB = 1024 * 1024


def _cp(sem, vmem_mb=48):
    return pltpu.CompilerParams(dimension_semantics=sem, vmem_limit_bytes=vmem_mb * VMEM_MB)


def _pick(pref, *ns):
    t = pref
    while any(n % t for n in ns):
        t //= 2
    return t


def _mod_index(i, tm, n_ctx_tok, dec_seq):
    t0 = i * tm
    return jnp.where(t0 < n_ctx_tok, 0, 1 + (t0 - n_ctx_tok) // dec_seq)


def _mod_kernel(c_ref, w_ref, b_ref, o_ref):
    c = c_ref[...]
    s = (c * jax.nn.sigmoid(c)).astype(BF16)
    o_ref[0] = jnp.dot(s, w_ref[0].astype(BF16), preferred_element_type=F32) + b_ref[0]


def _mod(cv, w_mod, b_mod):
    L, D, M = w_mod.shape
    tn = 1024
    return pl.pallas_call(
        _mod_kernel,
        grid=(L, M // tn),
        in_specs=[pl.BlockSpec((8, D), lambda l, j: (0, 0)),
                  pl.BlockSpec((1, D, tn), lambda l, j: (l, 0, j)),
                  pl.BlockSpec((1, 1, tn), lambda l, j: (l, 0, j))],
        out_specs=pl.BlockSpec((1, 8, tn), lambda l, j: (l, 0, j)),
        out_shape=jax.ShapeDtypeStruct((L, 8, M), F32),
        compiler_params=_cp(("parallel", "parallel")),
        name="mod_vectors",
    )(cv, w_mod, b_mod.reshape(L, 1, M))


def _norm_core(x_ref, g_ref, sc_ref, sh_ref):
    x = x_ref[...]
    y = x * lax.rsqrt(jnp.mean(x * x, axis=-1, keepdims=True) + EPS)
    return (y * g_ref[...]) * (1.0 + sc_ref[0]) + sh_ref[0]


def _norm_kernel(x_ref, g_ref, sc_ref, sh_ref, o_ref):
    o_ref[...] = _norm_core(x_ref, g_ref, sc_ref, sh_ref).astype(o_ref.dtype)


def _norm_router_kernel(x_ref, g_ref, sc_ref, sh_ref, rh_ref, rl_ref, o_ref, lg_ref):
    h = _norm_core(x_ref, g_ref, sc_ref, sh_ref)
    hh = h.astype(BF16)
    hl = (h - hh.astype(F32)).astype(BF16)
    o_ref[...] = hh
    nt = (((1,), (1,)), ((), ()))
    lg_ref[...] = (lax.dot_general(rh_ref[...], hh, nt, preferred_element_type=F32)
                   + lax.dot_general(rh_ref[...], hl, nt, preferred_element_type=F32)
                   + lax.dot_general(rl_ref[...], hh, nt, preferred_element_type=F32))


def _norm(x, g, sc, sh, midx, tm, out_dtype, router=None):
    N, D = x.shape
    in_specs = [pl.BlockSpec((tm, D), lambda i: (i, 0)),
                pl.BlockSpec((1, D), lambda i: (0, 0)),
                pl.BlockSpec((1, 1, D), lambda i: (midx(i), 0, 0)),
                pl.BlockSpec((1, 1, D), lambda i: (midx(i), 0, 0))]
    if router is None:
        return pl.pallas_call(
            _norm_kernel, grid=(N // tm,), in_specs=in_specs,
            out_specs=pl.BlockSpec((tm, D), lambda i: (i, 0)),
            out_shape=jax.ShapeDtypeStruct((N, D), out_dtype),
            compiler_params=_cp(("parallel",)), name="norm_mod",
        )(x, g, sc, sh)
    rh, rl = router
    E = rh.shape[0]
    in_specs += [pl.BlockSpec((E, D), lambda i: (0, 0)), pl.BlockSpec((E, D), lambda i: (0, 0))]
    return pl.pallas_call(
        _norm_router_kernel, grid=(N // tm,), in_specs=in_specs,
        out_specs=[pl.BlockSpec((tm, D), lambda i: (i, 0)), pl.BlockSpec((E, tm), lambda i: (0, i))],
        out_shape=[jax.ShapeDtypeStruct((N, D), BF16), jax.ShapeDtypeStruct((E, N), F32)],
        compiler_params=_cp(("parallel",)), name="norm_mod_router",
    )(x, g, sc, sh, rh, rl)


def _mm_kernel(a_ref, w_ref, o_ref):
    o_ref[...] = jnp.dot(a_ref[...], w_ref[...], preferred_element_type=F32).astype(o_ref.dtype)


def _matmul(a, w, tm, tn, out_dtype, name):
    M, K = a.shape
    Nc = w.shape[1]
    return pl.pallas_call(
        _mm_kernel, grid=(Nc // tn, M // tm),
        in_specs=[pl.BlockSpec((tm, K), lambda j, i: (i, 0)), pl.BlockSpec((K, tn), lambda j, i: (0, j))],
        out_specs=pl.BlockSpec((tm, tn), lambda j, i: (i, j)),
        out_shape=jax.ShapeDtypeStruct((M, Nc), out_dtype),
        compiler_params=_cp(("parallel", "parallel")), name=name,
    )(a, w)


def _rope(x, cos, sin_signed):
    lane = lax.broadcasted_iota(jnp.int32, (1, LANE), 1)
    first = (lane % (2 * (AX_DIM // 2))) < (AX_DIM // 2)
    xr = jnp.where(first, pltpu.roll(x, LANE - AX_DIM // 2, 1), pltpu.roll(x, AX_DIM // 2, 1))
    return x * cos + xr * sin_signed


def _attn_kernel(*refs, rope, ctx, hs, kb):
    refs = list(refs)
    q_ref, k_ref, v_ref = refs[:3]
    refs = refs[3:]
    if ctx:
        ck_ref, cv_ref = refs[:2]
        refs = refs[2:]
    if rope:
        cq_ref, sq_ref, ckk_ref, skk_ref = refs[:4]
        refs = refs[4:]
    lam_ref, g_ref, o_ref, kb_ref, vb_ref = refs[:5]
    if ctx:
        ckb_ref, cvb_ref = refs[5:]

    hsl = [slice(h * LANE, (h + 1) * LANE) for h in range(hs)]

    @pl.when(pl.program_id(2) == 0)
    def _():
        for h in range(hs):
            k = k_ref[:, hsl[h]]
            if rope:
                k = _rope(k, ckk_ref[...], skk_ref[...])
            kb_ref[:, hsl[h]] = k.astype(BF16)
        vb_ref[...] = v_ref[...].astype(BF16)
        if ctx:
            ckb_ref[...] = ck_ref[...].astype(BF16)
            cvb_ref[...] = cv_ref[...].astype(BF16)

    tq = q_ref.shape[0]
    n_new = kb_ref.shape[0]
    lane = lax.broadcasted_iota(jnp.int32, (1, LANE), 1)
    lo = lane < DA_HEAD
    nt = (((1,), (1,)), ((), ()))
    qqs = []
    for h in range(hs):
        q = q_ref[:, hsl[h]]
        if rope:
            q = _rope(q, cq_ref[...], sq_ref[...])
        q = q * (DA_HEAD ** -0.5 * math.log2(math.e))
        qqs.append(jnp.concatenate([jnp.where(lo, q, 0.0), jnp.where(lo, 0.0, q)], axis=0).astype(BF16))

    def block(qq, kblk, vblk, carry):
        m, l, acc = carry
        s = lax.dot_general(qq, kblk, nt, preferred_element_type=F32)
        mn = jnp.maximum(m, jnp.max(s, axis=-1, keepdims=True))
        e = jnp.exp2(s - mn)
        r = jnp.exp2(m - mn)
        l = r * l + jnp.sum(e, axis=-1, keepdims=True)
        acc = r * acc + jnp.dot(e.astype(BF16), vblk, preferred_element_type=F32)
        return mn, l, acc

    def all_heads(kref, vref, rows, carries):
        return tuple(block(qqs[h], kref[rows, hsl[h]], vref[rows, hsl[h]], carries[h]) for h in range(hs))

    carries = tuple((jnp.full((2 * tq, 1), -jnp.inf, F32), jnp.zeros((2 * tq, 1), F32),
                     jnp.zeros((2 * tq, LANE), F32)) for _ in range(hs))
    if ctx:
        for c in range(ckb_ref.shape[0] // kb):
            carries = all_heads(ckb_ref, cvb_ref, slice(c * kb, (c + 1) * kb), carries)
    if n_new // kb == 1:
        carries = all_heads(kb_ref, vb_ref, slice(None), carries)
    else:
        def body(j, carries):
            return all_heads(kb_ref, vb_ref, pl.ds(pl.multiple_of(j * kb, kb), kb), carries)
        carries = lax.fori_loop(0, n_new // kb, body, carries)
    for h in range(hs):
        _, l, acc = carries[h]
        o = acc[:tq] * (1.0 / l[:tq]) - acc[tq:] * (lam_ref[:, :1] / l[tq:])
        y = o * lax.rsqrt(jnp.mean(o * o, axis=-1, keepdims=True) + EPS)
        o_ref[:, hsl[h]] = (y * g_ref[...]).astype(o_ref.dtype)


def _attention(p, lam_row, g_row, row0, B, T, tq, hs, kb, rope_tabs=None, ctx=None):
    nq = T // tq
    rb0 = row0 // tq
    kb0 = row0 // T
    W = hs * LANE
    hq, hk, hv = C_Q // W, C_K // W, C_V // W
    assert T % kb == 0
    in_specs = [pl.BlockSpec((tq, W), lambda b, h, i: (rb0 + b * nq + i, hq + h)),
                pl.BlockSpec((T, W), lambda b, h, i: (kb0 + b, hk + h)),
                pl.BlockSpec((T, W), lambda b, h, i: (kb0 + b, hv + h))]
    args = [p, p, p]
    scratch = [pltpu.VMEM((T, W), BF16), pltpu.VMEM((T, W), BF16)]
    if ctx is not None:
        ck, cv, layer = ctx
        past = ck.shape[2]
        assert past % kb == 0
        in_specs += [pl.BlockSpec((None, None, past, W), lambda b, h, i: (b, layer, 0, h)),
                     pl.BlockSpec((None, None, past, W), lambda b, h, i: (b, layer, 0, h))]
        args += [ck, cv]
        scratch += [pltpu.VMEM((past, W), BF16), pltpu.VMEM((past, W), BF16)]
    if rope_tabs is not None:
        cos, sin = rope_tabs
        in_specs += [pl.BlockSpec((tq, LANE), lambda b, h, i: (i, 0)), pl.BlockSpec((tq, LANE), lambda b, h, i: (i, 0)),
                     pl.BlockSpec((T, LANE), lambda b, h, i: (0, 0)), pl.BlockSpec((T, LANE), lambda b, h, i: (0, 0))]
        args += [cos, sin, cos, sin]
    in_specs += [pl.BlockSpec((1, LANE), lambda b, h, i: (0, 0)), pl.BlockSpec((1, LANE), lambda b, h, i: (0, 0))]
    args += [lam_row, g_row]
    return pl.pallas_call(
        functools.partial(_attn_kernel, rope=rope_tabs is not None, ctx=ctx is not None, hs=hs, kb=kb),
        grid=(B, DA_HEADS // hs, nq),
        in_specs=in_specs,
        out_specs=pl.BlockSpec((tq, W), lambda b, h, i: (b * nq + i, h)),
        out_shape=jax.ShapeDtypeStruct((B * T, DA_HEADS * DA_VDIM), BF16),
        scratch_shapes=scratch,
        compiler_params=_cp(("parallel", "parallel", "arbitrary"), 56),
        name="diff_attention_latent" if ctx is not None else "diff_attention_context",
    )(*args)


def _attn_latent_kernel(q_ref, k_ref, v_ref, ck_ref, cv_ref, cq_ref, sq_ref, ckk_ref, skk_ref, lam_ref, g_ref, o_ref,
                        kb_ref, vb_ref):
    @pl.when(pl.program_id(2) == 0)
    def _():
        kb_ref[...] = _rope(k_ref[...], ckk_ref[...], skk_ref[...]).astype(BF16)
        vb_ref[...] = v_ref[...].astype(BF16)

    q = _rope(q_ref[...], cq_ref[...], sq_ref[...]) * (DA_HEAD ** -0.5)
    lane = lax.broadcasted_iota(jnp.int32, (1, LANE), 1)
    lo = lane < DA_HEAD
    q0 = jnp.where(lo, q, 0.0).astype(BF16)
    q1 = jnp.where(lo, 0.0, q).astype(BF16)
    nt = (((1,), (1,)), ((), ()))
    kb = kb_ref[...]
    ckb = ck_ref[...].astype(BF16)
    s0 = lax.dot_general(q0, kb, nt, preferred_element_type=F32)
    s1 = lax.dot_general(q1, kb, nt, preferred_element_type=F32)
    c0 = lax.dot_general(q0, ckb, nt, preferred_element_type=F32)
    c1 = lax.dot_general(q1, ckb, nt, preferred_element_type=F32)
    m0 = jnp.maximum(jnp.max(s0, axis=-1, keepdims=True), jnp.max(c0, axis=-1, keepdims=True))
    m1 = jnp.maximum(jnp.max(s1, axis=-1, keepdims=True), jnp.max(c1, axis=-1, keepdims=True))
    e0 = jnp.exp(s0 - m0)
    e1 = jnp.exp(s1 - m1)
    ec0 = jnp.exp(c0 - m0)
    ec1 = jnp.exp(c1 - m1)
    l0 = jnp.sum(e0, axis=-1, keepdims=True) + jnp.sum(ec0, axis=-1, keepdims=True)
    l1 = jnp.sum(e1, axis=-1, keepdims=True) + jnp.sum(ec1, axis=-1, keepdims=True)
    a0 = 1.0 / l0
    a1 = lam_ref[:, :1] / l1
    o = jnp.dot((e0 * a0 - e1 * a1).astype(BF16), vb_ref[...], preferred_element_type=F32)
    o = o + jnp.dot((ec0 * a0 - ec1 * a1).astype(BF16), cv_ref[...].astype(BF16), preferred_element_type=F32)
    y = o * lax.rsqrt(jnp.mean(o * o, axis=-1, keepdims=True) + EPS)
    o_ref[...] = (y * g_ref[...]).astype(o_ref.dtype)


def _attention_latent(p, lam_row, g_row, row0, B, T, tq, rope_tabs, ck, cv, layer):
    nq = T // tq
    rb0 = row0 // tq
    kb0 = row0 // T
    hq, hk, hv = C_Q // LANE, C_K // LANE, C_V // LANE
    past = ck.shape[2]
    cos, sin = rope_tabs
    in_specs = [pl.BlockSpec((tq, LANE), lambda b, h, i: (rb0 + b * nq + i, hq + h)),
                pl.BlockSpec((T, LANE), lambda b, h, i: (kb0 + b, hk + h)),
                pl.BlockSpec((T, LANE), lambda b, h, i: (kb0 + b, hv + h)),
                pl.BlockSpec((None, None, past, LANE), lambda b, h, i: (b, layer, 0, h)),
                pl.BlockSpec((None, None, past, LANE), lambda b, h, i: (b, layer, 0, h)),
                pl.BlockSpec((tq, LANE), lambda b, h, i: (i, 0)), pl.BlockSpec((tq, LANE), lambda b, h, i: (i, 0)),
                pl.BlockSpec((T, LANE), lambda b, h, i: (0, 0)), pl.BlockSpec((T, LANE), lambda b, h, i: (0, 0)),
                pl.BlockSpec((1, LANE), lambda b, h, i: (0, 0)), pl.BlockSpec((1, LANE), lambda b, h, i: (0, 0))]
    return pl.pallas_call(
        _attn_latent_kernel, grid=(B, DA_HEADS, nq), in_specs=in_specs,
        out_specs=pl.BlockSpec((tq, LANE), lambda b, h, i: (b * nq + i, h)),
        out_shape=jax.ShapeDtypeStruct((B * T, DA_HEADS * DA_VDIM), BF16),
        scratch_shapes=[pltpu.VMEM((T, LANE), BF16), pltpu.VMEM((T, LANE), BF16)],
        compiler_params=_cp(("parallel", "parallel", "arbitrary"), 56),
        name="diff_attention_latent",
    )(p, p, p, ck, cv, cos, sin, cos, sin, lam_row, g_row)


def _log_sigmoid(z):
    return jnp.minimum(z, 0.0) - jnp.log1p(jnp.exp(-jnp.abs(z)))


def _gla_direction(d, q_ref, k_ref, v_ref, ga_ref, wa_ref, ba_ref, tri_ref, o_ref, st_ref):
    C = GLA_CHUNK
    nsub = GLA_ROWS // C
    z = jnp.dot(ga_ref[...].astype(BF16), wa_ref[0, d], preferred_element_type=F32) + ba_ref[0, d]
    la = _log_sigmoid(z) / GLA_TAU
    hi = la.astype(BF16)
    r1 = la - hi.astype(F32)
    mid = r1.astype(BF16)
    low = (r1 - mid.astype(F32)).astype(BF16)
    tri = tri_ref[d]
    bsum = (jnp.dot(tri, hi, preferred_element_type=F32) + jnp.dot(tri, mid, preferred_element_type=F32)
            + jnp.dot(tri, low, preferred_element_type=F32))
    row = lax.broadcasted_iota(jnp.int32, (C, C), 0)
    col = lax.broadcasted_iota(jnp.int32, (C, C), 1)
    mask = (row >= col) if d == 0 else (row <= col)
    nt = (((1,), (1,)), ((), ()))
    tn = (((0,), (0,)), ((), ()))
    subs = range(nsub) if d == 0 else range(nsub - 1, -1, -1)
    for h in range(GLA_HEADS):
        hk = slice(h * GLA_DK, (h + 1) * GLA_DK)
        hv = slice(h * GLA_DV, (h + 1) * GLA_DV)
        st = st_ref[d, h]
        for s in subs:
            rs = slice(s * C, (s + 1) * C)
            end = s * C + C - 1 if d == 0 else s * C
            bb = bsum[rs, hk]
            bl = bsum[end:end + 1, hk]
            q = q_ref[rs, hk] * (GLA_DK ** -0.5)
            k = k_ref[rs, hk]
            v = v_ref[rs, hv].astype(BF16)
            qt = (q * jnp.exp(bb)).astype(BF16)
            kt = (k * jnp.exp(-bb)).astype(BF16)
            kend = (k * jnp.exp(bl - bb)).astype(BF16)
            att = jnp.where(mask, lax.dot_general(qt, kt, nt, preferred_element_type=F32), 0.0).astype(BF16)
            o = jnp.dot(att, v, preferred_element_type=F32)
            o = o + lax.dot_general(qt, st.astype(BF16), nt, preferred_element_type=F32)
            o_ref[rs, hv] = o
            st = st * jnp.exp(bl) + lax.dot_general(v, kend, tn, preferred_element_type=F32)
        st_ref[d, h] = st


def _gla_kernel(qf, kf, vf, gaf, qb, kb, vb, gab, wa_ref, ba_ref, tri_ref, s0_ref, of_ref, ob_ref, fin_ref, st_ref,
                *, seq_pos):
    first, last, stateless = seq_pos(pl.program_id(0))

    @pl.when(first & stateless)
    def _():
        st_ref[...] = jnp.zeros(st_ref.shape, F32)

    @pl.when(first & jnp.logical_not(stateless))
    def _():
        for d in range(2):
            for h in range(GLA_HEADS):
                st_ref[d, h] = s0_ref[d, h].T

    _gla_direction(0, qf, kf, vf, gaf, wa_ref, ba_ref, tri_ref, of_ref, st_ref)
    _gla_direction(1, qb, kb, vb, gab, wa_ref, ba_ref, tri_ref, ob_ref, st_ref)

    @pl.when(last & stateless)
    def _():
        for d in range(2):
            for h in range(GLA_HEADS):
                fin_ref[d, h] = st_ref[d, h].T


def _gla(p, wa, ba, tri, state, layer, seqs):
    N = p.shape[0]
    R = GLA_ROWS
    (n1, t1), (n2, t2) = seqs
    nb1, nb2 = t1 // R, t2 // R
    steps1 = n1 * nb1

    def locate(s):
        in1 = s < steps1
        s2 = s - steps1
        nb = jnp.where(in1, nb1, nb2)
        seq = jnp.where(in1, s // nb1, n1 + s2 // nb2)
        n = jnp.where(in1, s % nb1, s2 % nb2)
        base = jnp.where(in1, (s // nb1) * nb1, steps1 + (s2 // nb2) * nb2)
        return seq, n, nb, base

    def fwd_blk(s):
        _, n, _, base = locate(s)
        return base + n

    def bwd_blk(s):
        _, n, nb, base = locate(s)
        return base + nb - 1 - n

    def seq_pos(s):
        _, n, nb, _ = locate(s)
        return n == 0, n == nb - 1, s < steps1

    def specs(blk):
        return [pl.BlockSpec((R, GLA_KTOT), lambda s: (blk(s), C_GQ // GLA_KTOT)),
                pl.BlockSpec((R, GLA_KTOT), lambda s: (blk(s), C_GK // GLA_KTOT)),
                pl.BlockSpec((R, GLA_WIDTH), lambda s: (blk(s), C_GV // GLA_WIDTH)),
                pl.BlockSpec((R, LANE), lambda s: (blk(s), C_GA // LANE))]

    s0_spec = pl.BlockSpec((None, None, 2, GLA_HEADS, GLA_DK, GLA_DV),
                           lambda s: (jnp.maximum(locate(s)[0] - n1, 0), layer, 0, 0, 0, 0))
    fin_spec = pl.BlockSpec((None, 2, GLA_HEADS, GLA_DK, GLA_DV),
                            lambda s: (jnp.minimum(locate(s)[0], n1 - 1), 0, 0, 0, 0))
    return pl.pallas_call(
        functools.partial(_gla_kernel, seq_pos=seq_pos),
        grid=(N // R,),
        in_specs=specs(fwd_blk) + specs(bwd_blk) + [
            pl.BlockSpec((1, 2, LANE, GLA_KTOT), lambda s: (0, 0, 0, 0)),
            pl.BlockSpec((1, 2, 1, GLA_KTOT), lambda s: (0, 0, 0, 0)),
            pl.BlockSpec((2, R, R), lambda s: (0, 0, 0)),
            s0_spec],
        out_specs=[pl.BlockSpec((R, GLA_WIDTH), lambda s: (fwd_blk(s), 0)),
                   pl.BlockSpec((R, GLA_WIDTH), lambda s: (bwd_blk(s), 0)),
                   fin_spec],
        out_shape=[jax.ShapeDtypeStruct((N, GLA_WIDTH), F32), jax.ShapeDtypeStruct((N, GLA_WIDTH), F32),
                   jax.ShapeDtypeStruct((n1, 2, GLA_HEADS, GLA_DK, GLA_DV), F32)],
        scratch_shapes=[pltpu.VMEM((2, GLA_HEADS, GLA_DV, GLA_DK), F32)],
        compiler_params=_cp(("arbitrary",)),
        name="gla_bidir",
    )(p, p, p, p, p, p, p, p, wa, ba, tri, state)


def _gla_post_kernel(of_ref, ob_ref, r_ref, g_ref, o_ref):
    for h in range(GLA_HEADS):
        hv = slice(h * GLA_DV, (h + 1) * GLA_DV)
        o = of_ref[:, hv] + ob_ref[:, hv]
        y = o * lax.rsqrt(jnp.mean(o * o, axis=-1, keepdims=True) + EPS) * g_ref[...]
        r = r_ref[:, hv]
        o_ref[:, hv] = (y * (r * jax.nn.sigmoid(r))).astype(o_ref.dtype)


def _gla_post(of, ob, p, g, tm):
    N = of.shape[0]
    return pl.pallas_call(
        _gla_post_kernel, grid=(N // tm,),
        in_specs=[pl.BlockSpec((tm, GLA_WIDTH), lambda i: (i, 0)), pl.BlockSpec((tm, GLA_WIDTH), lambda i: (i, 0)),
                  pl.BlockSpec((tm, GLA_WIDTH), lambda i: (i, C_GR // GLA_WIDTH)),
                  pl.BlockSpec((1, GLA_DV), lambda i: (0, 0))],
        out_specs=pl.BlockSpec((tm, GLA_WIDTH), lambda i: (i, 0)),
        out_shape=jax.ShapeDtypeStruct((N, GLA_WIDTH), BF16),
        compiler_params=_cp(("parallel",)), name="gla_norm_gate",
    )(of, ob, p, g)


def _s5_matrices(a_re, a_im, log_dt, b_re, b_im, c_re, c_im):
    hp = lax.Precision.HIGHEST
    L, G, P, CH = S5_L, S5_GROUPS, S5_STATE, S5_CH
    taus = jnp.arange(L + 1, dtype=F32)
    intra = jnp.zeros((G, L, CH, L, CH), F32)
    ii = jnp.arange(L)
    dist = ii[None, :] - ii[:, None]
    state, carry, decay = [], [], []
    for d in range(2):
        dt = jnp.exp(log_dt[d])[:, None]
        ld_re, ld_im = a_re[d] * dt, a_im[d] * dt
        mag = jnp.exp(ld_re)
        lb_re, lb_im = mag * jnp.cos(ld_im), mag * jnp.sin(ld_im)
        den = a_re[d] * a_re[d] + a_im[d] * a_im[d]
        q_re = ((lb_re - 1.0) * a_re[d] + lb_im * a_im[d]) / den
        q_im = (lb_im * a_re[d] - (lb_re - 1.0) * a_im[d]) / den
        bb_re = q_re[..., None] * b_re - q_im[..., None] * b_im
        bb_im = q_re[..., None] * b_im + q_im[..., None] * b_re
        pmag = jnp.exp(ld_re[None] * taus[:, None, None])
        pw_re = pmag * jnp.cos(ld_im[None] * taus[:, None, None])
        pw_im = pmag * jnp.sin(ld_im[None] * taus[:, None, None])
        cp_re = c_re[None] * pw_re[:L, :, None, :] - c_im[None] * pw_im[:L, :, None, :]
        cp_im = c_re[None] * pw_im[:L, :, None, :] + c_im[None] * pw_re[:L, :, None, :]
        kern = (jnp.einsum('tgop,gpc->tgco', cp_re, bb_re, precision=hp)
                - jnp.einsum('tgop,gpc->tgco', cp_im, bb_im, precision=hp))
        dd = dist if d == 0 else -dist
        sel = kern[jnp.clip(dd, 0, L - 1)]
        sel = jnp.where((dd >= 0)[:, :, None, None, None], sel, 0.0)
        intra = intra + sel.transpose(2, 0, 3, 1, 4)
        e = (L - 1 - ii) if d == 0 else ii
        st_re = pw_re[e][:, :, :, None] * bb_re[None] - pw_im[e][:, :, :, None] * bb_im[None]
        st_im = pw_re[e][:, :, :, None] * bb_im[None] + pw_im[e][:, :, :, None] * bb_re[None]
        state.append(tuple(s.transpose(1, 0, 3, 2).reshape(G, L * CH, P) for s in (st_re, st_im)))
        e = (ii + 1) if d == 0 else (L - ii)
        cw_re = c_re[None] * pw_re[e][:, :, None, :] - c_im[None] * pw_im[e][:, :, None, :]
        cw_im = c_re[None] * pw_im[e][:, :, None, :] + c_im[None] * pw_re[e][:, :, None, :]
        carry.append(tuple(c.transpose(1, 3, 0, 2).reshape(G, P, L * CH) for c in (cw_re, -cw_im)))
        decay.append((pw_re[L], pw_im[L]))
    SG, GB = S5_SG, G // S5_SG
    st = jnp.stack([jnp.stack(s) for s in state])
    cw = jnp.stack([jnp.stack(c) for c in carry])
    t_c = intra.reshape(SG, GB, L, CH, L * CH).transpose(0, 2, 1, 3, 4).reshape(SG, S5_W, L * CH)
    s_c = st.reshape(2, 2, SG, GB, L, CH, P).transpose(2, 4, 3, 5, 0, 1, 6).reshape(SG, S5_W, 4 * P)
    r_c = cw.reshape(2, 2, SG, GB, P, L * CH).transpose(2, 0, 1, 3, 4, 5).reshape(SG, S5_W, L * CH)

    def expand(n_outer, n_inner):
        e = jnp.eye(n_outer, dtype=BF16)[:, None, :, None, None] * jnp.eye(n_inner, dtype=BF16)[None, :, None, None, :]
        return jnp.broadcast_to(e, (n_outer, n_inner, n_outer, GB, n_inner)).reshape(n_outer * n_inner, S5_W)

    col = jnp.arange(S5_W)
    grp_io = (col // CH) % GB
    grp_st = (col // P) % GB

    def dense(compact, e, grp_rows, grp_cols):
        w = jnp.einsum('skc,cn->skn', compact.astype(BF16), e, preferred_element_type=BF16)
        return jnp.where((grp_rows[:, None] == grp_cols[None, :])[None], w, jnp.zeros((), BF16))

    w_intra = dense(t_c, expand(L, CH), grp_io, grp_io)
    w_state = dense(s_c, expand(4, P), grp_io, grp_st)
    w_carry = dense(r_c, expand(L, CH), grp_st, grp_io)
    dec = jnp.stack([jnp.stack([x[0], x[1]]) for x in decay])
    dec = dec.reshape(4, SG, GB * P).transpose(1, 0, 2)
    return w_intra, w_state, w_carry, dec


def _s5_slab(u_ref, rt):
    return jnp.concatenate([u_ref[pl.ds(j, rt, stride=S5_L), :] for j in range(S5_L)], axis=1).astype(BF16)


def _s5_state_kernel(u_ref, w_ref, z_ref):
    z_ref[...] = jnp.dot(_s5_slab(u_ref, z_ref.shape[0]), w_ref[0], preferred_element_type=F32)


def _s5_out_kernel(u_ref, p_ref, wi_ref, wc_ref, y_ref):
    rt = p_ref.shape[0]
    y = (jnp.dot(_s5_slab(u_ref, rt), wi_ref[0], preferred_element_type=F32)
         + jnp.dot(p_ref[...].astype(BF16), wc_ref[0], preferred_element_type=F32))
    for i in range(S5_L):
        y_ref[pl.ds(i, rt, stride=S5_L), :] = y[:, i * LANE:(i + 1) * LANE]


def _s5_scan_kernel(z_ref, dec_ref, x0_ref, p_ref, fin_ref, *, segments):
    Q = S5_W // 4
    dec = [(dec_ref[0, 2 * d:2 * d + 1, :], dec_ref[0, 2 * d + 1:2 * d + 2, :]) for d in range(2)]
    seq_base = 0
    for (row0, n_seq, n_chunk) in segments:
        def seq_loop(sq, carry, row0=row0, n_chunk=n_chunk, seq_base=seq_base):
            r0 = row0 + sq * n_chunk
            sid = seq_base + sq
            x_init = tuple((x0_ref[0, pl.ds(4 * sid + 2 * d, 1), :], x0_ref[0, pl.ds(4 * sid + 2 * d + 1, 1), :])
                           for d in range(2))

            def step(n, xs):
                out = []
                for d in range(2):
                    xr, xi = xs[d]
                    ar, ai = dec[d]
                    r = r0 + (n if d == 0 else n_chunk - 1 - n)
                    cr = slice(2 * d * Q, (2 * d + 1) * Q)
                    ci = slice((2 * d + 1) * Q, (2 * d + 2) * Q)
                    zr = z_ref[pl.ds(r, 1), cr]
                    zi = z_ref[pl.ds(r, 1), ci]
                    p_ref[pl.ds(r, 1), cr] = xr
                    p_ref[pl.ds(r, 1), ci] = xi
                    out.append((ar * xr - ai * xi + zr, ar * xi + ai * xr + zi))
                return tuple(out)

            xs = lax.fori_loop(0, n_chunk, step, x_init)
            for d in range(2):
                fin_ref[0, pl.ds(4 * sid + 2 * d, 1), :] = xs[d][0]
                fin_ref[0, pl.ds(4 * sid + 2 * d + 1, 1), :] = xs[d][1]
            return carry

        lax.fori_loop(0, n_seq, seq_loop, 0)
        seq_base += n_seq


def _s5(p, weights, x0, segments):
    w_intra, w_state, w_carry, dec = weights
    N = p.shape[0]
    NC = N // S5_L
    SG = S5_SG
    rt = _pick(256, NC)
    nrow = x0.shape[1]
    cb = C_S5 // LANE
    u_spec = pl.BlockSpec((rt * S5_L, LANE), lambda s, r: (r, cb + s))
    w_spec = pl.BlockSpec((1, S5_W, S5_W), lambda s, r: (s, 0, 0))
    slab_spec = pl.BlockSpec((rt, S5_W), lambda s, r: (r, s))
    z = pl.pallas_call(
        _s5_state_kernel, grid=(SG, NC // rt), in_specs=[u_spec, w_spec], out_specs=slab_spec,
        out_shape=jax.ShapeDtypeStruct((NC, SG * S5_W), F32),
        compiler_params=_cp(("parallel", "parallel")), name="s5_chunk_state",
    )(p, w_state)
    state_spec = pl.BlockSpec((1, nrow, S5_W // 4), lambda s: (s, 0, 0))
    carry_in, fin = pl.pallas_call(
        functools.partial(_s5_scan_kernel, segments=segments), grid=(SG,),
        in_specs=[pl.BlockSpec((NC, S5_W), lambda s: (0, s)), pl.BlockSpec((1, 4, S5_W // 4), lambda s: (s, 0, 0)),
                  state_spec],
        out_specs=[pl.BlockSpec((NC, S5_W), lambda s: (0, s)), state_spec],
        out_shape=[jax.ShapeDtypeStruct((NC, SG * S5_W), F32), jax.ShapeDtypeStruct((SG, nrow, S5_W // 4), F32)],
        compiler_params=_cp(("parallel",)), name="s5_chunk_scan",
    )(z, dec, x0)
    y = pl.pallas_call(
        _s5_out_kernel, grid=(SG, NC // rt), in_specs=[u_spec, slab_spec, w_spec, w_spec],
        out_specs=pl.BlockSpec((rt * S5_L, LANE), lambda s, r: (r, s)),
        out_shape=jax.ShapeDtypeStruct((N, S5_WIDTH), F32),
        compiler_params=_cp(("parallel", "parallel"), 56), name="s5_chunk_out",
    )(p, carry_in, w_intra, w_carry)
    return y, fin


def _s5_post_kernel(y_ref, u_ref, d_ref, w_ref, b_ref, o_ref):
    yg = jax.nn.gelu(y_ref[...] + d_ref[...] * u_ref[...])
    z = jnp.dot(yg.astype(BF16), w_ref[...], preferred_element_type=F32) + b_ref[...]
    o_ref[...] = (yg * jax.nn.sigmoid(z)).astype(o_ref.dtype)


def _s5_post(y, p, d, w, b, tm):
    N = y.shape[0]
    W = S5_WIDTH
    return pl.pallas_call(
        _s5_post_kernel, grid=(N // tm,),
        in_specs=[pl.BlockSpec((tm, W), lambda i: (i, 0)), pl.BlockSpec((tm, W), lambda i: (i, C_S5 // W)),
                  pl.BlockSpec((1, W), lambda i: (0, 0)), pl.BlockSpec((W, W), lambda i: (0, 0)),
                  pl.BlockSpec((1, W), lambda i: (0, 0))],
        out_specs=pl.BlockSpec((tm, W), lambda i: (i, 0)),
        out_shape=jax.ShapeDtypeStruct((N, W), BF16),
        compiler_params=_cp(("parallel",)), name="s5_glu",
    )(y, p, d, w, b)


def _merge_kernel(h_ref, a_ref, b_ref, c_ref, wg0, wg1, wg2, bg0, bg1, bg2, wb_ref, o_ref):
    h = h_ref[...]
    acc = None
    for i, (br, wg, bg) in enumerate(((a_ref, wg0, bg0), (b_ref, wg1, bg1), (c_ref, wg2, bg2))):
        gate = jax.nn.sigmoid(jnp.dot(h, wg[...], preferred_element_type=F32) + bg[...])
        t = gate * jnp.dot(br[...], wb_ref[i], preferred_element_type=F32)
        acc = t if acc is None else acc + t
    o_ref[...] = acc.astype(o_ref.dtype)


def _merge(h, o_da, o_gla, o_s5, w_gate, b_gate, w_branch, tm, tn):
    N, D = h.shape
    W = o_da.shape[1]
    nt = D // tn
    act = pl.BlockSpec((tm, W), lambda j, i: (i, 0))
    in_specs = [pl.BlockSpec((tm, D), lambda j, i: (i, 0)), act, act, act]
    in_specs += [pl.BlockSpec((D, tn), lambda j, i, br=br: (0, br * nt + j)) for br in range(N_BRANCH)]
    in_specs += [pl.BlockSpec((1, tn), lambda j, i, br=br: (0, br * nt + j)) for br in range(N_BRANCH)]
    in_specs += [pl.BlockSpec((N_BRANCH, W, tn), lambda j, i: (0, 0, j))]
    return pl.pallas_call(
        _merge_kernel, grid=(nt, N // tm), in_specs=in_specs,
        out_specs=pl.BlockSpec((tm, tn), lambda j, i: (i, j)),
        out_shape=jax.ShapeDtypeStruct((N, D), BF16),
        compiler_params=_cp(("parallel", "parallel"), 56), name="gated_merge",
    )(h, o_da, o_gla, o_s5, w_gate, w_gate, w_gate, b_gate, b_gate, b_gate, w_branch)


def _wout_kernel(m_ref, w_ref, x_ref, g_ref, o_ref):
    o_ref[...] = x_ref[...] + g_ref[0] * jnp.dot(m_ref[...], w_ref[...], preferred_element_type=F32)


def _wout(merged, w, x, gate, midx, tm, tn):
    N, D = x.shape
    return pl.pallas_call(
        _wout_kernel, grid=(D // tn, N // tm),
        in_specs=[pl.BlockSpec((tm, D), lambda j, i: (i, 0)), pl.BlockSpec((D, tn), lambda j, i: (0, j)),
                  pl.BlockSpec((tm, tn), lambda j, i: (i, j)), pl.BlockSpec((1, 1, tn), lambda j, i: (midx(i), 0, j))],
        out_specs=pl.BlockSpec((tm, tn), lambda j, i: (i, j)),
        out_shape=jax.ShapeDtypeStruct((N, D), F32),
        compiler_params=_cp(("parallel", "parallel")), name="out_proj_residual",
    )(merged, w, x, gate)


def _route_kernel(lg_ref, bias_ref, asg_ref, tile_ref, rank_ref, *, tl):
    E, n_tok = lg_ref.shape
    EPG = EXPERTS_PER_GROUP
    ninf = -jnp.inf
    eio = lax.broadcasted_iota(jnp.int32, (E, tl), 0)
    upper = jnp.where(lax.broadcasted_iota(jnp.int32, (tl, tl), 0) < lax.broadcasted_iota(jnp.int32, (tl, tl), 1),
                      1.0, 0.0).astype(BF16)
    ecol = lax.broadcasted_iota(jnp.int32, (E, 1), 0)

    def select(j, carry):
        ls = pl.ds(pl.multiple_of(j * tl, tl), tl)
        s = jax.nn.sigmoid(lg_ref[:, ls])
        sel = s + bias_ref[...]
        rows = [sel[e:e + 1, :] for e in range(E)]
        srow = [s[e:e + 1, :] for e in range(E)]
        gi = jnp.zeros((1, tl), jnp.int32)
        best = None
        for g in range(N_EXPERT_GROUPS):
            a, b, c, d = rows[g * EPG:(g + 1) * EPG]
            hi1, lo1, hi2, lo2 = jnp.maximum(a, b), jnp.minimum(a, b), jnp.maximum(c, d), jnp.minimum(c, d)
            gs = jnp.maximum(hi1, hi2) + jnp.maximum(jnp.minimum(hi1, hi2), jnp.maximum(lo1, lo2))
            if best is None:
                best = gs
            else:
                upd = gs > best
                gi = jnp.where(upd, g, gi)
                best = jnp.where(upd, gs, best)

        def pick(rs, i):
            out = rs[(N_EXPERT_GROUPS - 1) * EPG + i]
            for g in range(N_EXPERT_GROUPS - 2, -1, -1):
                out = jnp.where(gi == g, rs[g * EPG + i], out)
            return out

        v = [pick(rows, i) for i in range(EPG)]
        sv = [pick(srow, i) for i in range(EPG)]
        i1, b1, s1 = jnp.zeros((1, tl), jnp.int32), v[0], sv[0]
        for i in range(1, EPG):
            upd = v[i] > b1
            i1, b1, s1 = jnp.where(upd, i, i1), jnp.where(upd, v[i], b1), jnp.where(upd, sv[i], s1)
        i2, b2, s2 = jnp.zeros((1, tl), jnp.int32), jnp.full((1, tl), ninf, F32), sv[0]
        for i in range(EPG):
            cand = jnp.where(i1 == i, ninf, v[i])
            upd = cand > b2
            i2, b2, s2 = jnp.where(upd, i, i2), jnp.where(upd, cand, b2), jnp.where(upd, sv[i], s2)
        e0 = gi * EPG + i1
        e1 = gi * EPG + i2
        den = s1 + s2
        asg_ref[2:3, ls] = s1 / den
        asg_ref[3:4, ls] = s2 / den
        asg_ref[4:5, ls] = e0.astype(F32)
        asg_ref[5:6, ls] = e1.astype(F32)
        m = jnp.where((eio == e0) | (eio == e1), 1.0, 0.0)
        rank_ref[:, ls] = jnp.dot(m.astype(BF16), upper, preferred_element_type=F32) + carry
        return carry + jnp.sum(m, axis=1, keepdims=True)

    counts = lax.fori_loop(0, n_tok // tl, select, jnp.zeros((E, 1), F32))
    ptiles = jnp.floor((counts + (MOE_TM - 1)) * (1.0 / MOE_TM))
    run = jnp.zeros((1, 1), F32)
    pstart_t = jnp.zeros((E, 1), F32)
    for e in range(E):
        pstart_t = jnp.where(ecol == e, run, pstart_t)
        run = run + ptiles[e:e + 1, :]
    pstart = pstart_t * MOE_TM

    def place(j, _):
        ls = pl.ds(pl.multiple_of(j * tl, tl), tl)
        dest = rank_ref[:, ls] + pstart
        e0 = asg_ref[4:5, ls].astype(jnp.int32)
        e1 = asg_ref[5:6, ls].astype(jnp.int32)
        asg_ref[0:1, ls] = jnp.sum(jnp.where(eio == e0, dest, 0.0), axis=0, keepdims=True)
        asg_ref[1:2, ls] = jnp.sum(jnp.where(eio == e1, dest, 0.0), axis=0, keepdims=True)
        asg_ref[6:8, ls] = jnp.zeros((2, tl), F32)
        return 0

    lax.fori_loop(0, n_tok // tl, place, 0)

    ntp = tile_ref.shape[1]
    pend_t = pstart_t + ptiles
    ti = lax.broadcasted_iota(jnp.int32, (1, ntp), 1).astype(F32)
    te = jnp.sum(jnp.where(pend_t <= ti, 1.0, 0.0), axis=0, keepdims=True)
    used = ti < run
    last_e = jnp.max(jnp.where(counts > 0.0, ecol.astype(F32), 0.0), axis=0, keepdims=True)
    te = jnp.where(used, jnp.minimum(te, E - 1.0), last_e)
    tile_ref[0:1, :] = te.astype(jnp.int32)
    tile_ref[1:2, :] = jnp.where(used, 1, 0).astype(jnp.int32)
    tile_ref[2:8, :] = jnp.zeros((6, ntp), jnp.int32)


def _route(logits_t, router_bias, ntile):
    E, N = logits_t.shape
    assert MOE_TM & (MOE_TM - 1) == 0
    ntp = -(-ntile // LANE) * LANE
    full = lambda shape: pl.BlockSpec(shape, lambda i: (0, 0))
    return pl.pallas_call(
        functools.partial(_route_kernel, tl=_pick(512, N)), grid=(1,),
        in_specs=[full((E, N)), full((E, 1))],
        out_specs=[full((8, N)), full((8, ntp))],
        out_shape=[jax.ShapeDtypeStruct((8, N), F32), jax.ShapeDtypeStruct((8, ntp), jnp.int32)],
        scratch_shapes=[pltpu.VMEM((E, N), F32)],
        compiler_params=_cp(("arbitrary",)), name="moe_route",
    )(logits_t, router_bias.astype(F32).reshape(E, 1))


def _moe_kernel(te_ref, tv_ref, x_ref, w1_ref, w3_ref, w2_ref, o_ref):
    valid = tv_ref[pl.program_id(0)] != 0

    @pl.when(valid)
    def _():
        x = x_ref[...]
        F = w1_ref.shape[2]
        acc = None
        for f0 in range(0, F, MOE_FC):
            fs = slice(f0, min(f0 + MOE_FC, F))
            a = jnp.dot(x, w1_ref[0, :, fs], preferred_element_type=F32)
            b = jnp.dot(x, w3_ref[0, :, fs], preferred_element_type=F32)
            hid = ((a * jax.nn.sigmoid(a)) * b).astype(BF16)
            t = jnp.dot(hid, w2_ref[0, fs, :], preferred_element_type=F32)
            acc = t if acc is None else acc + t
        o_ref[...] = acc.astype(o_ref.dtype)

    @pl.when(jnp.logical_not(valid))
    def _():
        o_ref[...] = jnp.zeros(o_ref.shape, o_ref.dtype)


def _moe(xin, w1, w3, w2, tile_e, tile_valid):
    cap, D = xin.shape
    F = w1.shape[2]
    TM = MOE_TM
    grid_spec = pltpu.PrefetchScalarGridSpec(
        num_scalar_prefetch=2, grid=(cap // TM,),
        in_specs=[pl.BlockSpec((TM, D), lambda i, te, tv: (i, 0)),
                  pl.BlockSpec((1, D, F), lambda i, te, tv: (te[i], 0, 0)),
                  pl.BlockSpec((1, D, F), lambda i, te, tv: (te[i], 0, 0)),
                  pl.BlockSpec((1, F, D), lambda i, te, tv: (te[i], 0, 0))],
        out_specs=pl.BlockSpec((TM, D), lambda i, te, tv: (i, 0)))
    return pl.pallas_call(
        _moe_kernel, grid_spec=grid_spec,
        out_shape=jax.ShapeDtypeStruct((cap, D), BF16),
        compiler_params=_cp(("arbitrary",), 58), name="moe_experts",
    )(tile_e, tile_valid, xin, w1, w3, w2)


def _combine_kernel(x_ref, y0_ref, y1_ref, w0_ref, w1_ref, g_ref, o_ref):
    o_ref[...] = x_ref[...] + g_ref[0] * (y0_ref[...] * w0_ref[...] + y1_ref[...] * w1_ref[...])


def _combine(x, y0, y1, w0, w1, gate, midx, tm):
    N, D = x.shape
    blk = pl.BlockSpec((tm, D), lambda i: (i, 0))
    col = pl.BlockSpec((tm, 1), lambda i: (i, 0))
    return pl.pallas_call(
        _combine_kernel, grid=(N // tm,),
        in_specs=[blk, blk, blk, col, col, pl.BlockSpec((1, 1, D), lambda i: (midx(i), 0, 0))],
        out_specs=blk, out_shape=jax.ShapeDtypeStruct((N, D), F32),
        compiler_params=_cp(("parallel",)), name="moe_residual",
    )(x, y0, y1, w0, w1, gate)


def _rope_tables(n_tokens):
    rows = n_tokens // GRID_W
    row = jnp.repeat(jnp.arange(rows, dtype=F32), GRID_W)
    col = jnp.tile(jnp.arange(GRID_W, dtype=F32), rows)
    inv = ROPE_BASE ** (-jnp.arange(0, AX_DIM, 2, dtype=F32) / AX_DIM)
    ar = row[:, None] * inv[None, :]
    ac = col[:, None] * inv[None, :]
    ang = jnp.tile(jnp.concatenate([ar, ar, ac, ac], axis=-1), (1, LANE // DA_HEAD))
    first = (jnp.arange(LANE) % (AX_DIM)) < (AX_DIM // 2)
    return jnp.cos(ang), jnp.where(first[None, :], -jnp.sin(ang), jnp.sin(ang))


def kernel(x_prompt, x_sample, cache_k, cache_v, state_gla, state_s5_re, state_s5_im, c, c_ctx, w_mod, b_mod, g_norm, w_in, da_lambda_q, da_lambda_k, da_subln_g, gla_w_alpha, gla_b_alpha, gla_norm_g, s5_a_re, s5_a_im, s5_log_dt, s5_b_re, s5_b_im, s5_c_re, s5_c_im, s5_d, s5_w_glu, s5_b_glu, w_branch, w_gate, b_gate, w_out, router_w, router_bias, moe_w1, moe_w3, moe_w2, g_final):
    B, T, D = x_prompt.shape
    DB, DT, _ = x_sample.shape
    L = w_in.shape[0]
    PAST = cache_k.shape[2]
    NP, NS = B * T, DB * DT
    N = NP + NS
    nseq = B + DB
    assert D == D_MODEL and 1 + DB <= 8 and NP % DT == 0 and T % GLA_ROWS == 0 and DT % GLA_ROWS == 0
    tm = _pick(512, NP, DT)
    midx = functools.partial(_mod_index, tm=tm, n_ctx_tok=NP, dec_seq=DT)

    x = jnp.concatenate([x_prompt.reshape(NP, D), x_sample.reshape(NS, D)], axis=0)
    cv = jnp.zeros((8, D), F32).at[0].set(c_ctx).at[1:1 + DB].set(c)
    mods = _mod(cv, w_mod, b_mod).reshape(L, 8, N_MOD, 1, D)
    zero_mod = jnp.zeros((8, 1, D), F32)

    rope_tabs = _rope_tables(DT)
    ck = cache_k.reshape(DB, L, PAST, DA_HEADS * 2 * DA_HEAD)
    cvv = cache_v.reshape(DB, L, PAST, DA_HEADS * DA_VDIM)
    r = jnp.arange(GLA_ROWS)
    same = (r[:, None] // GLA_CHUNK) == (r[None, :] // GLA_CHUNK)
    tri = jnp.stack([same & (r[None, :] <= r[:, None]), same & (r[None, :] >= r[:, None])]).astype(BF16)
    rw = router_w.astype(F32).T
    rw_hi = rw.astype(BF16)
    rw_lo = (rw - rw_hi.astype(F32)).astype(BF16)
    ntile = N * TOP_K // MOE_TM + N_EXPERTS
    tok2 = jnp.tile(jnp.arange(N, dtype=jnp.int32), TOP_K)
    tq_ctx = _pick(256, T)
    tq_lat = _pick(256, DT)
    s5_segments = ((0, B, T // S5_L), (NP // S5_L, DB, DT // S5_L))

    k_list, v_list, g_list, re_list, im_list = [], [], [], [], []
    for l in range(L):
        m = mods[l]
        shift1, scale1, gate1, shift2, scale2, gate2 = (m[:, i] for i in range(N_MOD))
        wi = w_in[l]
        w_in_p = jnp.concatenate([wi[:, :C_S5], wi[:, C_S5 + GLA_RANK:], wi[:, C_S5:C_S5 + GLA_RANK],
                                  jnp.zeros((D, P_COLS - C_GA - GLA_RANK), F32)], axis=1).astype(BF16)
        h = _norm(x, g_norm[l, 0][None], scale1, shift1, midx, tm, BF16)
        p = _matmul(h, w_in_p, tm, 1536, F32, "in_proj")

        lam_init = 0.8 - 0.6 * math.exp(-0.3 * l)
        lq, lk = da_lambda_q[l], da_lambda_k[l]
        lam = jnp.exp(jnp.sum(lq[0] * lk[0])) - jnp.exp(jnp.sum(lq[1] * lk[1])) + lam_init
        lam_row = jnp.full((1, LANE), lam, F32)
        g_row = (da_subln_g[l] * (1.0 - lam_init))[None]
        o_da = jnp.concatenate([
            _attention(p, lam_row, g_row, 0, B, T, tq_ctx, DA_HEADS, _pick(ATT_KB, T)),
            _attention_latent(p, lam_row, g_row, NP, DB, DT, tq_lat, rope_tabs, ck, cvv, l)], axis=0)

        wa = jnp.zeros((1, 2, LANE, GLA_KTOT), F32).at[0, :, :GLA_RANK].set(gla_w_alpha[l]).astype(BF16)
        ba = gla_b_alpha[l].reshape(1, 2, 1, GLA_KTOT)
        o_f, o_b, gla_fin = _gla(p, wa, ba, tri, state_gla, l, ((B, T), (DB, DT)))
        o_gla = _gla_post(o_f, o_b, p, gla_norm_g[l][None], tm)

        mats = _s5_matrices(s5_a_re[l], s5_a_im[l], s5_log_dt[l], s5_b_re[l], s5_b_im[l], s5_c_re[l], s5_c_im[l])
        x0 = jnp.stack([state_s5_re[:, l], state_s5_im[:, l]], axis=2)
        x0 = jnp.concatenate([jnp.zeros((B,) + x0.shape[1:], F32), x0], axis=0)
        x0 = x0.reshape(4 * nseq, S5_SG, S5_W // 4).transpose(1, 0, 2)
        y, s5_fin = _s5(p, mats, x0, s5_segments)
        o_s5 = _s5_post(y, p, s5_d[l][None], s5_w_glu[l].astype(BF16), s5_b_glu[l][None], tm)
        s5_fin = s5_fin.transpose(1, 0, 2).reshape(nseq, 2, 2, S5_GROUPS, S5_STATE)[:B]

        merged = _merge(h, o_da, o_gla, o_s5, w_gate[l].astype(BF16), b_gate[l][None], w_branch[l].astype(BF16),
                        tm, 512)
        x = _wout(merged, w_out[l].astype(BF16), x, gate1, midx, tm, 1024)

        h2, logits_t = _norm(x, g_norm[l, 1][None], scale2, shift2, midx, tm, BF16, router=(rw_hi, rw_lo))
        asg, tiles = _route(logits_t, router_bias, ntile)
        slot = asg[0:2].astype(jnp.int32)
        buf_tok = jnp.zeros((ntile * MOE_TM,), jnp.int32).at[slot.reshape(-1)].set(tok2, unique_indices=True)
        xin = jnp.take(h2, buf_tok, axis=0)
        ye = _moe(xin, moe_w1[l].astype(BF16), moe_w3[l].astype(BF16), moe_w2[l].astype(BF16), tiles[0, :ntile],
                  tiles[1, :ntile])
        x = _combine(x, jnp.take(ye, slot[0], axis=0), jnp.take(ye, slot[1], axis=0), asg[2].reshape(N, 1),
                     asg[3].reshape(N, 1), gate2, midx, tm)

        k_list.append(p[:NP, C_K:C_K + 1024].reshape(B, T, DA_HEADS, 2 * DA_HEAD))
        v_list.append(p[:NP, C_V:C_V + 1024].reshape(B, T, DA_HEADS, DA_VDIM))
        g_list.append(gla_fin[:B])
        re_list.append(s5_fin[:, :, 0])
        im_list.append(s5_fin[:, :, 1])

    y_all = _norm(x, g_final[None], zero_mod, zero_mod, midx, tm, F32)
    return (y_all[:NP].reshape(B, T, D), y_all[NP:].reshape(DB, DT, D),
            jnp.stack(k_list, axis=1), jnp.stack(v_list, axis=1), jnp.stack(g_list, axis=1),
            jnp.stack(re_list, axis=1), jnp.stack(im_list, axis=1))
```

```python
import functools
import math

import jax
import jax.numpy as jnp
from jax import lax
from jax.experimental import pallas as pl
from jax.experimental.pallas import tpu as pltpu

F32 = jnp.float32
BF16 = jnp.bfloat16

D_MODEL = 2048
GRID_W = 64
EPS = 1e-6
N_MOD = 6
N_BRANCH = 3
DA_HEADS = 8
DA_HEAD = 64
DA_VDIM = 128
ROPE_BASE = 10000.0
AX_DIM = DA_HEAD // 2
GLA_HEADS = 4
GLA_DK = 128
GLA_DV = 256
GLA_KTOT = GLA_HEADS * GLA_DK
GLA_WIDTH = GLA_HEADS * GLA_DV
GLA_RANK = 16
GLA_TAU = 16.0
GLA_CHUNK = 64
S5_WIDTH = 1024
S5_CH = 16
S5_GROUPS = S5_WIDTH // S5_CH
S5_STATE = 64
N_EXPERTS = 16
N_EXPERT_GROUPS = 4
EXPERTS_PER_GROUP = N_EXPERTS // N_EXPERT_GROUPS
TOPK_GROUP = 1
TOP_K = 2
D_FF_EXPERT = 1408

LANE = 128
C_Q, C_K, C_V, C_GQ, C_GK, C_GV, C_GR, C_S5, C_GA = 0, 1024, 2048, 3072, 3584, 4096, 5120, 6144, 7168
P_COLS = 7680
GLA_ROWS = 256
S5_L = 16
S5_SG = S5_WIDTH // LANE
S5_GB = S5_GROUPS // S5_SG
S5_W = S5_L * LANE
S5_SW = 4 * S5_GB * S5_STATE
MOE_TM = 512
MOE_FC = 512
ATT_KB = 512
VMEM_MB = 1024 * 1024


def _cp(sem, vmem_mb=48):
    return pltpu.CompilerParams(dimension_semantics=sem, vmem_limit_bytes=vmem_mb * VMEM_MB)


def _pick(pref, *ns):
    t = pref
    while any(n % t for n in ns):
        t //= 2
    return t


def _mod_index(i, tm, n_ctx_tok, dec_seq):
    t0 = i * tm
    return jnp.where(t0 < n_ctx_tok, 0, 1 + (t0 - n_ctx_tok) // dec_seq)


def _mod_kernel(c_ref, w_ref, b_ref, o_ref):
    c = c_ref[...]
    s = (c * jax.nn.sigmoid(c)).astype(BF16)
    o_ref[0] = jnp.dot(s, w_ref[0].astype(BF16), preferred_element_type=F32) + b_ref[0]


def _mod(cv, w_mod, b_mod):
    L, D, M = w_mod.shape
    tn = 1024
    return pl.pallas_call(
        _mod_kernel,
        grid=(L, M // tn),
        in_specs=[pl.BlockSpec((8, D), lambda l, j: (0, 0)),
                  pl.BlockSpec((1, D, tn), lambda l, j: (l, 0, j)),
                  pl.BlockSpec((1, 1, tn), lambda l, j: (l, 0, j))],
        out_specs=pl.BlockSpec((1, 8, tn), lambda l, j: (l, 0, j)),
        out_shape=jax.ShapeDtypeStruct((L, 8, M), F32),
        compiler_params=_cp(("parallel", "parallel")),
        name="mod_vectors",
    )(cv, w_mod, b_mod.reshape(L, 1, M))


def _norm_core(x_ref, g_ref, sc_ref, sh_ref):
    x = x_ref[...]
    y = x * lax.rsqrt(jnp.mean(x * x, axis=-1, keepdims=True) + EPS)
    return (y * g_ref[...]) * (1.0 + sc_ref[0]) + sh_ref[0]


def _norm_kernel(x_ref, g_ref, sc_ref, sh_ref, o_ref):
    o_ref[...] = _norm_core(x_ref, g_ref, sc_ref, sh_ref).astype(o_ref.dtype)


def _norm_router_kernel(x_ref, g_ref, sc_ref, sh_ref, rh_ref, rl_ref, o_ref, lg_ref):
    h = _norm_core(x_ref, g_ref, sc_ref, sh_ref)
    hh = h.astype(BF16)
    hl = (h - hh.astype(F32)).astype(BF16)
    o_ref[...] = hh
    nt = (((1,), (1,)), ((), ()))
    lg_ref[...] = (lax.dot_general(rh_ref[...], hh, nt, preferred_element_type=F32)
                   + lax.dot_general(rh_ref[...], hl, nt, preferred_element_type=F32)
                   + lax.dot_general(rl_ref[...], hh, nt, preferred_element_type=F32))


def _norm(x, g, sc, sh, midx, tm, out_dtype, router=None):
    N, D = x.shape
    in_specs = [pl.BlockSpec((tm, D), lambda i: (i, 0)),
                pl.BlockSpec((1, D), lambda i: (0, 0)),
                pl.BlockSpec((1, 1, D), lambda i: (midx(i), 0, 0)),
                pl.BlockSpec((1, 1, D), lambda i: (midx(i), 0, 0))]
    if router is None:
        return pl.pallas_call(
            _norm_kernel, grid=(N // tm,), in_specs=in_specs,
            out_specs=pl.BlockSpec((tm, D), lambda i: (i, 0)),
            out_shape=jax.ShapeDtypeStruct((N, D), out_dtype),
            compiler_params=_cp(("parallel",)), name="norm_mod",
        )(x, g, sc, sh)
    rh, rl = router
    E = rh.shape[0]
    in_specs += [pl.BlockSpec((E, D), lambda i: (0, 0)), pl.BlockSpec((E, D), lambda i: (0, 0))]
    return pl.pallas_call(
        _norm_router_kernel, grid=(N // tm,), in_specs=in_specs,
        out_specs=[pl.BlockSpec((tm, D), lambda i: (i, 0)), pl.BlockSpec((E, tm), lambda i: (0, i))],
        out_shape=[jax.ShapeDtypeStruct((N, D), BF16), jax.ShapeDtypeStruct((E, N), F32)],
        compiler_params=_cp(("parallel",)), name="norm_mod_router",
    )(x, g, sc, sh, rh, rl)


def _mm_kernel(a_ref, w_ref, o_ref):
    o_ref[...] = jnp.dot(a_ref[...], w_ref[...], preferred_element_type=F32).astype(o_ref.dtype)


def _matmul(a, w, tm, tn, out_dtype, name):
    M, K = a.shape
    Nc = w.shape[1]
    return pl.pallas_call(
        _mm_kernel, grid=(Nc // tn, M // tm),
        in_specs=[pl.BlockSpec((tm, K), lambda j, i: (i, 0)), pl.BlockSpec((K, tn), lambda j, i: (0, j))],
        out_specs=pl.BlockSpec((tm, tn), lambda j, i: (i, j)),
        out_shape=jax.ShapeDtypeStruct((M, Nc), out_dtype),
        compiler_params=_cp(("parallel", "parallel")), name=name,
    )(a, w)


def _rope(x, cos, sin_signed):
    lane = lax.broadcasted_iota(jnp.int32, (1, LANE), 1)
    first = (lane % (2 * (AX_DIM // 2))) < (AX_DIM // 2)
    xr = jnp.where(first, pltpu.roll(x, LANE - AX_DIM // 2, 1), pltpu.roll(x, AX_DIM // 2, 1))
    return x * cos + xr * sin_signed


def _attn_kernel(*refs, rope, ctx, hs, kb):
    refs = list(refs)
    q_ref, k_ref, v_ref = refs[:3]
    refs = refs[3:]
    if ctx:
        ck_ref, cv_ref = refs[:2]
        refs = refs[2:]
    if rope:
        cq_ref, sq_ref, ckk_ref, skk_ref = refs[:4]
        refs = refs[4:]
    lam_ref, g_ref, o_ref, kb_ref, vb_ref = refs[:5]
    if ctx:
        ckb_ref, cvb_ref = refs[5:]

    hsl = [slice(h * LANE, (h + 1) * LANE) for h in range(hs)]

    @pl.when(pl.program_id(2) == 0)
    def _():
        for h in range(hs):
            k = k_ref[:, hsl[h]]
            if rope:
                k = _rope(k, ckk_ref[...], skk_ref[...])
            kb_ref[:, hsl[h]] = k.astype(BF16)
        vb_ref[...] = v_ref[...].astype(BF16)
        if ctx:
            ckb_ref[...] = ck_ref[...].astype(BF16)
            cvb_ref[...] = cv_ref[...].astype(BF16)

    tq = q_ref.shape[0]
    n_new = kb_ref.shape[0]
    lane = lax.broadcasted_iota(jnp.int32, (1, LANE), 1)
    lo = lane < DA_HEAD
    nt = (((1,), (1,)), ((), ()))
    qqs = []
    for h in range(hs):
        q = q_ref[:, hsl[h]]
        if rope:
            q = _rope(q, cq_ref[...], sq_ref[...])
        q = q * (DA_HEAD ** -0.5 * math.log2(math.e))
        qqs.append(jnp.concatenate([jnp.where(lo, q, 0.0), jnp.where(lo, 0.0, q)], axis=0).astype(BF16))

    def block(qq, kblk, vblk, carry):
        m, l, acc = carry
        s = lax.dot_general(qq, kblk, nt, preferred_element_type=F32)
        mn = jnp.maximum(m, jnp.max(s, axis=-1, keepdims=True))
        e = jnp.exp2(s - mn)
        r = jnp.exp2(m - mn)
        l = r * l + jnp.sum(e, axis=-1, keepdims=True)
        acc = r * acc + jnp.dot(e.astype(BF16), vblk, preferred_element_type=F32)
        return mn, l, acc

    def all_heads(kref, vref, rows, carries):
        return tuple(block(qqs[h], kref[rows, hsl[h]], vref[rows, hsl[h]], carries[h]) for h in range(hs))

    carries = tuple((jnp.full((2 * tq, 1), -jnp.inf, F32), jnp.zeros((2 * tq, 1), F32),
                     jnp.zeros((2 * tq, LANE), F32)) for _ in range(hs))
    if ctx:
        for c in range(ckb_ref.shape[0] // kb):
            carries = all_heads(ckb_ref, cvb_ref, slice(c * kb, (c + 1) * kb), carries)
    if n_new // kb == 1:
        carries = all_heads(kb_ref, vb_ref, slice(None), carries)
    else:
        def body(j, carries):
            return all_heads(kb_ref, vb_ref, pl.ds(pl.multiple_of(j * kb, kb), kb), carries)
        carries = lax.fori_loop(0, n_new // kb, body, carries)
    for h in range(hs):
        _, l, acc = carries[h]
        o = acc[:tq] * (1.0 / l[:tq]) - acc[tq:] * (lam_ref[:, :1] / l[tq:])
        y = o * lax.rsqrt(jnp.mean(o * o, axis=-1, keepdims=True) + EPS)
        o_ref[:, hsl[h]] = (y * g_ref[...]).astype(o_ref.dtype)


def _attention(p, lam_row, g_row, row0, B, T, tq, hs, kb, rope_tabs=None, ctx=None):
    nq = T // tq
    rb0 = row0 // tq
    kb0 = row0 // T
    W = hs * LANE
    hq, hk, hv = C_Q // W, C_K // W, C_V // W
    assert T % kb == 0
    in_specs = [pl.BlockSpec((tq, W), lambda b, h, i: (rb0 + b * nq + i, hq + h)),
                pl.BlockSpec((T, W), lambda b, h, i: (kb0 + b, hk + h)),
                pl.BlockSpec((T, W), lambda b, h, i: (kb0 + b, hv + h))]
    args = [p, p, p]
    scratch = [pltpu.VMEM((T, W), BF16), pltpu.VMEM((T, W), BF16)]
    if ctx is not None:
        ck, cv, layer = ctx
        past = ck.shape[2]
        assert past % kb == 0
        in_specs += [pl.BlockSpec((None, None, past, W), lambda b, h, i: (b, layer, 0, h)),
                     pl.BlockSpec((None, None, past, W), lambda b, h, i: (b, layer, 0, h))]
        args += [ck, cv]
        scratch += [pltpu.VMEM((past, W), BF16), pltpu.VMEM((past, W), BF16)]
    if rope_tabs is not None:
        cos, sin = rope_tabs
        in_specs += [pl.BlockSpec((tq, LANE), lambda b, h, i: (i, 0)), pl.BlockSpec((tq, LANE), lambda b, h, i: (i, 0)),
                     pl.BlockSpec((T, LANE), lambda b, h, i: (0, 0)), pl.BlockSpec((T, LANE), lambda b, h, i: (0, 0))]
        args += [cos, sin, cos, sin]
    in_specs += [pl.BlockSpec((1, LANE), lambda b, h, i: (0, 0)), pl.BlockSpec((1, LANE), lambda b, h, i: (0, 0))]
    args += [lam_row, g_row]
    return pl.pallas_call(
        functools.partial(_attn_kernel, rope=rope_tabs is not None, ctx=ctx is not None, hs=hs, kb=kb),
        grid=(B, DA_HEADS // hs, nq),
        in_specs=in_specs,
        out_specs=pl.BlockSpec((tq, W), lambda b, h, i: (b * nq + i, h)),
        out_shape=jax.ShapeDtypeStruct((B * T, DA_HEADS * DA_VDIM), BF16),
        scratch_shapes=scratch,
        compiler_params=_cp(("parallel", "parallel", "arbitrary"), 56),
        name="diff_attention_latent" if ctx is not None else "diff_attention_context",
    )(*args)


def _attn_latent_kernel(q_ref, k_ref, v_ref, ck_ref, cv_ref, cq_ref, sq_ref, ckk_ref, skk_ref, lam_ref, g_ref, o_ref,
                        kb_ref, vb_ref):
    @pl.when(pl.program_id(2) == 0)
    def _():
        kb_ref[...] = _rope(k_ref[...], ckk_ref[...], skk_ref[...]).astype(BF16)
        vb_ref[...] = v_ref[...].astype(BF16)

    q = _rope(q_ref[...], cq_ref[...], sq_ref[...]) * (DA_HEAD ** -0.5)
    lane = lax.broadcasted_iota(jnp.int32, (1, LANE), 1)
    lo = lane < DA_HEAD
    q0 = jnp.where(lo, q, 0.0).astype(BF16)
    q1 = jnp.where(lo, 0.0, q).astype(BF16)
    nt = (((1,), (1,)), ((), ()))
    kb = kb_ref[...]
    ckb = ck_ref[...].astype(BF16)
    s0 = lax.dot_general(q0, kb, nt, preferred_element_type=F32)
    s1 = lax.dot_general(q1, kb, nt, preferred_element_type=F32)
    c0 = lax.dot_general(q0, ckb, nt, preferred_element_type=F32)
    c1 = lax.dot_general(q1, ckb, nt, preferred_element_type=F32)
    m0 = jnp.maximum(jnp.max(s0, axis=-1, keepdims=True), jnp.max(c0, axis=-1, keepdims=True))
    m1 = jnp.maximum(jnp.max(s1, axis=-1, keepdims=True), jnp.max(c1, axis=-1, keepdims=True))
    e0 = jnp.exp(s0 - m0)
    e1 = jnp.exp(s1 - m1)
    ec0 = jnp.exp(c0 - m0)
    ec1 = jnp.exp(c1 - m1)
    l0 = jnp.sum(e0, axis=-1, keepdims=True) + jnp.sum(ec0, axis=-1, keepdims=True)
    l1 = jnp.sum(e1, axis=-1, keepdims=True) + jnp.sum(ec1, axis=-1, keepdims=True)
    a0 = 1.0 / l0
    a1 = lam_ref[:, :1] / l1
    o = jnp.dot((e0 * a0 - e1 * a1).astype(BF16), vb_ref[...], preferred_element_type=F32)
    o = o + jnp.dot((ec0 * a0 - ec1 * a1).astype(BF16), cv_ref[...].astype(BF16), preferred_element_type=F32)
    y = o * lax.rsqrt(jnp.mean(o * o, axis=-1, keepdims=True) + EPS)
    o_ref[...] = (y * g_ref[...]).astype(o_ref.dtype)


def _attention_latent(p, lam_row, g_row, row0, B, T, tq, rope_tabs, ck, cv, layer):
    nq = T // tq
    rb0 = row0 // tq
    kb0 = row0 // T
    hq, hk, hv = C_Q // LANE, C_K // LANE, C_V // LANE
    past = ck.shape[2]
    cos, sin = rope_tabs
    in_specs = [pl.BlockSpec((tq, LANE), lambda b, h, i: (rb0 + b * nq + i, hq + h)),
                pl.BlockSpec((T, LANE), lambda b, h, i: (kb0 + b, hk + h)),
                pl.BlockSpec((T, LANE), lambda b, h, i: (kb0 + b, hv + h)),
                pl.BlockSpec((None, None, past, LANE), lambda b, h, i: (b, layer, 0, h)),
                pl.BlockSpec((None, None, past, LANE), lambda b, h, i: (b, layer, 0, h)),
                pl.BlockSpec((tq, LANE), lambda b, h, i: (i, 0)), pl.BlockSpec((tq, LANE), lambda b, h, i: (i, 0)),
                pl.BlockSpec((T, LANE), lambda b, h, i: (0, 0)), pl.BlockSpec((T, LANE), lambda b, h, i: (0, 0)),
                pl.BlockSpec((1, LANE), lambda b, h, i: (0, 0)), pl.BlockSpec((1, LANE), lambda b, h, i: (0, 0))]
    return pl.pallas_call(
        _attn_latent_kernel, grid=(B, DA_HEADS, nq), in_specs=in_specs,
        out_specs=pl.BlockSpec((tq, LANE), lambda b, h, i: (b * nq + i, h)),
        out_shape=jax.ShapeDtypeStruct((B * T, DA_HEADS * DA_VDIM), BF16),
        scratch_shapes=[pltpu.VMEM((T, LANE), BF16), pltpu.VMEM((T, LANE), BF16)],
        compiler_params=_cp(("parallel", "parallel", "arbitrary"), 56),
        name="diff_attention_latent",
    )(p, p, p, ck, cv, cos, sin, cos, sin, lam_row, g_row)


def _log_sigmoid(z):
    return jnp.minimum(z, 0.0) - jnp.log1p(jnp.exp(-jnp.abs(z)))


def _gla_direction(d, q_ref, k_ref, v_ref, ga_ref, wa_ref, ba_ref, tri_ref, o_ref, st_ref):
    C = GLA_CHUNK
    nsub = GLA_ROWS // C
    z = jnp.dot(ga_ref[...].astype(BF16), wa_ref[0, d], preferred_element_type=F32) + ba_ref[0, d]
    la = _log_sigmoid(z) / GLA_TAU
    hi = la.astype(BF16)
    r1 = la - hi.astype(F32)
    mid = r1.astype(BF16)
    low = (r1 - mid.astype(F32)).astype(BF16)
    tri = tri_ref[d]
    bsum = (jnp.dot(tri, hi, preferred_element_type=F32) + jnp.dot(tri, mid, preferred_element_type=F32)
            + jnp.dot(tri, low, preferred_element_type=F32))
    row = lax.broadcasted_iota(jnp.int32, (C, C), 0)
    col = lax.broadcasted_iota(jnp.int32, (C, C), 1)
    mask = (row >= col) if d == 0 else (row <= col)
    nt = (((1,), (1,)), ((), ()))
    tn = (((0,), (0,)), ((), ()))
    subs = range(nsub) if d == 0 else range(nsub - 1, -1, -1)
    for h in range(GLA_HEADS):
        hk = slice(h * GLA_DK, (h + 1) * GLA_DK)
        hv = slice(h * GLA_DV, (h + 1) * GLA_DV)
        st = st_ref[d, h]
        for s in subs:
            rs = slice(s * C, (s + 1) * C)
            end = s * C + C - 1 if d == 0 else s * C
            bb = bsum[rs, hk]
            bl = bsum[end:end + 1, hk]
            q = q_ref[rs, hk] * (GLA_DK ** -0.5)
            k = k_ref[rs, hk]
            v = v_ref[rs, hv].astype(BF16)
            qt = (q * jnp.exp(bb)).astype(BF16)
            kt = (k * jnp.exp(-bb)).astype(BF16)
            kend = (k * jnp.exp(bl - bb)).astype(BF16)
            att = jnp.where(mask, lax.dot_general(qt, kt, nt, preferred_element_type=F32), 0.0).astype(BF16)
            o = jnp.dot(att, v, preferred_element_type=F32)
            o = o + lax.dot_general(qt, st.astype(BF16), nt, preferred_element_type=F32)
            o_ref[rs, hv] = o
            st = st * jnp.exp(bl) + lax.dot_general(v, kend, tn, preferred_element_type=F32)
        st_ref[d, h] = st


def _gla_kernel(qf, kf, vf, gaf, qb, kb, vb, gab, wa_ref, ba_ref, tri_ref, s0_ref, of_ref, ob_ref, fin_ref, st_ref,
                *, seq_pos):
    first, last, stateless = seq_pos(pl.program_id(0))

    @pl.when(first & stateless)
    def _():
        st_ref[...] = jnp.zeros(st_ref.shape, F32)

    @pl.when(first & jnp.logical_not(stateless))
    def _():
        for d in range(2):
            for h in range(GLA_HEADS):
                st_ref[d, h] = s0_ref[d, h].T

    _gla_direction(0, qf, kf, vf, gaf, wa_ref, ba_ref, tri_ref, of_ref, st_ref)
    _gla_direction(1, qb, kb, vb, gab, wa_ref, ba_ref, tri_ref, ob_ref, st_ref)

    @pl.when(last & stateless)
    def _():
        for d in range(2):
            for h in range(GLA_HEADS):
                fin_ref[d, h] = st_ref[d, h].T


def _gla(p, wa, ba, tri, state, layer, seqs):
    N = p.shape[0]
    R = GLA_ROWS
    (n1, t1), (n2, t2) = seqs
    nb1, nb2 = t1 // R, t2 // R
    steps1 = n1 * nb1

    def locate(s):
        in1 = s < steps1
        s2 = s - steps1
        nb = jnp.where(in1, nb1, nb2)
        seq = jnp.where(in1, s // nb1, n1 + s2 // nb2)
        n = jnp.where(in1, s % nb1, s2 % nb2)
        base = jnp.where(in1, (s // nb1) * nb1, steps1 + (s2 // nb2) * nb2)
        return seq, n, nb, base

    def fwd_blk(s):
        _, n, _, base = locate(s)
        return base + n

    def bwd_blk(s):
        _, n, nb, base = locate(s)
        return base + nb - 1 - n

    def seq_pos(s):
        _, n, nb, _ = locate(s)
        return n == 0, n == nb - 1, s < steps1

    def specs(blk):
        return [pl.BlockSpec((R, GLA_KTOT), lambda s: (blk(s), C_GQ // GLA_KTOT)),
                pl.BlockSpec((R, GLA_KTOT), lambda s: (blk(s), C_GK // GLA_KTOT)),
                pl.BlockSpec((R, GLA_WIDTH), lambda s: (blk(s), C_GV // GLA_WIDTH)),
                pl.BlockSpec((R, LANE), lambda s: (blk(s), C_GA // LANE))]

    s0_spec = pl.BlockSpec((None, None, 2, GLA_HEADS, GLA_DK, GLA_DV),
                           lambda s: (jnp.maximum(locate(s)[0] - n1, 0), layer, 0, 0, 0, 0))
    fin_spec = pl.BlockSpec((None, 2, GLA_HEADS, GLA_DK, GLA_DV),
                            lambda s: (jnp.minimum(locate(s)[0], n1 - 1), 0, 0, 0, 0))
    return pl.pallas_call(
        functools.partial(_gla_kernel, seq_pos=seq_pos),
        grid=(N // R,),
        in_specs=specs(fwd_blk) + specs(bwd_blk) + [
            pl.BlockSpec((1, 2, LANE, GLA_KTOT), lambda s: (0, 0, 0, 0)),
            pl.BlockSpec((1, 2, 1, GLA_KTOT), lambda s: (0, 0, 0, 0)),
            pl.BlockSpec((2, R, R), lambda s: (0, 0, 0)),
            s0_spec],
        out_specs=[pl.BlockSpec((R, GLA_WIDTH), lambda s: (fwd_blk(s), 0)),
                   pl.BlockSpec((R, GLA_WIDTH), lambda s: (bwd_blk(s), 0)),
                   fin_spec],
        out_shape=[jax.ShapeDtypeStruct((N, GLA_WIDTH), F32), jax.ShapeDtypeStruct((N, GLA_WIDTH), F32),
                   jax.ShapeDtypeStruct((n1, 2, GLA_HEADS, GLA_DK, GLA_DV), F32)],
        scratch_shapes=[pltpu.VMEM((2, GLA_HEADS, GLA_DV, GLA_DK), F32)],
        compiler_params=_cp(("arbitrary",)),
        name="gla_bidir",
    )(p, p, p, p, p, p, p, p, wa, ba, tri, state)


def _gla_post_kernel(of_ref, ob_ref, r_ref, g_ref, o_ref):
    for h in range(GLA_HEADS):
        hv = slice(h * GLA_DV, (h + 1) * GLA_DV)
        o = of_ref[:, hv] + ob_ref[:, hv]
        y = o * lax.rsqrt(jnp.mean(o * o, axis=-1, keepdims=True) + EPS) * g_ref[...]
        r = r_ref[:, hv]
        o_ref[:, hv] = (y * (r * jax.nn.sigmoid(r))).astype(o_ref.dtype)


def _gla_post(of, ob, p, g, tm):
    N = of.shape[0]
    return pl.pallas_call(
        _gla_post_kernel, grid=(N // tm,),
        in_specs=[pl.BlockSpec((tm, GLA_WIDTH), lambda i: (i, 0)), pl.BlockSpec((tm, GLA_WIDTH), lambda i: (i, 0)),
                  pl.BlockSpec((tm, GLA_WIDTH), lambda i: (i, C_GR // GLA_WIDTH)),
                  pl.BlockSpec((1, GLA_DV), lambda i: (0, 0))],
        out_specs=pl.BlockSpec((tm, GLA_WIDTH), lambda i: (i, 0)),
        out_shape=jax.ShapeDtypeStruct((N, GLA_WIDTH), BF16),
        compiler_params=_cp(("parallel",)), name="gla_norm_gate",
    )(of, ob, p, g)


def _s5_matrices(a_re, a_im, log_dt, b_re, b_im, c_re, c_im):
    hp = lax.Precision.HIGHEST
    L, G, P, CH = S5_L, S5_GROUPS, S5_STATE, S5_CH
    taus = jnp.arange(L + 1, dtype=F32)
    intra = jnp.zeros((L, L, CH, G * CH), F32)
    ii = jnp.arange(L)
    dist = ii[:, None] - ii[None, :]
    state, carry, decay = [], [], []
    c_re_t, c_im_t = c_re.transpose(2, 0, 1), c_im.transpose(2, 0, 1)
    for d in range(2):
        dt = jnp.exp(log_dt[d])[:, None]
        ld_re, ld_im = a_re[d] * dt, a_im[d] * dt
        mag = jnp.exp(ld_re)
        lb_re, lb_im = mag * jnp.cos(ld_im), mag * jnp.sin(ld_im)
        den = a_re[d] * a_re[d] + a_im[d] * a_im[d]
        q_re = ((lb_re - 1.0) * a_re[d] + lb_im * a_im[d]) / den
        q_im = (lb_im * a_re[d] - (lb_re - 1.0) * a_im[d]) / den
        bb_re = q_re[..., None] * b_re - q_im[..., None] * b_im
        bb_im = q_re[..., None] * b_im + q_im[..., None] * b_re
        pmag = jnp.exp(ld_re[None] * taus[:, None, None])
        pw_re = pmag * jnp.cos(ld_im[None] * taus[:, None, None])
        pw_im = pmag * jnp.sin(ld_im[None] * taus[:, None, None])
        cp_re = c_re[None] * pw_re[:L, :, None, :] - c_im[None] * pw_im[:L, :, None, :]
        cp_im = c_re[None] * pw_im[:L, :, None, :] + c_im[None] * pw_re[:L, :, None, :]
        kern = (jnp.einsum('tgop,gpc->tcgo', cp_re, bb_re, precision=hp)
                - jnp.einsum('tgop,gpc->tcgo', cp_im, bb_im, precision=hp)).reshape(L, CH, G * CH)
        dd = dist if d == 0 else -dist
        sel = kern[jnp.clip(dd, 0, L - 1)]
        intra = intra + jnp.where((dd >= 0)[:, :, None, None], sel, 0.0)
        e = (L - 1 - ii) if d == 0 else ii
        bt_re, bt_im = bb_re.transpose(2, 0, 1)[None], bb_im.transpose(2, 0, 1)[None]
        st_re = pw_re[e][:, None] * bt_re - pw_im[e][:, None] * bt_im
        st_im = pw_re[e][:, None] * bt_im + pw_im[e][:, None] * bt_re
        state += [st_re.reshape(L * CH, G * P), st_im.reshape(L * CH, G * P)]
        e = (ii + 1) if d == 0 else (L - ii)
        pt_re, pt_im = pw_re[e].transpose(0, 2, 1)[..., None], pw_im[e].transpose(0, 2, 1)[..., None]
        cw_re = c_re_t[None] * pt_re - c_im_t[None] * pt_im
        cw_im = c_re_t[None] * pt_im + c_im_t[None] * pt_re
        carry += [cw_re.reshape(L, P, G * CH), (-cw_im).reshape(L, P, G * CH)]
        decay.append((pw_re[L], pw_im[L]))
    t2 = intra.reshape(L, L * CH, G * CH).astype(BF16)
    st = jnp.stack(state).astype(BF16)
    cw = jnp.stack(carry).astype(BF16)
    dec = jnp.stack([jnp.stack([x[0], x[1]]) for x in decay])
    dec = dec.reshape(4, S5_SG, S5_SW // 4).transpose(1, 0, 2)
    return t2, st, cw, dec


def _s5_slab(u_ref, rt):
    return jnp.concatenate([u_ref[pl.ds(j, rt, stride=S5_L), :] for j in range(S5_L)], axis=1).astype(BF16)


def _s5_group_masks(width, per_group):
    grp = (lax.broadcasted_iota(jnp.int32, (1, width), 1) // per_group) % S5_GB
    return [grp == a for a in range(S5_GB)]


def _s5_state_kernel(u_ref, st_ref, z_ref, w_ref):
    @pl.when(pl.program_id(1) == 0)
    def _():
        Q = S5_SW // 4
        masks = _s5_group_masks(Q, S5_STATE)
        for q in range(4):
            for j in range(S5_L):
                blk = st_ref[q, j * S5_CH:(j + 1) * S5_CH, :]
                for a in range(S5_GB):
                    r0 = j * LANE + a * S5_CH
                    w_ref[r0:r0 + S5_CH, q * Q:(q + 1) * Q] = jnp.where(masks[a], blk, jnp.zeros_like(blk))

    z_ref[...] = jnp.dot(_s5_slab(u_ref, z_ref.shape[0]), w_ref[...], preferred_element_type=F32)


def _s5_out_kernel(u_ref, p_ref, t_ref, c_ref, y_ref, wi_ref, wc_ref):
    @pl.when(pl.program_id(1) == 0)
    def _():
        Q = S5_SW // 4
        masks = _s5_group_masks(LANE, S5_CH)
        for i in range(S5_L):
            cols = slice(i * LANE, (i + 1) * LANE)
            for j in range(S5_L):
                blk = t_ref[i, j * S5_CH:(j + 1) * S5_CH, :]
                for a in range(S5_GB):
                    r0 = j * LANE + a * S5_CH
                    wi_ref[r0:r0 + S5_CH, cols] = jnp.where(masks[a], blk, jnp.zeros_like(blk))
            for q in range(4):
                blk = c_ref[q, i]
                for b in range(S5_GB):
                    r0 = q * Q + b * S5_STATE
                    wc_ref[r0:r0 + S5_STATE, cols] = jnp.where(masks[b], blk, jnp.zeros_like(blk))

    rt = p_ref.shape[0]
    y = (jnp.dot(_s5_slab(u_ref, rt), wi_ref[...], preferred_element_type=F32)
         + jnp.dot(p_ref[...].astype(BF16), wc_ref[...], preferred_element_type=F32))
    for i in range(S5_L):
        y_ref[pl.ds(i, rt, stride=S5_L), :] = y[:, i * LANE:(i + 1) * LANE]


def _s5_scan_kernel(z_ref, dec_ref, x0_ref, p_ref, fin_ref, *, segments):
    Q = S5_SW // 4
    dec = [(dec_ref[0, 2 * d:2 * d + 1, :], dec_ref[0, 2 * d + 1:2 * d + 2, :]) for d in range(2)]
    seq_base = 0
    for (row0, n_seq, n_chunk) in segments:
        def seq_loop(sq, carry, row0=row0, n_chunk=n_chunk, seq_base=seq_base):
            r0 = row0 + sq * n_chunk
            sid = seq_base + sq
            x_init = tuple((x0_ref[0, pl.ds(4 * sid + 2 * d, 1), :], x0_ref[0, pl.ds(4 * sid + 2 * d + 1, 1), :])
                           for d in range(2))

            def step(n, xs):
                out = []
                for d in range(2):
                    xr, xi = xs[d]
                    ar, ai = dec[d]
                    r = r0 + (n if d == 0 else n_chunk - 1 - n)
                    cr = slice(2 * d * Q, (2 * d + 1) * Q)
                    ci = slice((2 * d + 1) * Q, (2 * d + 2) * Q)
                    zr = z_ref[pl.ds(r, 1), cr]
                    zi = z_ref[pl.ds(r, 1), ci]
                    p_ref[pl.ds(r, 1), cr] = xr
                    p_ref[pl.ds(r, 1), ci] = xi
                    out.append((ar * xr - ai * xi + zr, ar * xi + ai * xr + zi))
                return tuple(out)

            xs = lax.fori_loop(0, n_chunk, step, x_init)
            for d in range(2):
                fin_ref[0, pl.ds(4 * sid + 2 * d, 1), :] = xs[d][0]
                fin_ref[0, pl.ds(4 * sid + 2 * d + 1, 1), :] = xs[d][1]
            return carry

        lax.fori_loop(0, n_seq, seq_loop, 0)
        seq_base += n_seq


def _s5(p, weights, x0, segments):
    t2, st, cw, dec = weights
    N = p.shape[0]
    NC = N // S5_L
    SG = S5_SG
    Q = S5_SW // 4
    rt = _pick(256, NC)
    nrow = x0.shape[1]
    cb = C_S5 // LANE
    u_spec = pl.BlockSpec((rt * S5_L, LANE), lambda s, r: (r, cb + s))
    slab_spec = pl.BlockSpec((rt, S5_SW), lambda s, r: (r, s))
    z = pl.pallas_call(
        _s5_state_kernel, grid=(SG, NC // rt),
        in_specs=[u_spec, pl.BlockSpec((4, S5_L * S5_CH, Q), lambda s, r: (0, 0, s))], out_specs=slab_spec,
        out_shape=jax.ShapeDtypeStruct((NC, SG * S5_SW), F32),
        scratch_shapes=[pltpu.VMEM((S5_W, S5_SW), BF16)],
        compiler_params=_cp(("parallel", "arbitrary")), name="s5_chunk_state",
    )(p, st)
    state_spec = pl.BlockSpec((1, nrow, Q), lambda s: (s, 0, 0))
    carry_in, fin = pl.pallas_call(
        functools.partial(_s5_scan_kernel, segments=segments), grid=(SG,),
        in_specs=[pl.BlockSpec((NC, S5_SW), lambda s: (0, s)), pl.BlockSpec((1, 4, Q), lambda s: (s, 0, 0)),
                  state_spec],
        out_specs=[pl.BlockSpec((NC, S5_SW), lambda s: (0, s)), state_spec],
        out_shape=[jax.ShapeDtypeStruct((NC, SG * S5_SW), F32), jax.ShapeDtypeStruct((SG, nrow, Q), F32)],
        compiler_params=_cp(("parallel",)), name="s5_chunk_scan",
    )(z, dec, x0)
    y = pl.pallas_call(
        _s5_out_kernel, grid=(SG, NC // rt),
        in_specs=[u_spec, slab_spec, pl.BlockSpec((S5_L, S5_L * S5_CH, LANE), lambda s, r: (0, 0, s)),
                  pl.BlockSpec((4, S5_L, S5_STATE, LANE), lambda s, r: (0, 0, 0, s))],
        out_specs=pl.BlockSpec((rt * S5_L, LANE), lambda s, r: (r, s)),
        out_shape=jax.ShapeDtypeStruct((N, S5_WIDTH), F32),
        scratch_shapes=[pltpu.VMEM((S5_W, S5_W), BF16), pltpu.VMEM((S5_SW, S5_W), BF16)],
        compiler_params=_cp(("parallel", "arbitrary"), 56), name="s5_chunk_out",
    )(p, carry_in, t2, cw)
    return y, fin


def _s5_post_kernel(y_ref, u_ref, d_ref, w_ref, b_ref, o_ref):
    yg = jax.nn.gelu(y_ref[...] + d_ref[...] * u_ref[...])
    z = jnp.dot(yg.astype(BF16), w_ref[...], preferred_element_type=F32) + b_ref[...]
    o_ref[...] = (yg * jax.nn.sigmoid(z)).astype(o_ref.dtype)


def _s5_post(y, p, d, w, b, tm):
    N = y.shape[0]
    W = S5_WIDTH
    return pl.pallas_call(
        _s5_post_kernel, grid=(N // tm,),
        in_specs=[pl.BlockSpec((tm, W), lambda i: (i, 0)), pl.BlockSpec((tm, W), lambda i: (i, C_S5 // W)),
                  pl.BlockSpec((1, W), lambda i: (0, 0)), pl.BlockSpec((W, W), lambda i: (0, 0)),
                  pl.BlockSpec((1, W), lambda i: (0, 0))],
        out_specs=pl.BlockSpec((tm, W), lambda i: (i, 0)),
        out_shape=jax.ShapeDtypeStruct((N, W), BF16),
        compiler_params=_cp(("parallel",)), name="s5_glu",
    )(y, p, d, w, b)


def _merge_kernel(h_ref, a_ref, b_ref, c_ref, wg0, wg1, wg2, bg0, bg1, bg2, wb_ref, o_ref):
    h = h_ref[...]
    acc = None
    for i, (br, wg, bg) in enumerate(((a_ref, wg0, bg0), (b_ref, wg1, bg1), (c_ref, wg2, bg2))):
        gate = jax.nn.sigmoid(jnp.dot(h, wg[...], preferred_element_type=F32) + bg[...])
        t = gate * jnp.dot(br[...], wb_ref[i], preferred_element_type=F32)
        acc = t if acc is None else acc + t
    o_ref[...] = acc.astype(o_ref.dtype)


def _merge(h, o_da, o_gla, o_s5, w_gate, b_gate, w_branch, tm, tn):
    N, D = h.shape
    W = o_da.shape[1]
    nt = D // tn
    act = pl.BlockSpec((tm, W), lambda j, i: (i, 0))
    in_specs = [pl.BlockSpec((tm, D), lambda j, i: (i, 0)), act, act, act]
    in_specs += [pl.BlockSpec((D, tn), lambda j, i, br=br: (0, br * nt + j)) for br in range(N_BRANCH)]
    in_specs += [pl.BlockSpec((1, tn), lambda j, i, br=br: (0, br * nt + j)) for br in range(N_BRANCH)]
    in_specs += [pl.BlockSpec((N_BRANCH, W, tn), lambda j, i: (0, 0, j))]
    return pl.pallas_call(
        _merge_kernel, grid=(nt, N // tm), in_specs=in_specs,
        out_specs=pl.BlockSpec((tm, tn), lambda j, i: (i, j)),
        out_shape=jax.ShapeDtypeStruct((N, D), BF16),
        compiler_params=_cp(("parallel", "parallel"), 56), name="gated_merge",
    )(h, o_da, o_gla, o_s5, w_gate, w_gate, w_gate, b_gate, b_gate, b_gate, w_branch)


def _wout_kernel(m_ref, w_ref, x_ref, g_ref, o_ref):
    o_ref[...] = x_ref[...] + g_ref[0] * jnp.dot(m_ref[...], w_ref[...], preferred_element_type=F32)


def _wout(merged, w, x, gate, midx, tm, tn):
    N, D = x.shape
    return pl.pallas_call(
        _wout_kernel, grid=(D // tn, N // tm),
        in_specs=[pl.BlockSpec((tm, D), lambda j, i: (i, 0)), pl.BlockSpec((D, tn), lambda j, i: (0, j)),
                  pl.BlockSpec((tm, tn), lambda j, i: (i, j)), pl.BlockSpec((1, 1, tn), lambda j, i: (midx(i), 0, j))],
        out_specs=pl.BlockSpec((tm, tn), lambda j, i: (i, j)),
        out_shape=jax.ShapeDtypeStruct((N, D), F32),
        compiler_params=_cp(("parallel", "parallel")), name="out_proj_residual",
    )(merged, w, x, gate)


def _route_kernel(lg_ref, bias_ref, asg_ref, tile_ref, rank_ref, *, tl):
    E, n_tok = lg_ref.shape
    EPG = EXPERTS_PER_GROUP
    ninf = -jnp.inf
    eio = lax.broadcasted_iota(jnp.int32, (E, tl), 0)
    upper = jnp.where(lax.broadcasted_iota(jnp.int32, (tl, tl), 0) < lax.broadcasted_iota(jnp.int32, (tl, tl), 1),
                      1.0, 0.0).astype(BF16)
    ecol = lax.broadcasted_iota(jnp.int32, (E, 1), 0)

    def select(j, carry):
        ls = pl.ds(pl.multiple_of(j * tl, tl), tl)
        s = jax.nn.sigmoid(lg_ref[:, ls])
        sel = s + bias_ref[...]
        rows = [sel[e:e + 1, :] for e in range(E)]
        srow = [s[e:e + 1, :] for e in range(E)]
        gi = jnp.zeros((1, tl), jnp.int32)
        best = None
        for g in range(N_EXPERT_GROUPS):
            a, b, c, d = rows[g * EPG:(g + 1) * EPG]
            hi1, lo1, hi2, lo2 = jnp.maximum(a, b), jnp.minimum(a, b), jnp.maximum(c, d), jnp.minimum(c, d)
            gs = jnp.maximum(hi1, hi2) + jnp.maximum(jnp.minimum(hi1, hi2), jnp.maximum(lo1, lo2))
            if best is None:
                best = gs
            else:
                upd = gs > best
                gi = jnp.where(upd, g, gi)
                best = jnp.where(upd, gs, best)

        def pick(rs, i):
            out = rs[(N_EXPERT_GROUPS - 1) * EPG + i]
            for g in range(N_EXPERT_GROUPS - 2, -1, -1):
                out = jnp.where(gi == g, rs[g * EPG + i], out)
            return out

        v = [pick(rows, i) for i in range(EPG)]
        sv = [pick(srow, i) for i in range(EPG)]
        i1, b1, s1 = jnp.zeros((1, tl), jnp.int32), v[0], sv[0]
        for i in range(1, EPG):
            upd = v[i] > b1
            i1, b1, s1 = jnp.where(upd, i, i1), jnp.where(upd, v[i], b1), jnp.where(upd, sv[i], s1)
        i2, b2, s2 = jnp.zeros((1, tl), jnp.int32), jnp.full((1, tl), ninf, F32), sv[0]
        for i in range(EPG):
            cand = jnp.where(i1 == i, ninf, v[i])
            upd = cand > b2
            i2, b2, s2 = jnp.where(upd, i, i2), jnp.where(upd, cand, b2), jnp.where(upd, sv[i], s2)
        e0 = gi * EPG + i1
        e1 = gi * EPG + i2
        den = s1 + s2
        asg_ref[2:3, ls] = s1 / den
        asg_ref[3:4, ls] = s2 / den
        asg_ref[4:5, ls] = e0.astype(F32)
        asg_ref[5:6, ls] = e1.astype(F32)
        m = jnp.where((eio == e0) | (eio == e1), 1.0, 0.0)
        rank_ref[:, ls] = jnp.dot(m.astype(BF16), upper, preferred_element_type=F32) + carry
        return carry + jnp.sum(m, axis=1, keepdims=True)

    counts = lax.fori_loop(0, n_tok // tl, select, jnp.zeros((E, 1), F32))
    ptiles = jnp.floor((counts + (MOE_TM - 1)) * (1.0 / MOE_TM))
    run = jnp.zeros((1, 1), F32)
    pstart_t = jnp.zeros((E, 1), F32)
    for e in range(E):
        pstart_t = jnp.where(ecol == e, run, pstart_t)
        run = run + ptiles[e:e + 1, :]
    pstart = pstart_t * MOE_TM

    def place(j, _):
        ls = pl.ds(pl.multiple_of(j * tl, tl), tl)
        dest = rank_ref[:, ls] + pstart
        e0 = asg_ref[4:5, ls].astype(jnp.int32)
        e1 = asg_ref[5:6, ls].astype(jnp.int32)
        asg_ref[0:1, ls] = jnp.sum(jnp.where(eio == e0, dest, 0.0), axis=0, keepdims=True)
        asg_ref[1:2, ls] = jnp.sum(jnp.where(eio == e1, dest, 0.0), axis=0, keepdims=True)
        asg_ref[6:8, ls] = jnp.zeros((2, tl), F32)
        return 0

    lax.fori_loop(0, n_tok // tl, place, 0)

    ntp = tile_ref.shape[1]
    pend_t = pstart_t + ptiles
    ti = lax.broadcasted_iota(jnp.int32, (1, ntp), 1).astype(F32)
    te = jnp.sum(jnp.where(pend_t <= ti, 1.0, 0.0), axis=0, keepdims=True)
    used = ti < run
    last_e = jnp.max(jnp.where(counts > 0.0, ecol.astype(F32), 0.0), axis=0, keepdims=True)
    te = jnp.where(used, jnp.minimum(te, E - 1.0), last_e)
    tile_ref[0:1, :] = te.astype(jnp.int32)
    tile_ref[1:2, :] = jnp.where(used, 1, 0).astype(jnp.int32)
    tile_ref[2:8, :] = jnp.zeros((6, ntp), jnp.int32)


def _route(logits_t, router_bias, ntile):
    E, N = logits_t.shape
    assert MOE_TM & (MOE_TM - 1) == 0
    ntp = -(-ntile // LANE) * LANE
    full = lambda shape: pl.BlockSpec(shape, lambda i: (0, 0))
    return pl.pallas_call(
        functools.partial(_route_kernel, tl=_pick(512, N)), grid=(1,),
        in_specs=[full((E, N)), full((E, 1))],
        out_specs=[full((8, N)), full((8, ntp))],
        out_shape=[jax.ShapeDtypeStruct((8, N), F32), jax.ShapeDtypeStruct((8, ntp), jnp.int32)],
        scratch_shapes=[pltpu.VMEM((E, N), F32)],
        compiler_params=_cp(("arbitrary",)), name="moe_route",
    )(logits_t, router_bias.astype(F32).reshape(E, 1))


def _moe_kernel(te_ref, tv_ref, x_ref, w1_ref, w3_ref, w2_ref, o_ref):
    valid = tv_ref[pl.program_id(0)] != 0

    @pl.when(valid)
    def _():
        x = x_ref[...]
        F = w1_ref.shape[2]
        acc = None
        for f0 in range(0, F, MOE_FC):
            fs = slice(f0, min(f0 + MOE_FC, F))
            a = jnp.dot(x, w1_ref[0, :, fs], preferred_element_type=F32)
            b = jnp.dot(x, w3_ref[0, :, fs], preferred_element_type=F32)
            hid = ((a * jax.nn.sigmoid(a)) * b).astype(BF16)
            t = jnp.dot(hid, w2_ref[0, fs, :], preferred_element_type=F32)
            acc = t if acc is None else acc + t
        o_ref[...] = acc.astype(o_ref.dtype)

    @pl.when(jnp.logical_not(valid))
    def _():
        o_ref[...] = jnp.zeros(o_ref.shape, o_ref.dtype)


def _moe(xin, w1, w3, w2, tile_e, tile_valid):
    cap, D = xin.shape
    F = w1.shape[2]
    TM = MOE_TM
    grid_spec = pltpu.PrefetchScalarGridSpec(
        num_scalar_prefetch=2, grid=(cap // TM,),
        in_specs=[pl.BlockSpec((TM, D), lambda i, te, tv: (i, 0)),
                  pl.BlockSpec((1, D, F), lambda i, te, tv: (te[i], 0, 0)),
                  pl.BlockSpec((1, D, F), lambda i, te, tv: (te[i], 0, 0)),
                  pl.BlockSpec((1, F, D), lambda i, te, tv: (te[i], 0, 0))],
        out_specs=pl.BlockSpec((TM, D), lambda i, te, tv: (i, 0)))
    return pl.pallas_call(
        _moe_kernel, grid_spec=grid_spec,
        out_shape=jax.ShapeDtypeStruct((cap, D), BF16),
        compiler_params=_cp(("arbitrary",), 58), name="moe_experts",
    )(tile_e, tile_valid, xin, w1, w3, w2)


def _combine_kernel(x_ref, y0_ref, y1_ref, w0_ref, w1_ref, g_ref, o_ref):
    o_ref[...] = x_ref[...] + g_ref[0] * (y0_ref[...] * w0_ref[...] + y1_ref[...] * w1_ref[...])


def _combine(x, y0, y1, w0, w1, gate, midx, tm):
    N, D = x.shape
    blk = pl.BlockSpec((tm, D), lambda i: (i, 0))
    col = pl.BlockSpec((tm, 1), lambda i: (i, 0))
    return pl.pallas_call(
        _combine_kernel, grid=(N // tm,),
        in_specs=[blk, blk, blk, col, col, pl.BlockSpec((1, 1, D), lambda i: (midx(i), 0, 0))],
        out_specs=blk, out_shape=jax.ShapeDtypeStruct((N, D), F32),
        compiler_params=_cp(("parallel",)), name="moe_residual",
    )(x, y0, y1, w0, w1, gate)


def _rope_tables(n_tokens):
    rows = n_tokens // GRID_W
    row = jnp.repeat(jnp.arange(rows, dtype=F32), GRID_W)
    col = jnp.tile(jnp.arange(GRID_W, dtype=F32), rows)
    inv = ROPE_BASE ** (-jnp.arange(0, AX_DIM, 2, dtype=F32) / AX_DIM)
    ar = row[:, None] * inv[None, :]
    ac = col[:, None] * inv[None, :]
    ang = jnp.tile(jnp.concatenate([ar, ar, ac, ac], axis=-1), (1, LANE // DA_HEAD))
    first = (jnp.arange(LANE) % (AX_DIM)) < (AX_DIM // 2)
    return jnp.cos(ang), jnp.where(first[None, :], -jnp.sin(ang), jnp.sin(ang))


def kernel(x_prompt, x_sample, cache_k, cache_v, state_gla, state_s5_re, state_s5_im, c, c_ctx, w_mod, b_mod, g_norm, w_in, da_lambda_q, da_lambda_k, da_subln_g, gla_w_alpha, gla_b_alpha, gla_norm_g, s5_a_re, s5_a_im, s5_log_dt, s5_b_re, s5_b_im, s5_c_re, s5_c_im, s5_d, s5_w_glu, s5_b_glu, w_branch, w_gate, b_gate, w_out, router_w, router_bias, moe_w1, moe_w3, moe_w2, g_final):
    B, T, D = x_prompt.shape
    DB, DT, _ = x_sample.shape
    L = w_in.shape[0]
    PAST = cache_k.shape[2]
    NP, NS = B * T, DB * DT
    N = NP + NS
    nseq = B + DB
    assert D == D_MODEL and 1 + DB <= 8 and NP % DT == 0 and T % GLA_ROWS == 0 and DT % GLA_ROWS == 0
    tm = _pick(512, NP, DT)
    midx = functools.partial(_mod_index, tm=tm, n_ctx_tok=NP, dec_seq=DT)

    x = jnp.concatenate([x_prompt.reshape(NP, D), x_sample.reshape(NS, D)], axis=0)
    cv = jnp.zeros((8, D), F32).at[0].set(c_ctx).at[1:1 + DB].set(c)
    mods = _mod(cv, w_mod, b_mod).reshape(L, 8, N_MOD, 1, D)
    zero_mod = jnp.zeros((8, 1, D), F32)

    rope_tabs = _rope_tables(DT)
    ck = cache_k.reshape(DB, L, PAST, DA_HEADS * 2 * DA_HEAD)
    cvv = cache_v.reshape(DB, L, PAST, DA_HEADS * DA_VDIM)
    r = jnp.arange(GLA_ROWS)
    same = (r[:, None] // GLA_CHUNK) == (r[None, :] // GLA_CHUNK)
    tri = jnp.stack([same & (r[None, :] <= r[:, None]), same & (r[None, :] >= r[:, None])]).astype(BF16)
    rw = router_w.astype(F32).T
    rw_hi = rw.astype(BF16)
    rw_lo = (rw - rw_hi.astype(F32)).astype(BF16)
    ntile = N * TOP_K // MOE_TM + N_EXPERTS
    tok2 = jnp.tile(jnp.arange(N, dtype=jnp.int32), TOP_K)
    tq_ctx = _pick(256, T)
    tq_lat = _pick(256, DT)
    s5_segments = ((0, B, T // S5_L), (NP // S5_L, DB, DT // S5_L))

    k_list, v_list, g_list, re_list, im_list = [], [], [], [], []
    for l in range(L):
        m = mods[l]
        shift1, scale1, gate1, shift2, scale2, gate2 = (m[:, i] for i in range(N_MOD))
        wi = w_in[l]
        w_in_p = jnp.concatenate([wi[:, :C_S5], wi[:, C_S5 + GLA_RANK:], wi[:, C_S5:C_S5 + GLA_RANK],
                                  jnp.zeros((D, P_COLS - C_GA - GLA_RANK), F32)], axis=1).astype(BF16)
        h = _norm(x, g_norm[l, 0][None], scale1, shift1, midx, tm, BF16)
        p = _matmul(h, w_in_p, tm, 1536, F32, "in_proj")

        lam_init = 0.8 - 0.6 * math.exp(-0.3 * l)
        lq, lk = da_lambda_q[l], da_lambda_k[l]
        lam = jnp.exp(jnp.sum(lq[0] * lk[0])) - jnp.exp(jnp.sum(lq[1] * lk[1])) + lam_init
        lam_row = jnp.full((1, LANE), lam, F32)
        g_row = (da_subln_g[l] * (1.0 - lam_init))[None]
        o_da = jnp.concatenate([
            _attention(p, lam_row, g_row, 0, B, T, tq_ctx, DA_HEADS, _pick(ATT_KB, T)),
            _attention_latent(p, lam_row, g_row, NP, DB, DT, tq_lat, rope_tabs, ck, cvv, l)], axis=0)

        wa = jnp.zeros((1, 2, LANE, GLA_KTOT), F32).at[0, :, :GLA_RANK].set(gla_w_alpha[l]).astype(BF16)
        ba = gla_b_alpha[l].reshape(1, 2, 1, GLA_KTOT)
        o_f, o_b, gla_fin = _gla(p, wa, ba, tri, state_gla, l, ((B, T), (DB, DT)))
        o_gla = _gla_post(o_f, o_b, p, gla_norm_g[l][None], tm)

        mats = _s5_matrices(s5_a_re[l], s5_a_im[l], s5_log_dt[l], s5_b_re[l], s5_b_im[l], s5_c_re[l], s5_c_im[l])
        x0 = jnp.stack([state_s5_re[:, l], state_s5_im[:, l]], axis=2)
        x0 = jnp.concatenate([jnp.zeros((B,) + x0.shape[1:], F32), x0], axis=0)
        x0 = x0.reshape(4 * nseq, S5_SG, S5_SW // 4).transpose(1, 0, 2)
        y, s5_fin = _s5(p, mats, x0, s5_segments)
        o_s5 = _s5_post(y, p, s5_d[l][None], s5_w_glu[l].astype(BF16), s5_b_glu[l][None], tm)
        s5_fin = s5_fin.transpose(1, 0, 2).reshape(nseq, 2, 2, S5_GROUPS, S5_STATE)[:B]

        merged = _merge(h, o_da, o_gla, o_s5, w_gate[l].astype(BF16), b_gate[l][None], w_branch[l].astype(BF16),
                        tm, 512)
        x = _wout(merged, w_out[l].astype(BF16), x, gate1, midx, tm, 1024)

        h2, logits_t = _norm(x, g_norm[l, 1][None], scale2, shift2, midx, tm, BF16, router=(rw_hi, rw_lo))
        asg, tiles = _route(logits_t, router_bias, ntile)
        slot = asg[0:2].astype(jnp.int32)
        buf_tok = jnp.zeros((ntile * MOE_TM,), jnp.int32).at[slot.reshape(-1)].set(tok2, unique_indices=True)
        xin = jnp.take(h2, buf_tok, axis=0)
        ye = _moe(xin, moe_w1[l].astype(BF16), moe_w3[l].astype(BF16), moe_w2[l].astype(BF16), tiles[0, :ntile],
                  tiles[1, :ntile])
        x = _combine(x, jnp.take(ye, slot[0], axis=0), jnp.take(ye, slot[1], axis=0), asg[2].reshape(N, 1),
                     asg[3].reshape(N, 1), gate2, midx, tm)

        k_list.append(p[:NP, C_K:C_K + 1024].reshape(B, T, DA_HEADS, 2 * DA_HEAD))
        v_list.append(p[:NP, C_V:C_V + 1024].reshape(B, T, DA_HEADS, DA_VDIM))
        g_list.append(gla_fin[:B])
        re_list.append(s5_fin[:, :, 0])
        im_list.append(s5_fin[:, :, 1])

    y_all = _norm(x, g_final[None], zero_mod, zero_mod, midx, tm, F32)
    return (y_all[:NP].reshape(B, T, D), y_all[NP:].reshape(DB, DT, D),
            jnp.stack(k_list, axis=1), jnp.stack(v_list, axis=1), jnp.stack(g_list, axis=1),
            jnp.stack(re_list, axis=1), jnp.stack(im_list, axis=1))
```

```python
import functools
import math

import jax
import jax.numpy as jnp
from jax import lax
from jax.experimental import pallas as pl
from jax.experimental.pallas import tpu as pltpu

F32 = jnp.float32
BF16 = jnp.bfloat16

D_MODEL = 2048
GRID_W = 64
EPS = 1e-6
N_MOD = 6
N_BRANCH = 3
DA_HEADS = 8
DA_HEAD = 64
DA_VDIM = 128
ROPE_BASE = 10000.0
AX_DIM = DA_HEAD // 2
GLA_HEADS = 4
GLA_DK = 128
GLA_DV = 256
GLA_KTOT = GLA_HEADS * GLA_DK
GLA_WIDTH = GLA_HEADS * GLA_DV
GLA_RANK = 16
GLA_TAU = 16.0
GLA_CHUNK = 64
S5_WIDTH = 1024
S5_CH = 16
S5_GROUPS = S5_WIDTH // S5_CH
S5_STATE = 64
N_EXPERTS = 16
N_EXPERT_GROUPS = 4
EXPERTS_PER_GROUP = N_EXPERTS // N_EXPERT_GROUPS
TOPK_GROUP = 1
TOP_K = 2
D_FF_EXPERT = 1408

LANE = 128
C_Q, C_K, C_V, C_GQ, C_GK, C_GV, C_GR, C_S5, C_GA = 0, 1024, 2048, 3072, 3584, 4096, 5120, 6144, 7168
P_COLS = 7680
GLA_ROWS = 256
S5_L = 16
S5_SG = S5_WIDTH // LANE
S5_GB = S5_GROUPS // S5_SG
S5_W = S5_L * LANE
S5_SW = 4 * S5_GB * S5_STATE
MOE_TM = 512
MOE_FC = 512
ATT_KB = 512
VMEM_MB = 1024 * 1024


def _cp(sem, vmem_mb=48):
    return pltpu.CompilerParams(dimension_semantics=sem, vmem_limit_bytes=vmem_mb * VMEM_MB)


def _pick(pref, *ns):
    t = pref
    while any(n % t for n in ns):
        t //= 2
    return t


def _mod_index(i, tm, n_ctx_tok, dec_seq):
    t0 = i * tm
    return jnp.where(t0 < n_ctx_tok, 0, 1 + (t0 - n_ctx_tok) // dec_seq)


def _mod_kernel(c_ref, w_ref, b_ref, o_ref):
    c = c_ref[...]
    s = (c * jax.nn.sigmoid(c)).astype(BF16)
    o_ref[0] = jnp.dot(s, w_ref[0].astype(BF16), preferred_element_type=F32) + b_ref[0]


def _mod(cv, w_mod, b_mod):
    L, D, M = w_mod.shape
    tn = 1024
    return pl.pallas_call(
        _mod_kernel,
        grid=(L, M // tn),
        in_specs=[pl.BlockSpec((8, D), lambda l, j: (0, 0)),
                  pl.BlockSpec((1, D, tn), lambda l, j: (l, 0, j)),
                  pl.BlockSpec((1, 1, tn), lambda l, j: (l, 0, j))],
        out_specs=pl.BlockSpec((1, 8, tn), lambda l, j: (l, 0, j)),
        out_shape=jax.ShapeDtypeStruct((L, 8, M), F32),
        compiler_params=_cp(("parallel", "parallel")),
        name="mod_vectors",
    )(cv, w_mod, b_mod.reshape(L, 1, M))


def _cast_kernel(x_ref, o_ref):
    o_ref[...] = x_ref[...].astype(o_ref.dtype)


def _cast_bf16(w):
    lead, (R, C) = w.shape[:-2], w.shape[-2:]
    n = math.prod(lead)
    out = pl.pallas_call(
        _cast_kernel, grid=(n,),
        in_specs=[pl.BlockSpec((1, R, C), lambda i: (i, 0, 0))], out_specs=pl.BlockSpec((1, R, C), lambda i: (i, 0, 0)),
        out_shape=jax.ShapeDtypeStruct((n, R, C), BF16),
        compiler_params=_cp(("parallel",)), name="cast_bf16",
    )(w.reshape(n, R, C))
    return out.reshape(lead + (R, C))


def _norm_core(x_ref, g_ref, sc_ref, sh_ref):
    x = x_ref[...]
    y = x * lax.rsqrt(jnp.mean(x * x, axis=-1, keepdims=True) + EPS)
    return (y * g_ref[...]) * (1.0 + sc_ref[0]) + sh_ref[0]


def _norm_kernel(x_ref, g_ref, sc_ref, sh_ref, o_ref):
    o_ref[...] = _norm_core(x_ref, g_ref, sc_ref, sh_ref).astype(o_ref.dtype)


def _norm_router_kernel(x_ref, g_ref, sc_ref, sh_ref, rh_ref, rl_ref, o_ref, lg_ref):
    h = _norm_core(x_ref, g_ref, sc_ref, sh_ref)
    hh = h.astype(BF16)
    hl = (h - hh.astype(F32)).astype(BF16)
    o_ref[...] = hh
    nt = (((1,), (1,)), ((), ()))
    lg_ref[...] = (lax.dot_general(rh_ref[...], hh, nt, preferred_element_type=F32)
                   + lax.dot_general(rh_ref[...], hl, nt, preferred_element_type=F32)
                   + lax.dot_general(rl_ref[...], hh, nt, preferred_element_type=F32))


def _norm(x, g, sc, sh, midx, tm, out_dtype, router=None, rows=None):
    N, D = x.shape
    r0, n = (0, N) if rows is None else rows
    t0 = r0 // tm
    in_specs = [pl.BlockSpec((tm, D), lambda i: (t0 + i, 0)),
                pl.BlockSpec((1, D), lambda i: (0, 0)),
                pl.BlockSpec((1, 1, D), lambda i: (midx(t0 + i), 0, 0)),
                pl.BlockSpec((1, 1, D), lambda i: (midx(t0 + i), 0, 0))]
    if router is None:
        return pl.pallas_call(
            _norm_kernel, grid=(n // tm,), in_specs=in_specs,
            out_specs=pl.BlockSpec((tm, D), lambda i: (i, 0)),
            out_shape=jax.ShapeDtypeStruct((n, D), out_dtype),
            compiler_params=_cp(("parallel",)), name="norm_mod",
        )(x, g, sc, sh)
    assert rows is None
    rh, rl = router
    E = rh.shape[0]
    in_specs += [pl.BlockSpec((E, D), lambda i: (0, 0)), pl.BlockSpec((E, D), lambda i: (0, 0))]
    return pl.pallas_call(
        _norm_router_kernel, grid=(N // tm,), in_specs=in_specs,
        out_specs=[pl.BlockSpec((tm, D), lambda i: (i, 0)), pl.BlockSpec((E, tm), lambda i: (0, i))],
        out_shape=[jax.ShapeDtypeStruct((N, D), BF16), jax.ShapeDtypeStruct((E, N), F32)],
        compiler_params=_cp(("parallel",)), name="norm_mod_router",
    )(x, g, sc, sh, rh, rl)


def _mm_kernel(a_ref, w_ref, o_ref):
    o_ref[...] = jnp.dot(a_ref[...], w_ref[...], preferred_element_type=F32).astype(o_ref.dtype)


def _matmul(a, w, tm, tn, out_dtype, name):
    M, K = a.shape
    Nc = w.shape[1]
    return pl.pallas_call(
        _mm_kernel, grid=(Nc // tn, M // tm),
        in_specs=[pl.BlockSpec((tm, K), lambda j, i: (i, 0)), pl.BlockSpec((K, tn), lambda j, i: (0, j))],
        out_specs=pl.BlockSpec((tm, tn), lambda j, i: (i, j)),
        out_shape=jax.ShapeDtypeStruct((M, Nc), out_dtype),
        compiler_params=_cp(("parallel", "parallel")), name=name,
    )(a, w)


def _kv_store_kernel(k_ref, v_ref, *refs):
    ko_ref, vo_ref = refs[-2:]
    ko_ref[...] = k_ref[...]
    vo_ref[...] = v_ref[...]


def _kv_store(p, bufs, layer, n_layers, B, T):
    W = DA_HEADS * DA_VDIM
    in_specs = [pl.BlockSpec((T, W), lambda b: (b, C_K // W)), pl.BlockSpec((T, W), lambda b: (b, C_V // W))]
    args = [p, p]
    aliases = {}
    if bufs is not None:
        in_specs += [pl.BlockSpec(memory_space=pl.ANY), pl.BlockSpec(memory_space=pl.ANY)]
        args += list(bufs)
        aliases = {2: 0, 3: 1}
    out_spec = pl.BlockSpec((None, None, T, W), lambda b: (b, layer, 0, 0))
    shape = jax.ShapeDtypeStruct((B, n_layers, T, W), F32)
    return pl.pallas_call(
        _kv_store_kernel, grid=(B,), in_specs=in_specs, out_specs=[out_spec, out_spec], out_shape=[shape, shape],
        input_output_aliases=aliases, compiler_params=_cp(("parallel",)), name="kv_store",
    )(*args)


def _rope(x, cos, sin_signed):
    lane = lax.broadcasted_iota(jnp.int32, (1, LANE), 1)
    first = (lane % (2 * (AX_DIM // 2))) < (AX_DIM // 2)
    xr = jnp.where(first, pltpu.roll(x, LANE - AX_DIM // 2, 1), pltpu.roll(x, AX_DIM // 2, 1))
    return x * cos + xr * sin_signed


def _attn_kernel(*refs, rope, ctx, hs, kb):
    refs = list(refs)
    q_ref, k_ref, v_ref = refs[:3]
    refs = refs[3:]
    if ctx:
        ck_ref, cv_ref = refs[:2]
        refs = refs[2:]
    if rope:
        cq_ref, sq_ref, ckk_ref, skk_ref = refs[:4]
        refs = refs[4:]
    lam_ref, g_ref, o_ref, kb_ref, vb_ref = refs[:5]
    if ctx:
        ckb_ref, cvb_ref = refs[5:]

    hsl = [slice(h * LANE, (h + 1) * LANE) for h in range(hs)]

    @pl.when(pl.program_id(2) == 0)
    def _():
        for h in range(hs):
            k = k_ref[:, hsl[h]]
            if rope:
                k = _rope(k, ckk_ref[...], skk_ref[...])
            kb_ref[:, hsl[h]] = k.astype(BF16)
        vb_ref[...] = v_ref[...].astype(BF16)
        if ctx:
            ckb_ref[...] = ck_ref[...].astype(BF16)
            cvb_ref[...] = cv_ref[...].astype(BF16)

    tq = q_ref.shape[0]
    n_new = kb_ref.shape[0]
    lane = lax.broadcasted_iota(jnp.int32, (1, LANE), 1)
    lo = lane < DA_HEAD
    nt = (((1,), (1,)), ((), ()))
    qqs = []
    for h in range(hs):
        q = q_ref[:, hsl[h]]
        if rope:
            q = _rope(q, cq_ref[...], sq_ref[...])
        q = q * (DA_HEAD ** -0.5 * math.log2(math.e))
        qqs.append(jnp.concatenate([jnp.where(lo, q, 0.0), jnp.where(lo, 0.0, q)], axis=0).astype(BF16))

    def block(qq, kblk, vblk, carry):
        m, l, acc = carry
        s = lax.dot_general(qq, kblk, nt, preferred_element_type=F32)
        mn = jnp.maximum(m, jnp.max(s, axis=-1, keepdims=True))
        e = jnp.exp2(s - mn)
        r = jnp.exp2(m - mn)
        l = r * l + jnp.sum(e, axis=-1, keepdims=True)
        acc = r * acc + jnp.dot(e.astype(BF16), vblk, preferred_element_type=F32)
        return mn, l, acc

    def all_heads(kref, vref, rows, carries):
        return tuple(block(qqs[h], kref[rows, hsl[h]], vref[rows, hsl[h]], carries[h]) for h in range(hs))

    carries = tuple((jnp.full((2 * tq, 1), -jnp.inf, F32), jnp.zeros((2 * tq, 1), F32),
                     jnp.zeros((2 * tq, LANE), F32)) for _ in range(hs))
    if ctx:
        for c in range(ckb_ref.shape[0] // kb):
            carries = all_heads(ckb_ref, cvb_ref, slice(c * kb, (c + 1) * kb), carries)
    if n_new // kb == 1:
        carries = all_heads(kb_ref, vb_ref, slice(None), carries)
    else:
        def body(j, carries):
            return all_heads(kb_ref, vb_ref, pl.ds(pl.multiple_of(j * kb, kb), kb), carries)
        carries = lax.fori_loop(0, n_new // kb, body, carries)
    for h in range(hs):
        _, l, acc = carries[h]
        o = acc[:tq] * (1.0 / l[:tq]) - acc[tq:] * (lam_ref[:, :1] / l[tq:])
        y = o * lax.rsqrt(jnp.mean(o * o, axis=-1, keepdims=True) + EPS)
        o_ref[:, hsl[h]] = (y * g_ref[...]).astype(o_ref.dtype)


def _attention(p, lam_row, g_row, row0, B, T, tq, hs, kb, rope_tabs=None, ctx=None):
    nq = T // tq
    rb0 = row0 // tq
    kb0 = row0 // T
    W = hs * LANE
    hq, hk, hv = C_Q // W, C_K // W, C_V // W
    assert T % kb == 0
    in_specs = [pl.BlockSpec((tq, W), lambda b, h, i: (rb0 + b * nq + i, hq + h)),
                pl.BlockSpec((T, W), lambda b, h, i: (kb0 + b, hk + h)),
                pl.BlockSpec((T, W), lambda b, h, i: (kb0 + b, hv + h))]
    args = [p, p, p]
    scratch = [pltpu.VMEM((T, W), BF16), pltpu.VMEM((T, W), BF16)]
    if ctx is not None:
        ck, cv, layer = ctx
        past = ck.shape[2]
        assert past % kb == 0
        in_specs += [pl.BlockSpec((None, None, past, W), lambda b, h, i: (b, layer, 0, h)),
                     pl.BlockSpec((None, None, past, W), lambda b, h, i: (b, layer, 0, h))]
        args += [ck, cv]
        scratch += [pltpu.VMEM((past, W), BF16), pltpu.VMEM((past, W), BF16)]
    if rope_tabs is not None:
        cos, sin = rope_tabs
        in_specs += [pl.BlockSpec((tq, LANE), lambda b, h, i: (i, 0)), pl.BlockSpec((tq, LANE), lambda b, h, i: (i, 0)),
                     pl.BlockSpec((T, LANE), lambda b, h, i: (0, 0)), pl.BlockSpec((T, LANE), lambda b, h, i: (0, 0))]
        args += [cos, sin, cos, sin]
    in_specs += [pl.BlockSpec((1, LANE), lambda b, h, i: (0, 0)), pl.BlockSpec((1, LANE), lambda b, h, i: (0, 0))]
    args += [lam_row, g_row]
    return pl.pallas_call(
        functools.partial(_attn_kernel, rope=rope_tabs is not None, ctx=ctx is not None, hs=hs, kb=kb),
        grid=(B, DA_HEADS // hs, nq),
        in_specs=in_specs,
        out_specs=pl.BlockSpec((tq, W), lambda b, h, i: (b * nq + i, h)),
        out_shape=jax.ShapeDtypeStruct((B * T, DA_HEADS * DA_VDIM), BF16),
        scratch_shapes=scratch,
        compiler_params=_cp(("parallel", "parallel", "arbitrary"), 56),
        name="diff_attention_latent" if ctx is not None else "diff_attention_context",
    )(*args)


def _attn_latent_kernel(q_ref, k_ref, v_ref, ck_ref, cv_ref, cq_ref, sq_ref, ckk_ref, skk_ref, lam_ref, g_ref, o_ref,
                        kb_ref, vb_ref):
    @pl.when(pl.program_id(2) == 0)
    def _():
        kb_ref[...] = _rope(k_ref[...], ckk_ref[...], skk_ref[...]).astype(BF16)
        vb_ref[...] = v_ref[...].astype(BF16)

    q = _rope(q_ref[...], cq_ref[...], sq_ref[...]) * (DA_HEAD ** -0.5)
    lane = lax.broadcasted_iota(jnp.int32, (1, LANE), 1)
    lo = lane < DA_HEAD
    q0 = jnp.where(lo, q, 0.0).astype(BF16)
    q1 = jnp.where(lo, 0.0, q).astype(BF16)
    nt = (((1,), (1,)), ((), ()))
    kb = kb_ref[...]
    ckb = ck_ref[...].astype(BF16)
    s0 = lax.dot_general(q0, kb, nt, preferred_element_type=F32)
    s1 = lax.dot_general(q1, kb, nt, preferred_element_type=F32)
    c0 = lax.dot_general(q0, ckb, nt, preferred_element_type=F32)
    c1 = lax.dot_general(q1, ckb, nt, preferred_element_type=F32)
    m0 = jnp.maximum(jnp.max(s0, axis=-1, keepdims=True), jnp.max(c0, axis=-1, keepdims=True))
    m1 = jnp.maximum(jnp.max(s1, axis=-1, keepdims=True), jnp.max(c1, axis=-1, keepdims=True))
    e0 = jnp.exp(s0 - m0)
    e1 = jnp.exp(s1 - m1)
    ec0 = jnp.exp(c0 - m0)
    ec1 = jnp.exp(c1 - m1)
    l0 = jnp.sum(e0, axis=-1, keepdims=True) + jnp.sum(ec0, axis=-1, keepdims=True)
    l1 = jnp.sum(e1, axis=-1, keepdims=True) + jnp.sum(ec1, axis=-1, keepdims=True)
    a0 = 1.0 / l0
    a1 = lam_ref[:, :1] / l1
    o = jnp.dot((e0 * a0 - e1 * a1).astype(BF16), vb_ref[...], preferred_element_type=F32)
    o = o + jnp.dot((ec0 * a0 - ec1 * a1).astype(BF16), cv_ref[...].astype(BF16), preferred_element_type=F32)
    y = o * lax.rsqrt(jnp.mean(o * o, axis=-1, keepdims=True) + EPS)
    o_ref[...] = (y * g_ref[...]).astype(o_ref.dtype)


def _attention_latent(p, lam_row, g_row, row0, B, T, tq, rope_tabs, ck, cv, layer):
    nq = T // tq
    rb0 = row0 // tq
    kb0 = row0 // T
    hq, hk, hv = C_Q // LANE, C_K // LANE, C_V // LANE
    past = ck.shape[2]
    cos, sin = rope_tabs
    in_specs = [pl.BlockSpec((tq, LANE), lambda b, h, i: (rb0 + b * nq + i, hq + h)),
                pl.BlockSpec((T, LANE), lambda b, h, i: (kb0 + b, hk + h)),
                pl.BlockSpec((T, LANE), lambda b, h, i: (kb0 + b, hv + h)),
                pl.BlockSpec((None, None, past, LANE), lambda b, h, i: (b, layer, 0, h)),
                pl.BlockSpec((None, None, past, LANE), lambda b, h, i: (b, layer, 0, h)),
                pl.BlockSpec((tq, LANE), lambda b, h, i: (i, 0)), pl.BlockSpec((tq, LANE), lambda b, h, i: (i, 0)),
                pl.BlockSpec((T, LANE), lambda b, h, i: (0, 0)), pl.BlockSpec((T, LANE), lambda b, h, i: (0, 0)),
                pl.BlockSpec((1, LANE), lambda b, h, i: (0, 0)), pl.BlockSpec((1, LANE), lambda b, h, i: (0, 0))]
    return pl.pallas_call(
        _attn_latent_kernel, grid=(B, DA_HEADS, nq), in_specs=in_specs,
        out_specs=pl.BlockSpec((tq, LANE), lambda b, h, i: (b * nq + i, h)),
        out_shape=jax.ShapeDtypeStruct((B * T, DA_HEADS * DA_VDIM), BF16),
        scratch_shapes=[pltpu.VMEM((T, LANE), BF16), pltpu.VMEM((T, LANE), BF16)],
        compiler_params=_cp(("parallel", "parallel", "arbitrary"), 56),
        name="diff_attention_latent",
    )(p, p, p, ck, cv, cos, sin, cos, sin, lam_row, g_row)


def _log_sigmoid(z):
    return jnp.minimum(z, 0.0) - jnp.log1p(jnp.exp(-jnp.abs(z)))


def _gla_direction(d, q_ref, k_ref, v_ref, ga_ref, wa_ref, ba_ref, tri_ref, o_ref, st_ref):
    C = GLA_CHUNK
    nsub = GLA_ROWS // C
    z = jnp.dot(ga_ref[...].astype(BF16), wa_ref[0, d], preferred_element_type=F32) + ba_ref[0, d]
    la = _log_sigmoid(z) / GLA_TAU
    hi = la.astype(BF16)
    r1 = la - hi.astype(F32)
    mid = r1.astype(BF16)
    low = (r1 - mid.astype(F32)).astype(BF16)
    tri = tri_ref[d]
    bsum = (jnp.dot(tri, hi, preferred_element_type=F32) + jnp.dot(tri, mid, preferred_element_type=F32)
            + jnp.dot(tri, low, preferred_element_type=F32))
    row = lax.broadcasted_iota(jnp.int32, (C, C), 0)
    col = lax.broadcasted_iota(jnp.int32, (C, C), 1)
    mask = (row >= col) if d == 0 else (row <= col)
    nt = (((1,), (1,)), ((), ()))
    tn = (((0,), (0,)), ((), ()))
    subs = range(nsub) if d == 0 else range(nsub - 1, -1, -1)
    for h in range(GLA_HEADS):
        hk = slice(h * GLA_DK, (h + 1) * GLA_DK)
        hv = slice(h * GLA_DV, (h + 1) * GLA_DV)
        st = st_ref[d, h]
        for s in subs:
            rs = slice(s * C, (s + 1) * C)
            end = s * C + C - 1 if d == 0 else s * C
            bb = bsum[rs, hk]
            bl = bsum[end:end + 1, hk]
            q = q_ref[rs, hk] * (GLA_DK ** -0.5)
            k = k_ref[rs, hk]
            v = v_ref[rs, hv].astype(BF16)
            qt = (q * jnp.exp(bb)).astype(BF16)
            kt = (k * jnp.exp(-bb)).astype(BF16)
            kend = (k * jnp.exp(bl - bb)).astype(BF16)
            att = jnp.where(mask, lax.dot_general(qt, kt, nt, preferred_element_type=F32), 0.0).astype(BF16)
            o = jnp.dot(att, v, preferred_element_type=F32)
            o = o + lax.dot_general(qt, st.astype(BF16), nt, preferred_element_type=F32)
            o_ref[rs, hv] = o
            st = st * jnp.exp(bl) + lax.dot_general(v, kend, tn, preferred_element_type=F32)
        st_ref[d, h] = st


def _gla_kernel(qf, kf, vf, gaf, qb, kb, vb, gab, wa_ref, ba_ref, tri_ref, s0_ref, of_ref, ob_ref, fin_ref, st_ref,
                *, seq_pos):
    first, last, stateless = seq_pos(pl.program_id(0))

    @pl.when(first & stateless)
    def _():
        st_ref[...] = jnp.zeros(st_ref.shape, F32)

    @pl.when(first & jnp.logical_not(stateless))
    def _():
        for d in range(2):
            for h in range(GLA_HEADS):
                st_ref[d, h] = s0_ref[d, h].T

    _gla_direction(0, qf, kf, vf, gaf, wa_ref, ba_ref, tri_ref, of_ref, st_ref)
    _gla_direction(1, qb, kb, vb, gab, wa_ref, ba_ref, tri_ref, ob_ref, st_ref)

    @pl.when(last & stateless)
    def _():
        for d in range(2):
            for h in range(GLA_HEADS):
                fin_ref[d, h] = st_ref[d, h].T


def _gla(p, wa, ba, tri, state, layer, seqs):
    N = p.shape[0]
    R = GLA_ROWS
    (n1, t1), (n2, t2) = seqs
    nb1, nb2 = t1 // R, t2 // R
    steps1 = n1 * nb1

    def locate(s):
        in1 = s < steps1
        s2 = s - steps1
        nb = jnp.where(in1, nb1, nb2)
        seq = jnp.where(in1, s // nb1, n1 + s2 // nb2)
        n = jnp.where(in1, s % nb1, s2 % nb2)
        base = jnp.where(in1, (s // nb1) * nb1, steps1 + (s2 // nb2) * nb2)
        return seq, n, nb, base

    def fwd_blk(s):
        _, n, _, base = locate(s)
        return base + n

    def bwd_blk(s):
        _, n, nb, base = locate(s)
        return base + nb - 1 - n

    def seq_pos(s):
        _, n, nb, _ = locate(s)
        return n == 0, n == nb - 1, s < steps1

    def specs(blk):
        return [pl.BlockSpec((R, GLA_KTOT), lambda s: (blk(s), C_GQ // GLA_KTOT)),
                pl.BlockSpec((R, GLA_KTOT), lambda s: (blk(s), C_GK // GLA_KTOT)),
                pl.BlockSpec((R, GLA_WIDTH), lambda s: (blk(s), C_GV // GLA_WIDTH)),
                pl.BlockSpec((R, LANE), lambda s: (blk(s), C_GA // LANE))]

    s0_spec = pl.BlockSpec((None, None, 2, GLA_HEADS, GLA_DK, GLA_DV),
                           lambda s: (jnp.maximum(locate(s)[0] - n1, 0), layer, 0, 0, 0, 0))
    fin_spec = pl.BlockSpec((None, 2, GLA_HEADS, GLA_DK, GLA_DV),
                            lambda s: (jnp.minimum(locate(s)[0], n1 - 1), 0, 0, 0, 0))
    return pl.pallas_call(
        functools.partial(_gla_kernel, seq_pos=seq_pos),
        grid=(N // R,),
        in_specs=specs(fwd_blk) + specs(bwd_blk) + [
            pl.BlockSpec((1, 2, LANE, GLA_KTOT), lambda s: (0, 0, 0, 0)),
            pl.BlockSpec((1, 2, 1, GLA_KTOT), lambda s: (0, 0, 0, 0)),
            pl.BlockSpec((2, R, R), lambda s: (0, 0, 0)),
            s0_spec],
        out_specs=[pl.BlockSpec((R, GLA_WIDTH), lambda s: (fwd_blk(s), 0)),
                   pl.BlockSpec((R, GLA_WIDTH), lambda s: (bwd_blk(s), 0)),
                   fin_spec],
        out_shape=[jax.ShapeDtypeStruct((N, GLA_WIDTH), F32), jax.ShapeDtypeStruct((N, GLA_WIDTH), F32),
                   jax.ShapeDtypeStruct((n1, 2, GLA_HEADS, GLA_DK, GLA_DV), F32)],
        scratch_shapes=[pltpu.VMEM((2, GLA_HEADS, GLA_DV, GLA_DK), F32)],
        compiler_params=_cp(("arbitrary",)),
        name="gla_bidir",
    )(p, p, p, p, p, p, p, p, wa, ba, tri, state)


def _gla_post_kernel(of_ref, ob_ref, r_ref, g_ref, o_ref):
    for h in range(GLA_HEADS):
        hv = slice(h * GLA_DV, (h + 1) * GLA_DV)
        o = of_ref[:, hv] + ob_ref[:, hv]
        y = o * lax.rsqrt(jnp.mean(o * o, axis=-1, keepdims=True) + EPS) * g_ref[...]
        r = r_ref[:, hv]
        o_ref[:, hv] = (y * (r * jax.nn.sigmoid(r))).astype(o_ref.dtype)


def _gla_post(of, ob, p, g, tm):
    N = of.shape[0]
    return pl.pallas_call(
        _gla_post_kernel, grid=(N // tm,),
        in_specs=[pl.BlockSpec((tm, GLA_WIDTH), lambda i: (i, 0)), pl.BlockSpec((tm, GLA_WIDTH), lambda i: (i, 0)),
                  pl.BlockSpec((tm, GLA_WIDTH), lambda i: (i, C_GR // GLA_WIDTH)),
                  pl.BlockSpec((1, GLA_DV), lambda i: (0, 0))],
        out_specs=pl.BlockSpec((tm, GLA_WIDTH), lambda i: (i, 0)),
        out_shape=jax.ShapeDtypeStruct((N, GLA_WIDTH), BF16),
        compiler_params=_cp(("parallel",)), name="gla_norm_gate",
    )(of, ob, p, g)


def _s5_matrices(a_re, a_im, log_dt, b_re, b_im, c_re, c_im):
    hp = lax.Precision.HIGHEST
    L, G, P, CH = S5_L, S5_GROUPS, S5_STATE, S5_CH
    taus = jnp.arange(L + 1, dtype=F32)
    ii = jnp.arange(L)
    kerns, powers, bbars, carry, decay = [], [], [], [], []
    c_re_t, c_im_t = c_re.transpose(2, 0, 1), c_im.transpose(2, 0, 1)
    for d in range(2):
        dt = jnp.exp(log_dt[d])[:, None]
        ld_re, ld_im = a_re[d] * dt, a_im[d] * dt
        mag = jnp.exp(ld_re)
        lb_re, lb_im = mag * jnp.cos(ld_im), mag * jnp.sin(ld_im)
        den = a_re[d] * a_re[d] + a_im[d] * a_im[d]
        q_re = ((lb_re - 1.0) * a_re[d] + lb_im * a_im[d]) / den
        q_im = (lb_im * a_re[d] - (lb_re - 1.0) * a_im[d]) / den
        bb_re = q_re[..., None] * b_re - q_im[..., None] * b_im
        bb_im = q_re[..., None] * b_im + q_im[..., None] * b_re
        pmag = jnp.exp(ld_re[None] * taus[:, None, None])
        pw_re = pmag * jnp.cos(ld_im[None] * taus[:, None, None])
        pw_im = pmag * jnp.sin(ld_im[None] * taus[:, None, None])
        cp_re = c_re[None] * pw_re[:L, :, None, :] - c_im[None] * pw_im[:L, :, None, :]
        cp_im = c_re[None] * pw_im[:L, :, None, :] + c_im[None] * pw_re[:L, :, None, :]
        kern = (jnp.einsum('tgop,gpc->tcgo', cp_re, bb_re, precision=hp)
                - jnp.einsum('tgop,gpc->tcgo', cp_im, bb_im, precision=hp)).reshape(L, CH, G * CH)
        kerns.append(kern)
        powers.append(jnp.stack([pw_re.reshape(L + 1, G * P), pw_im.reshape(L + 1, G * P)]))
        bbars.append(jnp.stack([bb_re.transpose(2, 0, 1).reshape(CH, G * P), bb_im.transpose(2, 0, 1).reshape(CH, G * P)]))
        e = (ii + 1) if d == 0 else (L - ii)
        pt_re, pt_im = pw_re[e].transpose(0, 2, 1)[..., None], pw_im[e].transpose(0, 2, 1)[..., None]
        cw_re = c_re_t[None] * pt_re - c_im_t[None] * pt_im
        cw_im = c_re_t[None] * pt_im + c_im_t[None] * pt_re
        carry += [cw_re.reshape(L, P, G * CH), (-cw_im).reshape(L, P, G * CH)]
        decay.append((pw_re[L], pw_im[L]))
    kk = jnp.stack([kerns[0].at[0].add(kerns[1][0]), kerns[1]]).astype(BF16)
    pw = jnp.stack(powers)
    bb = jnp.stack(bbars)
    cw = jnp.stack(carry).astype(BF16)
    dec = jnp.stack([jnp.stack([x[0], x[1]]) for x in decay])
    dec = dec.reshape(4, S5_SG, S5_SW // 4).transpose(1, 0, 2)
    return kk, pw, bb, cw, dec


def _s5_slab(u_ref, rt):
    return jnp.concatenate([u_ref[pl.ds(j, rt, stride=S5_L), :] for j in range(S5_L)], axis=1).astype(BF16)


def _s5_group_masks(width, per_group):
    grp = (lax.broadcasted_iota(jnp.int32, (1, width), 1) // per_group) % S5_GB
    return [grp == a for a in range(S5_GB)]


def _s5_state_kernel(u_ref, pw_ref, bb_ref, z_ref, w_ref):
    @pl.when(pl.program_id(1) == 0)
    def _():
        Q = S5_SW // 4
        masks = _s5_group_masks(Q, S5_STATE)
        for d in range(2):
            b_re, b_im = bb_ref[d, 0], bb_ref[d, 1]
            for j in range(S5_L):
                e = S5_L - 1 - j if d == 0 else j
                p_re, p_im = pw_ref[d, 0, e:e + 1, :], pw_ref[d, 1, e:e + 1, :]
                parts = ((p_re * b_re - p_im * b_im).astype(BF16), (p_re * b_im + p_im * b_re).astype(BF16))
                for ri in range(2):
                    q = 2 * d + ri
                    for a in range(S5_GB):
                        r0 = j * LANE + a * S5_CH
                        w_ref[r0:r0 + S5_CH, q * Q:(q + 1) * Q] = jnp.where(masks[a], parts[ri],
                                                                            jnp.zeros_like(parts[ri]))

    z_ref[...] = jnp.dot(_s5_slab(u_ref, z_ref.shape[0]), w_ref[...], preferred_element_type=F32)


def _s5_out_kernel(u_ref, p_ref, k_ref, c_ref, y_ref, wi_ref, wc_ref):
    @pl.when(pl.program_id(1) == 0)
    def _():
        Q = S5_SW // 4
        masks = _s5_group_masks(LANE, S5_CH)
        for i in range(S5_L):
            cols = slice(i * LANE, (i + 1) * LANE)
            for j in range(S5_L):
                blk = k_ref[0 if i >= j else 1, abs(i - j)]
                for a in range(S5_GB):
                    r0 = j * LANE + a * S5_CH
                    wi_ref[r0:r0 + S5_CH, cols] = jnp.where(masks[a], blk, jnp.zeros_like(blk))
            for q in range(4):
                blk = c_ref[q, i]
                for b in range(S5_GB):
                    r0 = q * Q + b * S5_STATE
                    wc_ref[r0:r0 + S5_STATE, cols] = jnp.where(masks[b], blk, jnp.zeros_like(blk))

    rt = p_ref.shape[0]
    y = (jnp.dot(_s5_slab(u_ref, rt), wi_ref[...], preferred_element_type=F32)
         + jnp.dot(p_ref[...].astype(BF16), wc_ref[...], preferred_element_type=F32))
    for i in range(S5_L):
        y_ref[pl.ds(i, rt, stride=S5_L), :] = y[:, i * LANE:(i + 1) * LANE]


def _s5_scan_kernel(z_ref, dec_ref, x0_ref, p_ref, fin_ref, *, segments):
    Q = S5_SW // 4
    dec = [(dec_ref[0, 2 * d:2 * d + 1, :], dec_ref[0, 2 * d + 1:2 * d + 2, :]) for d in range(2)]
    seq_base = 0
    for (row0, n_seq, n_chunk) in segments:
        def seq_loop(sq, carry, row0=row0, n_chunk=n_chunk, seq_base=seq_base):
            r0 = row0 + sq * n_chunk
            sid = seq_base + sq
            x_init = tuple((x0_ref[0, pl.ds(4 * sid + 2 * d, 1), :], x0_ref[0, pl.ds(4 * sid + 2 * d + 1, 1), :])
                           for d in range(2))

            def step(n, xs):
                out = []
                for d in range(2):
                    xr, xi = xs[d]
                    ar, ai = dec[d]
                    r = r0 + (n if d == 0 else n_chunk - 1 - n)
                    cr = slice(2 * d * Q, (2 * d + 1) * Q)
                    ci = slice((2 * d + 1) * Q, (2 * d + 2) * Q)
                    zr = z_ref[pl.ds(r, 1), cr]
                    zi = z_ref[pl.ds(r, 1), ci]
                    p_ref[pl.ds(r, 1), cr] = xr
                    p_ref[pl.ds(r, 1), ci] = xi
                    out.append((ar * xr - ai * xi + zr, ar * xi + ai * xr + zi))
                return tuple(out)

            xs = lax.fori_loop(0, n_chunk, step, x_init)
            for d in range(2):
                fin_ref[0, pl.ds(4 * sid + 2 * d, 1), :] = xs[d][0]
                fin_ref[0, pl.ds(4 * sid + 2 * d + 1, 1), :] = xs[d][1]
            return carry

        lax.fori_loop(0, n_seq, seq_loop, 0)
        seq_base += n_seq


def _s5(p, weights, x0, segments):
    kk, pw, bb, cw, dec = weights
    N = p.shape[0]
    NC = N // S5_L
    SG = S5_SG
    Q = S5_SW // 4
    rt = _pick(256, NC)
    nrow = x0.shape[1]
    cb = C_S5 // LANE
    u_spec = pl.BlockSpec((rt * S5_L, LANE), lambda s, r: (r, cb + s))
    slab_spec = pl.BlockSpec((rt, S5_SW), lambda s, r: (r, s))
    z = pl.pallas_call(
        _s5_state_kernel, grid=(SG, NC // rt),
        in_specs=[u_spec, pl.BlockSpec((2, 2, S5_L + 1, Q), lambda s, r: (0, 0, 0, s)),
                  pl.BlockSpec((2, 2, S5_CH, Q), lambda s, r: (0, 0, 0, s))], out_specs=slab_spec,
        out_shape=jax.ShapeDtypeStruct((NC, SG * S5_SW), F32),
        scratch_shapes=[pltpu.VMEM((S5_W, S5_SW), BF16)],
        compiler_params=_cp(("parallel", "arbitrary")), name="s5_chunk_state",
    )(p, pw, bb)
    state_spec = pl.BlockSpec((1, nrow, Q), lambda s: (s, 0, 0))
    carry_in, fin = pl.pallas_call(
        functools.partial(_s5_scan_kernel, segments=segments), grid=(SG,),
        in_specs=[pl.BlockSpec((NC, S5_SW), lambda s: (0, s)), pl.BlockSpec((1, 4, Q), lambda s: (s, 0, 0)),
                  state_spec],
        out_specs=[pl.BlockSpec((NC, S5_SW), lambda s: (0, s)), state_spec],
        out_shape=[jax.ShapeDtypeStruct((NC, SG * S5_SW), F32), jax.ShapeDtypeStruct((SG, nrow, Q), F32)],
        compiler_params=_cp(("parallel",)), name="s5_chunk_scan",
    )(z, dec, x0)
    y = pl.pallas_call(
        _s5_out_kernel, grid=(SG, NC // rt),
        in_specs=[u_spec, slab_spec, pl.BlockSpec((2, S5_L, S5_CH, LANE), lambda s, r: (0, 0, 0, s)),
                  pl.BlockSpec((4, S5_L, S5_STATE, LANE), lambda s, r: (0, 0, 0, s))],
        out_specs=pl.BlockSpec((rt * S5_L, LANE), lambda s, r: (r, s)),
        out_shape=jax.ShapeDtypeStruct((N, S5_WIDTH), F32),
        scratch_shapes=[pltpu.VMEM((S5_W, S5_W), BF16), pltpu.VMEM((S5_SW, S5_W), BF16)],
        compiler_params=_cp(("parallel", "arbitrary"), 56), name="s5_chunk_out",
    )(p, carry_in, kk, cw)
    return y, fin


def _s5_post_kernel(y_ref, u_ref, d_ref, w_ref, b_ref, o_ref):
    yg = jax.nn.gelu(y_ref[...] + d_ref[...] * u_ref[...])
    z = jnp.dot(yg.astype(BF16), w_ref[...], preferred_element_type=F32) + b_ref[...]
    o_ref[...] = (yg * jax.nn.sigmoid(z)).astype(o_ref.dtype)


def _s5_post(y, p, d, w, b, tm):
    N = y.shape[0]
    W = S5_WIDTH
    return pl.pallas_call(
        _s5_post_kernel, grid=(N // tm,),
        in_specs=[pl.BlockSpec((tm, W), lambda i: (i, 0)), pl.BlockSpec((tm, W), lambda i: (i, C_S5 // W)),
                  pl.BlockSpec((1, W), lambda i: (0, 0)), pl.BlockSpec((W, W), lambda i: (0, 0)),
                  pl.BlockSpec((1, W), lambda i: (0, 0))],
        out_specs=pl.BlockSpec((tm, W), lambda i: (i, 0)),
        out_shape=jax.ShapeDtypeStruct((N, W), BF16),
        compiler_params=_cp(("parallel",)), name="s5_glu",
    )(y, p, d, w, b)


def _merge_kernel(h_ref, a_ref, b_ref, c_ref, wg0, wg1, wg2, bg0, bg1, bg2, wb_ref, o_ref):
    h = h_ref[...]
    acc = None
    for i, (br, wg, bg) in enumerate(((a_ref, wg0, bg0), (b_ref, wg1, bg1), (c_ref, wg2, bg2))):
        gate = jax.nn.sigmoid(jnp.dot(h, wg[...], preferred_element_type=F32) + bg[...])
        t = gate * jnp.dot(br[...], wb_ref[i], preferred_element_type=F32)
        acc = t if acc is None else acc + t
    o_ref[...] = acc.astype(o_ref.dtype)


def _merge(h, o_da, o_gla, o_s5, w_gate, b_gate, w_branch, tm, tn):
    N, D = h.shape
    W = o_da.shape[1]
    nt = D // tn
    act = pl.BlockSpec((tm, W), lambda j, i: (i, 0))
    in_specs = [pl.BlockSpec((tm, D), lambda j, i: (i, 0)), act, act, act]
    in_specs += [pl.BlockSpec((D, tn), lambda j, i, br=br: (0, br * nt + j)) for br in range(N_BRANCH)]
    in_specs += [pl.BlockSpec((1, tn), lambda j, i, br=br: (0, br * nt + j)) for br in range(N_BRANCH)]
    in_specs += [pl.BlockSpec((N_BRANCH, W, tn), lambda j, i: (0, 0, j))]
    return pl.pallas_call(
        _merge_kernel, grid=(nt, N // tm), in_specs=in_specs,
        out_specs=pl.BlockSpec((tm, tn), lambda j, i: (i, j)),
        out_shape=jax.ShapeDtypeStruct((N, D), BF16),
        compiler_params=_cp(("parallel", "parallel"), 56), name="gated_merge",
    )(h, o_da, o_gla, o_s5, w_gate, w_gate, w_gate, b_gate, b_gate, b_gate, w_branch)


def _wout_kernel(m_ref, w_ref, x_ref, g_ref, o_ref):
    o_ref[...] = x_ref[...] + g_ref[0] * jnp.dot(m_ref[...], w_ref[...], preferred_element_type=F32)


def _wout(merged, w, x, gate, midx, tm, tn):
    N, D = x.shape
    return pl.pallas_call(
        _wout_kernel, grid=(D // tn, N // tm),
        in_specs=[pl.BlockSpec((tm, D), lambda j, i: (i, 0)), pl.BlockSpec((D, tn), lambda j, i: (0, j)),
                  pl.BlockSpec((tm, tn), lambda j, i: (i, j)), pl.BlockSpec((1, 1, tn), lambda j, i: (midx(i), 0, j))],
        out_specs=pl.BlockSpec((tm, tn), lambda j, i: (i, j)),
        out_shape=jax.ShapeDtypeStruct((N, D), F32),
        compiler_params=_cp(("parallel", "parallel")), name="out_proj_residual",
    )(merged, w, x, gate)


def _route_kernel(lg_ref, bias_ref, asg_ref, tile_ref, rank_ref, *, tl):
    E, n_tok = lg_ref.shape
    EPG = EXPERTS_PER_GROUP
    ninf = -jnp.inf
    eio = lax.broadcasted_iota(jnp.int32, (E, tl), 0)
    upper = jnp.where(lax.broadcasted_iota(jnp.int32, (tl, tl), 0) < lax.broadcasted_iota(jnp.int32, (tl, tl), 1),
                      1.0, 0.0).astype(BF16)
    ecol = lax.broadcasted_iota(jnp.int32, (E, 1), 0)

    def select(j, carry):
        ls = pl.ds(pl.multiple_of(j * tl, tl), tl)
        s = jax.nn.sigmoid(lg_ref[:, ls])
        sel = s + bias_ref[...]
        rows = [sel[e:e + 1, :] for e in range(E)]
        srow = [s[e:e + 1, :] for e in range(E)]
        gi = jnp.zeros((1, tl), jnp.int32)
        best = None
        for g in range(N_EXPERT_GROUPS):
            a, b, c, d = rows[g * EPG:(g + 1) * EPG]
            hi1, lo1, hi2, lo2 = jnp.maximum(a, b), jnp.minimum(a, b), jnp.maximum(c, d), jnp.minimum(c, d)
            gs = jnp.maximum(hi1, hi2) + jnp.maximum(jnp.minimum(hi1, hi2), jnp.maximum(lo1, lo2))
            if best is None:
                best = gs
            else:
                upd = gs > best
                gi = jnp.where(upd, g, gi)
                best = jnp.where(upd, gs, best)

        def pick(rs, i):
            out = rs[(N_EXPERT_GROUPS - 1) * EPG + i]
            for g in range(N_EXPERT_GROUPS - 2, -1, -1):
                out = jnp.where(gi == g, rs[g * EPG + i], out)
            return out

        v = [pick(rows, i) for i in range(EPG)]
        sv = [pick(srow, i) for i in range(EPG)]
        i1, b1, s1 = jnp.zeros((1, tl), jnp.int32), v[0], sv[0]
        for i in range(1, EPG):
            upd = v[i] > b1
            i1, b1, s1 = jnp.where(upd, i, i1), jnp.where(upd, v[i], b1), jnp.where(upd, sv[i], s1)
        i2, b2, s2 = jnp.zeros((1, tl), jnp.int32), jnp.full((1, tl), ninf, F32), sv[0]
        for i in range(EPG):
            cand = jnp.where(i1 == i, ninf, v[i])
            upd = cand > b2
            i2, b2, s2 = jnp.where(upd, i, i2), jnp.where(upd, cand, b2), jnp.where(upd, sv[i], s2)
        e0 = gi * EPG + i1
        e1 = gi * EPG + i2
        den = s1 + s2
        asg_ref[2:3, ls] = s1 / den
        asg_ref[3:4, ls] = s2 / den
        asg_ref[4:5, ls] = e0.astype(F32)
        asg_ref[5:6, ls] = e1.astype(F32)
        m = jnp.where((eio == e0) | (eio == e1), 1.0, 0.0)
        rank_ref[:, ls] = jnp.dot(m.astype(BF16), upper, preferred_element_type=F32) + carry
        return carry + jnp.sum(m, axis=1, keepdims=True)

    counts = lax.fori_loop(0, n_tok // tl, select, jnp.zeros((E, 1), F32))
    ptiles = jnp.floor((counts + (MOE_TM - 1)) * (1.0 / MOE_TM))
    run = jnp.zeros((1, 1), F32)
    pstart_t = jnp.zeros((E, 1), F32)
    for e in range(E):
        pstart_t = jnp.where(ecol == e, run, pstart_t)
        run = run + ptiles[e:e + 1, :]
    pstart = pstart_t * MOE_TM

    def place(j, _):
        ls = pl.ds(pl.multiple_of(j * tl, tl), tl)
        dest = rank_ref[:, ls] + pstart
        e0 = asg_ref[4:5, ls].astype(jnp.int32)
        e1 = asg_ref[5:6, ls].astype(jnp.int32)
        asg_ref[0:1, ls] = jnp.sum(jnp.where(eio == e0, dest, 0.0), axis=0, keepdims=True)
        asg_ref[1:2, ls] = jnp.sum(jnp.where(eio == e1, dest, 0.0), axis=0, keepdims=True)
        asg_ref[6:8, ls] = jnp.zeros((2, tl), F32)
        return 0

    lax.fori_loop(0, n_tok // tl, place, 0)

    ntp = tile_ref.shape[1]
    pend_t = pstart_t + ptiles
    ti = lax.broadcasted_iota(jnp.int32, (1, ntp), 1).astype(F32)
    te = jnp.sum(jnp.where(pend_t <= ti, 1.0, 0.0), axis=0, keepdims=True)
    used = ti < run
    last_e = jnp.max(jnp.where(counts > 0.0, ecol.astype(F32), 0.0), axis=0, keepdims=True)
    te = jnp.where(used, jnp.minimum(te, E - 1.0), last_e)
    tile_ref[0:1, :] = te.astype(jnp.int32)
    tile_ref[1:2, :] = jnp.where(used, 1, 0).astype(jnp.int32)
    tile_ref[2:8, :] = jnp.zeros((6, ntp), jnp.int32)


def _route(logits_t, router_bias, ntile):
    E, N = logits_t.shape
    assert MOE_TM & (MOE_TM - 1) == 0
    ntp = -(-ntile // LANE) * LANE
    full = lambda shape: pl.BlockSpec(shape, lambda i: (0, 0))
    return pl.pallas_call(
        functools.partial(_route_kernel, tl=_pick(512, N)), grid=(1,),
        in_specs=[full((E, N)), full((E, 1))],
        out_specs=[full((8, N)), full((8, ntp))],
        out_shape=[jax.ShapeDtypeStruct((8, N), F32), jax.ShapeDtypeStruct((8, ntp), jnp.int32)],
        scratch_shapes=[pltpu.VMEM((E, N), F32)],
        compiler_params=_cp(("arbitrary",)), name="moe_route",
    )(logits_t, router_bias.astype(F32).reshape(E, 1))


def _moe_kernel(te_ref, tv_ref, x_ref, w1_ref, w3_ref, w2_ref, o_ref):
    valid = tv_ref[pl.program_id(0)] != 0

    @pl.when(valid)
    def _():
        x = x_ref[...]
        F = w1_ref.shape[2]
        acc = None
        for f0 in range(0, F, MOE_FC):
            fs = slice(f0, min(f0 + MOE_FC, F))
            a = jnp.dot(x, w1_ref[0, :, fs], preferred_element_type=F32)
            b = jnp.dot(x, w3_ref[0, :, fs], preferred_element_type=F32)
            hid = ((a * jax.nn.sigmoid(a)) * b).astype(BF16)
            t = jnp.dot(hid, w2_ref[0, fs, :], preferred_element_type=F32)
            acc = t if acc is None else acc + t
        o_ref[...] = acc.astype(o_ref.dtype)

    @pl.when(jnp.logical_not(valid))
    def _():
        o_ref[...] = jnp.zeros(o_ref.shape, o_ref.dtype)


def _moe(xin, w1, w3, w2, tile_e, tile_valid):
    cap, D = xin.shape
    F = w1.shape[2]
    TM = MOE_TM
    grid_spec = pltpu.PrefetchScalarGridSpec(
        num_scalar_prefetch=2, grid=(cap // TM,),
        in_specs=[pl.BlockSpec((TM, D), lambda i, te, tv: (i, 0)),
                  pl.BlockSpec((1, D, F), lambda i, te, tv: (te[i], 0, 0)),
                  pl.BlockSpec((1, D, F), lambda i, te, tv: (te[i], 0, 0)),
                  pl.BlockSpec((1, F, D), lambda i, te, tv: (te[i], 0, 0))],
        out_specs=pl.BlockSpec((TM, D), lambda i, te, tv: (i, 0)))
    return pl.pallas_call(
        _moe_kernel, grid_spec=grid_spec,
        out_shape=jax.ShapeDtypeStruct((cap, D), BF16),
        compiler_params=_cp(("arbitrary",), 58), name="moe_experts",
    )(tile_e, tile_valid, xin, w1, w3, w2)


def _combine_kernel(x_ref, y0_ref, y1_ref, w0_ref, w1_ref, g_ref, o_ref):
    o_ref[...] = x_ref[...] + g_ref[0] * (y0_ref[...] * w0_ref[...] + y1_ref[...] * w1_ref[...])


def _combine(x, y0, y1, w0, w1, gate, midx, tm):
    N, D = x.shape
    blk = pl.BlockSpec((tm, D), lambda i: (i, 0))
    col = pl.BlockSpec((tm, 1), lambda i: (i, 0))
    return pl.pallas_call(
        _combine_kernel, grid=(N // tm,),
        in_specs=[blk, blk, blk, col, col, pl.BlockSpec((1, 1, D), lambda i: (midx(i), 0, 0))],
        out_specs=blk, out_shape=jax.ShapeDtypeStruct((N, D), F32),
        compiler_params=_cp(("parallel",)), name="moe_residual",
    )(x, y0, y1, w0, w1, gate)


def _rope_tables(n_tokens):
    rows = n_tokens // GRID_W
    row = jnp.repeat(jnp.arange(rows, dtype=F32), GRID_W)
    col = jnp.tile(jnp.arange(GRID_W, dtype=F32), rows)
    inv = ROPE_BASE ** (-jnp.arange(0, AX_DIM, 2, dtype=F32) / AX_DIM)
    ar = row[:, None] * inv[None, :]
    ac = col[:, None] * inv[None, :]
    ang = jnp.tile(jnp.concatenate([ar, ar, ac, ac], axis=-1), (1, LANE // DA_HEAD))
    first = (jnp.arange(LANE) % (AX_DIM)) < (AX_DIM // 2)
    return jnp.cos(ang), jnp.where(first[None, :], -jnp.sin(ang), jnp.sin(ang))


def kernel(x_prompt, x_sample, cache_k, cache_v, state_gla, state_s5_re, state_s5_im, c, c_ctx, w_mod, b_mod, g_norm, w_in, da_lambda_q, da_lambda_k, da_subln_g, gla_w_alpha, gla_b_alpha, gla_norm_g, s5_a_re, s5_a_im, s5_log_dt, s5_b_re, s5_b_im, s5_c_re, s5_c_im, s5_d, s5_w_glu, s5_b_glu, w_branch, w_gate, b_gate, w_out, router_w, router_bias, moe_w1, moe_w3, moe_w2, g_final):
    B, T, D = x_prompt.shape
    DB, DT, _ = x_sample.shape
    L = w_in.shape[0]
    PAST = cache_k.shape[2]
    NP, NS = B * T, DB * DT
    N = NP + NS
    nseq = B + DB
    assert D == D_MODEL and 1 + DB <= 8 and NP % DT == 0 and T % GLA_ROWS == 0 and DT % GLA_ROWS == 0
    tm = _pick(512, NP, DT)
    midx = functools.partial(_mod_index, tm=tm, n_ctx_tok=NP, dec_seq=DT)

    x = jnp.concatenate([x_prompt.reshape(NP, D), x_sample.reshape(NS, D)], axis=0)
    cv = jnp.zeros((8, D), F32).at[0].set(c_ctx).at[1:1 + DB].set(c)
    mods = _mod(cv, w_mod, b_mod).reshape(L, 8, N_MOD, 1, D)
    zero_mod = jnp.zeros((8, 1, D), F32)

    rope_tabs = _rope_tables(DT)
    ck = cache_k.reshape(DB, L, PAST, DA_HEADS * 2 * DA_HEAD)
    cvv = cache_v.reshape(DB, L, PAST, DA_HEADS * DA_VDIM)
    r = jnp.arange(GLA_ROWS)
    same = (r[:, None] // GLA_CHUNK) == (r[None, :] // GLA_CHUNK)
    tri = jnp.stack([same & (r[None, :] <= r[:, None]), same & (r[None, :] >= r[:, None])]).astype(BF16)
    rw = router_w.astype(F32).T
    rw_hi = rw.astype(BF16)
    rw_lo = (rw - rw_hi.astype(F32)).astype(BF16)
    ntile = N * TOP_K // MOE_TM + N_EXPERTS
    tok2 = jnp.tile(jnp.arange(N, dtype=jnp.int32), TOP_K)
    moe_w2_bf = _cast_bf16(moe_w2)
    tq_ctx = _pick(256, T)
    tq_lat = _pick(256, DT)
    s5_segments = ((0, B, T // S5_L), (NP // S5_L, DB, DT // S5_L))

    g_list, re_list, im_list = [], [], []
    for l in range(L):
        m = mods[l]
        shift1, scale1, gate1, shift2, scale2, gate2 = (m[:, i] for i in range(N_MOD))
        wi = w_in[l]
        w_in_p = jnp.concatenate([wi[:, :C_S5], wi[:, C_S5 + GLA_RANK:], wi[:, C_S5:C_S5 + GLA_RANK],
                                  jnp.zeros((D, P_COLS - C_GA - GLA_RANK), F32)], axis=1).astype(BF16)
        h = _norm(x, g_norm[l, 0][None], scale1, shift1, midx, tm, BF16)
        p = _matmul(h, w_in_p, tm, 1536, F32, "in_proj")

        lam_init = 0.8 - 0.6 * math.exp(-0.3 * l)
        lq, lk = da_lambda_q[l], da_lambda_k[l]
        lam = jnp.exp(jnp.sum(lq[0] * lk[0])) - jnp.exp(jnp.sum(lq[1] * lk[1])) + lam_init
        lam_row = jnp.full((1, LANE), lam, F32)
        g_row = (da_subln_g[l] * (1.0 - lam_init))[None]
        o_da = jnp.concatenate([
            _attention(p, lam_row, g_row, 0, B, T, tq_ctx, DA_HEADS, _pick(ATT_KB, T)),
            _attention_latent(p, lam_row, g_row, NP, DB, DT, tq_lat, rope_tabs, ck, cvv, l)], axis=0)

        wa = jnp.zeros((1, 2, LANE, GLA_KTOT), F32).at[0, :, :GLA_RANK].set(gla_w_alpha[l]).astype(BF16)
        ba = gla_b_alpha[l].reshape(1, 2, 1, GLA_KTOT)
        o_f, o_b, gla_fin = _gla(p, wa, ba, tri, state_gla, l, ((B, T), (DB, DT)))
        o_gla = _gla_post(o_f, o_b, p, gla_norm_g[l][None], tm)

        mats = _s5_matrices(s5_a_re[l], s5_a_im[l], s5_log_dt[l], s5_b_re[l], s5_b_im[l], s5_c_re[l], s5_c_im[l])
        x0 = jnp.stack([state_s5_re[:, l], state_s5_im[:, l]], axis=2)
        x0 = jnp.concatenate([jnp.zeros((B,) + x0.shape[1:], F32), x0], axis=0)
        x0 = x0.reshape(4 * nseq, S5_SG, S5_SW // 4).transpose(1, 0, 2)
        y, s5_fin = _s5(p, mats, x0, s5_segments)
        o_s5 = _s5_post(y, p, s5_d[l][None], s5_w_glu[l].astype(BF16), s5_b_glu[l][None], tm)
        s5_fin = s5_fin.transpose(1, 0, 2).reshape(nseq, 2, 2, S5_GROUPS, S5_STATE)[:B]

        merged = _merge(h, o_da, o_gla, o_s5, w_gate[l].astype(BF16), b_gate[l][None], w_branch[l].astype(BF16),
                        tm, 512)
        x = _wout(merged, w_out[l].astype(BF16), x, gate1, midx, tm, 1024)

        h2, logits_t = _norm(x, g_norm[l, 1][None], scale2, shift2, midx, tm, BF16, router=(rw_hi, rw_lo))
        asg, tiles = _route(logits_t, router_bias, ntile)
        slot = asg[0:2].astype(jnp.int32)
        buf_tok = jnp.zeros((ntile * MOE_TM,), jnp.int32).at[slot.reshape(-1)].set(tok2, unique_indices=True)
        xin = jnp.take(h2, buf_tok, axis=0)
        ye = _moe(xin, moe_w1[l].astype(BF16), moe_w3[l].astype(BF16), moe_w2_bf[l], tiles[0, :ntile],
                  tiles[1, :ntile])
        x = _combine(x, jnp.take(ye, slot[0], axis=0), jnp.take(ye, slot[1], axis=0), asg[2].reshape(N, 1),
                     asg[3].reshape(N, 1), gate2, midx, tm)

        kbuf, vbuf = _kv_store(p, None if l == 0 else (kbuf, vbuf), l, L, B, T)
        g_list.append(gla_fin[:B])
        re_list.append(s5_fin[:, :, 0])
        im_list.append(s5_fin[:, :, 1])

    y_ctx = _norm(x, g_final[None], zero_mod, zero_mod, midx, tm, F32, rows=(0, NP))
    y_lat = _norm(x, g_final[None], zero_mod, zero_mod, midx, tm, F32, rows=(NP, NS))
    return (y_ctx.reshape(B, T, D), y_lat.reshape(DB, DT, D),
            kbuf.reshape(B, L, T, DA_HEADS, 2 * DA_HEAD), vbuf.reshape(B, L, T, DA_HEADS, DA_VDIM),
            jnp.stack(g_list, axis=1),
            jnp.stack(re_list, axis=1), jnp.stack(im_list, axis=1))
```

```python
import functools
import math

import jax
import jax.numpy as jnp
from jax import lax
from jax.experimental import pallas as pl
from jax.experimental.pallas import tpu as pltpu

F32 = jnp.float32
BF16 = jnp.bfloat16

D_MODEL = 2048
GRID_W = 64
EPS = 1e-6
N_MOD = 6
N_BRANCH = 3
DA_HEADS = 8
DA_HEAD = 64
DA_VDIM = 128
ROPE_BASE = 10000.0
AX_DIM = DA_HEAD // 2
GLA_HEADS = 4
GLA_DK = 128
GLA_DV = 256
GLA_KTOT = GLA_HEADS * GLA_DK
GLA_WIDTH = GLA_HEADS * GLA_DV
GLA_RANK = 16
GLA_TAU = 16.0
GLA_CHUNK = 64
S5_WIDTH = 1024
S5_CH = 16
S5_GROUPS = S5_WIDTH // S5_CH
S5_STATE = 64
N_EXPERTS = 16
N_EXPERT_GROUPS = 4
EXPERTS_PER_GROUP = N_EXPERTS // N_EXPERT_GROUPS
TOPK_GROUP = 1
TOP_K = 2
D_FF_EXPERT = 1408

LANE = 128
C_Q, C_K, C_V, C_GQ, C_GK, C_GV, C_GR, C_S5, C_GA = 0, 1024, 2048, 3072, 3584, 4096, 5120, 6144, 7168
P_COLS = 7680
GLA_ROWS = 256
S5_L = 16
S5_SG = S5_WIDTH // LANE
S5_GB = S5_GROUPS // S5_SG
S5_W = S5_L * LANE
S5_SW = 4 * S5_GB * S5_STATE
MOE_TM = 512
MOE_FC = 512
ATT_KB = 512
VMEM_MB = 1024 * 1024


def _cp(sem, vmem_mb=48):
    return pltpu.CompilerParams(dimension_semantics=sem, vmem_limit_bytes=vmem_mb * VMEM_MB)


def _pick(pref, *ns):
    t = pref
    while any(n % t for n in ns):
        t //= 2
    return t


def _mod_index(i, tm, n_ctx_tok, dec_seq):
    t0 = i * tm
    return jnp.where(t0 < n_ctx_tok, 0, 1 + (t0 - n_ctx_tok) // dec_seq)


def _mod_kernel(c_ref, w_ref, b_ref, o_ref):
    c = c_ref[...]
    s = (c * jax.nn.sigmoid(c)).astype(BF16)
    o_ref[0] = jnp.dot(s, w_ref[0].astype(BF16), preferred_element_type=F32) + b_ref[0]


def _mod(cv, w_mod, b_mod):
    L, D, M = w_mod.shape
    tn = 1024
    return pl.pallas_call(
        _mod_kernel,
        grid=(L, M // tn),
        in_specs=[pl.BlockSpec((8, D), lambda l, j: (0, 0)),
                  pl.BlockSpec((1, D, tn), lambda l, j: (l, 0, j)),
                  pl.BlockSpec((1, 1, tn), lambda l, j: (l, 0, j))],
        out_specs=pl.BlockSpec((1, 8, tn), lambda l, j: (l, 0, j)),
        out_shape=jax.ShapeDtypeStruct((L, 8, M), F32),
        compiler_params=_cp(("parallel", "parallel")),
        name="mod_vectors",
    )(cv, w_mod, b_mod.reshape(L, 1, M))


def _cast_kernel(x_ref, o_ref):
    o_ref[...] = x_ref[...].astype(o_ref.dtype)


def _cast_bf16(w):
    lead, (R, C) = w.shape[:-2], w.shape[-2:]
    n = math.prod(lead)
    out = pl.pallas_call(
        _cast_kernel, grid=(n,),
        in_specs=[pl.BlockSpec((1, R, C), lambda i: (i, 0, 0))], out_specs=pl.BlockSpec((1, R, C), lambda i: (i, 0, 0)),
        out_shape=jax.ShapeDtypeStruct((n, R, C), BF16),
        compiler_params=_cp(("parallel",)), name="cast_bf16",
    )(w.reshape(n, R, C))
    return out.reshape(lead + (R, C))


def _norm_core(x_ref, g_ref, sc_ref, sh_ref):
    x = x_ref[...]
    y = x * lax.rsqrt(jnp.mean(x * x, axis=-1, keepdims=True) + EPS)
    return (y * g_ref[...]) * (1.0 + sc_ref[0]) + sh_ref[0]


def _norm_kernel(x_ref, g_ref, sc_ref, sh_ref, o_ref):
    o_ref[...] = _norm_core(x_ref, g_ref, sc_ref, sh_ref).astype(o_ref.dtype)


def _norm_router_kernel(x_ref, g_ref, sc_ref, sh_ref, rh_ref, rl_ref, o_ref, lg_ref):
    h = _norm_core(x_ref, g_ref, sc_ref, sh_ref)
    hh = h.astype(BF16)
    hl = (h - hh.astype(F32)).astype(BF16)
    o_ref[...] = hh
    nt = (((1,), (1,)), ((), ()))
    lg_ref[...] = (lax.dot_general(rh_ref[...], hh, nt, preferred_element_type=F32)
                   + lax.dot_general(rh_ref[...], hl, nt, preferred_element_type=F32)
                   + lax.dot_general(rl_ref[...], hh, nt, preferred_element_type=F32))


def _norm(x, g, sc, sh, midx, tm, out_dtype, router=None, rows=None):
    N, D = x.shape
    r0, n = (0, N) if rows is None else rows
    t0 = r0 // tm
    in_specs = [pl.BlockSpec((tm, D), lambda i: (t0 + i, 0)),
                pl.BlockSpec((1, D), lambda i: (0, 0)),
                pl.BlockSpec((1, 1, D), lambda i: (midx(t0 + i), 0, 0)),
                pl.BlockSpec((1, 1, D), lambda i: (midx(t0 + i), 0, 0))]
    if router is None:
        return pl.pallas_call(
            _norm_kernel, grid=(n // tm,), in_specs=in_specs,
            out_specs=pl.BlockSpec((tm, D), lambda i: (i, 0)),
            out_shape=jax.ShapeDtypeStruct((n, D), out_dtype),
            compiler_params=_cp(("parallel",)), name="norm_mod",
        )(x, g, sc, sh)
    assert rows is None
    rh, rl = router
    E = rh.shape[0]
    in_specs += [pl.BlockSpec((E, D), lambda i: (0, 0)), pl.BlockSpec((E, D), lambda i: (0, 0))]
    return pl.pallas_call(
        _norm_router_kernel, grid=(N // tm,), in_specs=in_specs,
        out_specs=[pl.BlockSpec((tm, D), lambda i: (i, 0)), pl.BlockSpec((E, tm), lambda i: (0, i))],
        out_shape=[jax.ShapeDtypeStruct((N, D), BF16), jax.ShapeDtypeStruct((E, N), F32)],
        compiler_params=_cp(("parallel",)), name="norm_mod_router",
    )(x, g, sc, sh, rh, rl)


def _mm_kernel(a_ref, w_ref, o_ref):
    o_ref[...] = jnp.dot(a_ref[...], w_ref[...], preferred_element_type=F32).astype(o_ref.dtype)


def _matmul(a, w, tm, tn, out_dtype, name):
    M, K = a.shape
    Nc = w.shape[1]
    return pl.pallas_call(
        _mm_kernel, grid=(Nc // tn, M // tm),
        in_specs=[pl.BlockSpec((tm, K), lambda j, i: (i, 0)), pl.BlockSpec((K, tn), lambda j, i: (0, j))],
        out_specs=pl.BlockSpec((tm, tn), lambda j, i: (i, j)),
        out_shape=jax.ShapeDtypeStruct((M, Nc), out_dtype),
        compiler_params=_cp(("parallel", "parallel")), name=name,
    )(a, w)


def _kv_store_kernel(k_ref, v_ref, *refs):
    ko_ref, vo_ref = refs[-2:]
    ko_ref[...] = k_ref[...]
    vo_ref[...] = v_ref[...]


def _kv_store(p, bufs, layer, n_layers, B, T):
    W = DA_HEADS * DA_VDIM
    in_specs = [pl.BlockSpec((T, W), lambda b: (b, C_K // W)), pl.BlockSpec((T, W), lambda b: (b, C_V // W))]
    args = [p, p]
    aliases = {}
    if bufs is not None:
        in_specs += [pl.BlockSpec(memory_space=pl.ANY), pl.BlockSpec(memory_space=pl.ANY)]
        args += list(bufs)
        aliases = {2: 0, 3: 1}
    out_spec = pl.BlockSpec((None, None, T, W), lambda b: (b, layer, 0, 0))
    shape = jax.ShapeDtypeStruct((B, n_layers, T, W), F32)
    return pl.pallas_call(
        _kv_store_kernel, grid=(B,), in_specs=in_specs, out_specs=[out_spec, out_spec], out_shape=[shape, shape],
        input_output_aliases=aliases, compiler_params=_cp(("parallel",)), name="kv_store",
    )(*args)


def _rope(x, cos, sin_signed):
    lane = lax.broadcasted_iota(jnp.int32, (1, LANE), 1)
    first = (lane % (2 * (AX_DIM // 2))) < (AX_DIM // 2)
    xr = jnp.where(first, pltpu.roll(x, LANE - AX_DIM // 2, 1), pltpu.roll(x, AX_DIM // 2, 1))
    return x * cos + xr * sin_signed


def _attn_kernel(*refs, rope, ctx, hs, kb):
    refs = list(refs)
    q_ref, k_ref, v_ref = refs[:3]
    refs = refs[3:]
    if ctx:
        ck_ref, cv_ref = refs[:2]
        refs = refs[2:]
    if rope:
        cq_ref, sq_ref, ckk_ref, skk_ref = refs[:4]
        refs = refs[4:]
    lam_ref, g_ref, o_ref, kb_ref, vb_ref = refs[:5]
    if ctx:
        ckb_ref, cvb_ref = refs[5:]

    hsl = [slice(h * LANE, (h + 1) * LANE) for h in range(hs)]

    @pl.when(pl.program_id(2) == 0)
    def _():
        for h in range(hs):
            k = k_ref[:, hsl[h]]
            if rope:
                k = _rope(k, ckk_ref[...], skk_ref[...])
            kb_ref[:, hsl[h]] = k.astype(BF16)
        vb_ref[...] = v_ref[...].astype(BF16)
        if ctx:
            ckb_ref[...] = ck_ref[...].astype(BF16)
            cvb_ref[...] = cv_ref[...].astype(BF16)

    tq = q_ref.shape[0]
    n_new = kb_ref.shape[0]
    lane = lax.broadcasted_iota(jnp.int32, (1, LANE), 1)
    lo = lane < DA_HEAD
    nt = (((1,), (1,)), ((), ()))
    qqs = []
    for h in range(hs):
        q = q_ref[:, hsl[h]]
        if rope:
            q = _rope(q, cq_ref[...], sq_ref[...])
        q = q * (DA_HEAD ** -0.5 * math.log2(math.e))
        qqs.append(jnp.concatenate([jnp.where(lo, q, 0.0), jnp.where(lo, 0.0, q)], axis=0).astype(BF16))

    def block(qq, kblk, vblk, carry):
        m, l, acc = carry
        s = lax.dot_general(qq, kblk, nt, preferred_element_type=F32)
        mn = jnp.maximum(m, jnp.max(s, axis=-1, keepdims=True))
        e = jnp.exp2(s - mn)
        r = jnp.exp2(m - mn)
        l = r * l + jnp.sum(e, axis=-1, keepdims=True)
        acc = r * acc + jnp.dot(e.astype(BF16), vblk, preferred_element_type=F32)
        return mn, l, acc

    def all_heads(kref, vref, rows, carries):
        return tuple(block(qqs[h], kref[rows, hsl[h]], vref[rows, hsl[h]], carries[h]) for h in range(hs))

    carries = tuple((jnp.full((2 * tq, 1), -jnp.inf, F32), jnp.zeros((2 * tq, 1), F32),
                     jnp.zeros((2 * tq, LANE), F32)) for _ in range(hs))
    if ctx:
        for c in range(ckb_ref.shape[0] // kb):
            carries = all_heads(ckb_ref, cvb_ref, slice(c * kb, (c + 1) * kb), carries)
    if n_new // kb == 1:
        carries = all_heads(kb_ref, vb_ref, slice(None), carries)
    else:
        def body(j, carries):
            return all_heads(kb_ref, vb_ref, pl.ds(pl.multiple_of(j * kb, kb), kb), carries)
        carries = lax.fori_loop(0, n_new // kb, body, carries)
    for h in range(hs):
        _, l, acc = carries[h]
        o = acc[:tq] * (1.0 / l[:tq]) - acc[tq:] * (lam_ref[:, :1] / l[tq:])
        y = o * lax.rsqrt(jnp.mean(o * o, axis=-1, keepdims=True) + EPS)
        o_ref[:, hsl[h]] = (y * g_ref[...]).astype(o_ref.dtype)


def _attention(p, lam_row, g_row, row0, B, T, tq, hs, kb, rope_tabs=None, ctx=None):
    nq = T // tq
    rb0 = row0 // tq
    kb0 = row0 // T
    W = hs * LANE
    hq, hk, hv = C_Q // W, C_K // W, C_V // W
    assert T % kb == 0
    in_specs = [pl.BlockSpec((tq, W), lambda b, h, i: (rb0 + b * nq + i, hq + h)),
                pl.BlockSpec((T, W), lambda b, h, i: (kb0 + b, hk + h)),
                pl.BlockSpec((T, W), lambda b, h, i: (kb0 + b, hv + h))]
    args = [p, p, p]
    scratch = [pltpu.VMEM((T, W), BF16), pltpu.VMEM((T, W), BF16)]
    if ctx is not None:
        ck, cv, layer = ctx
        past = ck.shape[2]
        assert past % kb == 0
        in_specs += [pl.BlockSpec((None, None, past, W), lambda b, h, i: (b, layer, 0, h)),
                     pl.BlockSpec((None, None, past, W), lambda b, h, i: (b, layer, 0, h))]
        args += [ck, cv]
        scratch += [pltpu.VMEM((past, W), BF16), pltpu.VMEM((past, W), BF16)]
    if rope_tabs is not None:
        cos, sin = rope_tabs
        in_specs += [pl.BlockSpec((tq, LANE), lambda b, h, i: (i, 0)), pl.BlockSpec((tq, LANE), lambda b, h, i: (i, 0)),
                     pl.BlockSpec((T, LANE), lambda b, h, i: (0, 0)), pl.BlockSpec((T, LANE), lambda b, h, i: (0, 0))]
        args += [cos, sin, cos, sin]
    in_specs += [pl.BlockSpec((1, LANE), lambda b, h, i: (0, 0)), pl.BlockSpec((1, LANE), lambda b, h, i: (0, 0))]
    args += [lam_row, g_row]
    return pl.pallas_call(
        functools.partial(_attn_kernel, rope=rope_tabs is not None, ctx=ctx is not None, hs=hs, kb=kb),
        grid=(B, DA_HEADS // hs, nq),
        in_specs=in_specs,
        out_specs=pl.BlockSpec((tq, W), lambda b, h, i: (b * nq + i, h)),
        out_shape=jax.ShapeDtypeStruct((B * T, DA_HEADS * DA_VDIM), BF16),
        scratch_shapes=scratch,
        compiler_params=_cp(("parallel", "parallel", "arbitrary"), 56),
        name="diff_attention_latent" if ctx is not None else "diff_attention_context",
    )(*args)


def _attn_latent_kernel(q_ref, k_ref, v_ref, ck_ref, cv_ref, cq_ref, sq_ref, ckk_ref, skk_ref, lam_ref, g_ref, o_ref,
                        kb_ref, vb_ref):
    @pl.when(pl.program_id(2) == 0)
    def _():
        kb_ref[...] = _rope(k_ref[...], ckk_ref[...], skk_ref[...]).astype(BF16)
        vb_ref[...] = v_ref[...].astype(BF16)

    q = _rope(q_ref[...], cq_ref[...], sq_ref[...]) * (DA_HEAD ** -0.5)
    lane = lax.broadcasted_iota(jnp.int32, (1, LANE), 1)
    lo = lane < DA_HEAD
    q0 = jnp.where(lo, q, 0.0).astype(BF16)
    q1 = jnp.where(lo, 0.0, q).astype(BF16)
    nt = (((1,), (1,)), ((), ()))
    kb = kb_ref[...]
    ckb = ck_ref[...].astype(BF16)
    s0 = lax.dot_general(q0, kb, nt, preferred_element_type=F32)
    s1 = lax.dot_general(q1, kb, nt, preferred_element_type=F32)
    c0 = lax.dot_general(q0, ckb, nt, preferred_element_type=F32)
    c1 = lax.dot_general(q1, ckb, nt, preferred_element_type=F32)
    m0 = jnp.maximum(jnp.max(s0, axis=-1, keepdims=True), jnp.max(c0, axis=-1, keepdims=True))
    m1 = jnp.maximum(jnp.max(s1, axis=-1, keepdims=True), jnp.max(c1, axis=-1, keepdims=True))
    e0 = jnp.exp(s0 - m0)
    e1 = jnp.exp(s1 - m1)
    ec0 = jnp.exp(c0 - m0)
    ec1 = jnp.exp(c1 - m1)
    l0 = jnp.sum(e0, axis=-1, keepdims=True) + jnp.sum(ec0, axis=-1, keepdims=True)
    l1 = jnp.sum(e1, axis=-1, keepdims=True) + jnp.sum(ec1, axis=-1, keepdims=True)
    a0 = 1.0 / l0
    a1 = lam_ref[:, :1] / l1
    o = jnp.dot((e0 * a0 - e1 * a1).astype(BF16), vb_ref[...], preferred_element_type=F32)
    o = o + jnp.dot((ec0 * a0 - ec1 * a1).astype(BF16), cv_ref[...].astype(BF16), preferred_element_type=F32)
    y = o * lax.rsqrt(jnp.mean(o * o, axis=-1, keepdims=True) + EPS)
    o_ref[...] = (y * g_ref[...]).astype(o_ref.dtype)


def _attention_latent(p, lam_row, g_row, row0, B, T, tq, rope_tabs, ck, cv, layer):
    nq = T // tq
    rb0 = row0 // tq
    kb0 = row0 // T
    hq, hk, hv = C_Q // LANE, C_K // LANE, C_V // LANE
    past = ck.shape[2]
    cos, sin = rope_tabs
    in_specs = [pl.BlockSpec((tq, LANE), lambda b, h, i: (rb0 + b * nq + i, hq + h)),
                pl.BlockSpec((T, LANE), lambda b, h, i: (kb0 + b, hk + h)),
                pl.BlockSpec((T, LANE), lambda b, h, i: (kb0 + b, hv + h)),
                pl.BlockSpec((None, None, past, LANE), lambda b, h, i: (b, layer, 0, h)),
                pl.BlockSpec((None, None, past, LANE), lambda b, h, i: (b, layer, 0, h)),
                pl.BlockSpec((tq, LANE), lambda b, h, i: (i, 0)), pl.BlockSpec((tq, LANE), lambda b, h, i: (i, 0)),
                pl.BlockSpec((T, LANE), lambda b, h, i: (0, 0)), pl.BlockSpec((T, LANE), lambda b, h, i: (0, 0)),
                pl.BlockSpec((1, LANE), lambda b, h, i: (0, 0)), pl.BlockSpec((1, LANE), lambda b, h, i: (0, 0))]
    return pl.pallas_call(
        _attn_latent_kernel, grid=(B, DA_HEADS, nq), in_specs=in_specs,
        out_specs=pl.BlockSpec((tq, LANE), lambda b, h, i: (b * nq + i, h)),
        out_shape=jax.ShapeDtypeStruct((B * T, DA_HEADS * DA_VDIM), BF16),
        scratch_shapes=[pltpu.VMEM((T, LANE), BF16), pltpu.VMEM((T, LANE), BF16)],
        compiler_params=_cp(("parallel", "parallel", "arbitrary"), 56),
        name="diff_attention_latent",
    )(p, p, p, ck, cv, cos, sin, cos, sin, lam_row, g_row)


def _log_sigmoid(z):
    return jnp.minimum(z, 0.0) - jnp.log1p(jnp.exp(-jnp.abs(z)))


def _gla_direction(d, q_ref, k_ref, v_ref, ga_ref, wa_ref, ba_ref, tri_ref, o_ref, st_ref):
    C = GLA_CHUNK
    nsub = GLA_ROWS // C
    z = jnp.dot(ga_ref[...].astype(BF16), wa_ref[0, d], preferred_element_type=F32) + ba_ref[0, d]
    la = _log_sigmoid(z) / GLA_TAU
    hi = la.astype(BF16)
    r1 = la - hi.astype(F32)
    mid = r1.astype(BF16)
    low = (r1 - mid.astype(F32)).astype(BF16)
    tri = tri_ref[d]
    bsum = (jnp.dot(tri, hi, preferred_element_type=F32) + jnp.dot(tri, mid, preferred_element_type=F32)
            + jnp.dot(tri, low, preferred_element_type=F32))
    row = lax.broadcasted_iota(jnp.int32, (C, C), 0)
    col = lax.broadcasted_iota(jnp.int32, (C, C), 1)
    mask = (row >= col) if d == 0 else (row <= col)
    nt = (((1,), (1,)), ((), ()))
    tn = (((0,), (0,)), ((), ()))
    subs = range(nsub) if d == 0 else range(nsub - 1, -1, -1)
    for h in range(GLA_HEADS):
        hk = slice(h * GLA_DK, (h + 1) * GLA_DK)
        hv = slice(h * GLA_DV, (h + 1) * GLA_DV)
        st = st_ref[d, h]
        for s in subs:
            rs = slice(s * C, (s + 1) * C)
            end = s * C + C - 1 if d == 0 else s * C
            bb = bsum[rs, hk]
            bl = bsum[end:end + 1, hk]
            q = q_ref[rs, hk] * (GLA_DK ** -0.5)
            k = k_ref[rs, hk]
            v = v_ref[rs, hv].astype(BF16)
            qt = (q * jnp.exp(bb)).astype(BF16)
            kt = (k * jnp.exp(-bb)).astype(BF16)
            kend = (k * jnp.exp(bl - bb)).astype(BF16)
            att = jnp.where(mask, lax.dot_general(qt, kt, nt, preferred_element_type=F32), 0.0).astype(BF16)
            o = jnp.dot(att, v, preferred_element_type=F32)
            o = o + lax.dot_general(qt, st.astype(BF16), nt, preferred_element_type=F32)
            o_ref[rs, hv] = o
            st = st * jnp.exp(bl) + lax.dot_general(v, kend, tn, preferred_element_type=F32)
        st_ref[d, h] = st


def _gla_kernel(qf, kf, vf, gaf, qb, kb, vb, gab, wa_ref, ba_ref, tri_ref, s0_ref, of_ref, ob_ref, fin_ref, st_ref,
                *, seq_pos):
    first, last, stateless = seq_pos(pl.program_id(0))

    @pl.when(first & stateless)
    def _():
        st_ref[...] = jnp.zeros(st_ref.shape, F32)

    @pl.when(first & jnp.logical_not(stateless))
    def _():
        for d in range(2):
            for h in range(GLA_HEADS):
                st_ref[d, h] = s0_ref[d, h].T

    _gla_direction(0, qf, kf, vf, gaf, wa_ref, ba_ref, tri_ref, of_ref, st_ref)
    _gla_direction(1, qb, kb, vb, gab, wa_ref, ba_ref, tri_ref, ob_ref, st_ref)

    @pl.when(last & stateless)
    def _():
        for d in range(2):
            for h in range(GLA_HEADS):
                fin_ref[d, h] = st_ref[d, h].T


def _gla(p, wa, ba, tri, state, layer, seqs):
    N = p.shape[0]
    R = GLA_ROWS
    (n1, t1), (n2, t2) = seqs
    nb1, nb2 = t1 // R, t2 // R
    steps1 = n1 * nb1

    def locate(s):
        in1 = s < steps1
        s2 = s - steps1
        nb = jnp.where(in1, nb1, nb2)
        seq = jnp.where(in1, s // nb1, n1 + s2 // nb2)
        n = jnp.where(in1, s % nb1, s2 % nb2)
        base = jnp.where(in1, (s // nb1) * nb1, steps1 + (s2 // nb2) * nb2)
        return seq, n, nb, base

    def fwd_blk(s):
        _, n, _, base = locate(s)
        return base + n

    def bwd_blk(s):
        _, n, nb, base = locate(s)
        return base + nb - 1 - n

    def seq_pos(s):
        _, n, nb, _ = locate(s)
        return n == 0, n == nb - 1, s < steps1

    def specs(blk):
        return [pl.BlockSpec((R, GLA_KTOT), lambda s: (blk(s), C_GQ // GLA_KTOT)),
                pl.BlockSpec((R, GLA_KTOT), lambda s: (blk(s), C_GK // GLA_KTOT)),
                pl.BlockSpec((R, GLA_WIDTH), lambda s: (blk(s), C_GV // GLA_WIDTH)),
                pl.BlockSpec((R, LANE), lambda s: (blk(s), C_GA // LANE))]

    s0_spec = pl.BlockSpec((None, None, 2, GLA_HEADS, GLA_DK, GLA_DV),
                           lambda s: (jnp.maximum(locate(s)[0] - n1, 0), layer, 0, 0, 0, 0))
    fin_spec = pl.BlockSpec((None, 2, GLA_HEADS, GLA_DK, GLA_DV),
                            lambda s: (jnp.minimum(locate(s)[0], n1 - 1), 0, 0, 0, 0))
    return pl.pallas_call(
        functools.partial(_gla_kernel, seq_pos=seq_pos),
        grid=(N // R,),
        in_specs=specs(fwd_blk) + specs(bwd_blk) + [
            pl.BlockSpec((1, 2, LANE, GLA_KTOT), lambda s: (0, 0, 0, 0)),
            pl.BlockSpec((1, 2, 1, GLA_KTOT), lambda s: (0, 0, 0, 0)),
            pl.BlockSpec((2, R, R), lambda s: (0, 0, 0)),
            s0_spec],
        out_specs=[pl.BlockSpec((R, GLA_WIDTH), lambda s: (fwd_blk(s), 0)),
                   pl.BlockSpec((R, GLA_WIDTH), lambda s: (bwd_blk(s), 0)),
                   fin_spec],
        out_shape=[jax.ShapeDtypeStruct((N, GLA_WIDTH), F32), jax.ShapeDtypeStruct((N, GLA_WIDTH), F32),
                   jax.ShapeDtypeStruct((n1, 2, GLA_HEADS, GLA_DK, GLA_DV), F32)],
        scratch_shapes=[pltpu.VMEM((2, GLA_HEADS, GLA_DV, GLA_DK), F32)],
        compiler_params=_cp(("arbitrary",)),
        name="gla_bidir",
    )(p, p, p, p, p, p, p, p, wa, ba, tri, state)


def _gla_post_kernel(of_ref, ob_ref, r_ref, g_ref, o_ref):
    for h in range(GLA_HEADS):
        hv = slice(h * GLA_DV, (h + 1) * GLA_DV)
        o = of_ref[:, hv] + ob_ref[:, hv]
        y = o * lax.rsqrt(jnp.mean(o * o, axis=-1, keepdims=True) + EPS) * g_ref[...]
        r = r_ref[:, hv]
        o_ref[:, hv] = (y * (r * jax.nn.sigmoid(r))).astype(o_ref.dtype)


def _gla_post(of, ob, p, g, tm):
    N = of.shape[0]
    return pl.pallas_call(
        _gla_post_kernel, grid=(N // tm,),
        in_specs=[pl.BlockSpec((tm, GLA_WIDTH), lambda i: (i, 0)), pl.BlockSpec((tm, GLA_WIDTH), lambda i: (i, 0)),
                  pl.BlockSpec((tm, GLA_WIDTH), lambda i: (i, C_GR // GLA_WIDTH)),
                  pl.BlockSpec((1, GLA_DV), lambda i: (0, 0))],
        out_specs=pl.BlockSpec((tm, GLA_WIDTH), lambda i: (i, 0)),
        out_shape=jax.ShapeDtypeStruct((N, GLA_WIDTH), BF16),
        compiler_params=_cp(("parallel",)), name="gla_norm_gate",
    )(of, ob, p, g)


def _s5_matrices(a_re, a_im, log_dt, b_re, b_im, c_re, c_im):
    hp = lax.Precision.HIGHEST
    L, G, P, CH = S5_L, S5_GROUPS, S5_STATE, S5_CH
    taus = jnp.arange(L + 1, dtype=F32)
    ii = jnp.arange(L)
    kerns, powers, bbars, carry, decay = [], [], [], [], []
    c_re_t, c_im_t = c_re.transpose(2, 0, 1), c_im.transpose(2, 0, 1)
    for d in range(2):
        dt = jnp.exp(log_dt[d])[:, None]
        ld_re, ld_im = a_re[d] * dt, a_im[d] * dt
        mag = jnp.exp(ld_re)
        lb_re, lb_im = mag * jnp.cos(ld_im), mag * jnp.sin(ld_im)
        den = a_re[d] * a_re[d] + a_im[d] * a_im[d]
        q_re = ((lb_re - 1.0) * a_re[d] + lb_im * a_im[d]) / den
        q_im = (lb_im * a_re[d] - (lb_re - 1.0) * a_im[d]) / den
        bb_re = q_re[..., None] * b_re - q_im[..., None] * b_im
        bb_im = q_re[..., None] * b_im + q_im[..., None] * b_re
        pmag = jnp.exp(ld_re[None] * taus[:, None, None])
        pw_re = pmag * jnp.cos(ld_im[None] * taus[:, None, None])
        pw_im = pmag * jnp.sin(ld_im[None] * taus[:, None, None])
        cp_re = c_re[None] * pw_re[:L, :, None, :] - c_im[None] * pw_im[:L, :, None, :]
        cp_im = c_re[None] * pw_im[:L, :, None, :] + c_im[None] * pw_re[:L, :, None, :]
        kern = (jnp.einsum('tgop,gpc->tcgo', cp_re, bb_re, precision=hp)
                - jnp.einsum('tgop,gpc->tcgo', cp_im, bb_im, precision=hp)).reshape(L, CH, G * CH)
        kerns.append(kern)
        powers.append(jnp.stack([pw_re.reshape(L + 1, G * P), pw_im.reshape(L + 1, G * P)]))
        bbars.append(jnp.stack([bb_re.transpose(2, 0, 1).reshape(CH, G * P), bb_im.transpose(2, 0, 1).reshape(CH, G * P)]))
        e = (ii + 1) if d == 0 else (L - ii)
        pt_re, pt_im = pw_re[e].transpose(0, 2, 1)[..., None], pw_im[e].transpose(0, 2, 1)[..., None]
        cw_re = c_re_t[None] * pt_re - c_im_t[None] * pt_im
        cw_im = c_re_t[None] * pt_im + c_im_t[None] * pt_re
        carry += [cw_re.reshape(L, P, G * CH), (-cw_im).reshape(L, P, G * CH)]
        decay.append((pw_re[L], pw_im[L]))
    kk = jnp.stack([kerns[0].at[0].add(kerns[1][0]), kerns[1]]).astype(BF16)
    pw = jnp.stack(powers)
    bb = jnp.stack(bbars)
    cw = jnp.stack(carry).astype(BF16)
    dec = jnp.stack([jnp.stack([x[0], x[1]]) for x in decay])
    dec = dec.reshape(4, S5_SG, S5_SW // 4).transpose(1, 0, 2)
    return kk, pw, bb, cw, dec


def _s5_slab(u_ref, rt):
    return jnp.concatenate([u_ref[pl.ds(j, rt, stride=S5_L), :] for j in range(S5_L)], axis=1).astype(BF16)


def _s5_group_masks(width, per_group):
    grp = (lax.broadcasted_iota(jnp.int32, (1, width), 1) // per_group) % S5_GB
    return [grp == a for a in range(S5_GB)]


def _s5_state_kernel(u_ref, pw_ref, bb_ref, z_ref, w_ref):
    @pl.when(pl.program_id(1) == 0)
    def _():
        Q = S5_SW // 4
        masks = _s5_group_masks(Q, S5_STATE)
        for d in range(2):
            b_re, b_im = bb_ref[d, 0], bb_ref[d, 1]
            for j in range(S5_L):
                e = S5_L - 1 - j if d == 0 else j
                p_re, p_im = pw_ref[d, 0, e:e + 1, :], pw_ref[d, 1, e:e + 1, :]
                parts = ((p_re * b_re - p_im * b_im).astype(BF16), (p_re * b_im + p_im * b_re).astype(BF16))
                for ri in range(2):
                    q = 2 * d + ri
                    for a in range(S5_GB):
                        r0 = j * LANE + a * S5_CH
                        w_ref[r0:r0 + S5_CH, q * Q:(q + 1) * Q] = jnp.where(masks[a], parts[ri],
                                                                            jnp.zeros_like(parts[ri]))

    z_ref[...] = jnp.dot(_s5_slab(u_ref, z_ref.shape[0]), w_ref[...], preferred_element_type=F32)


def _s5_out_kernel(u_ref, p_ref, k_ref, c_ref, y_ref, wi_ref, wc_ref):
    @pl.when(pl.program_id(1) == 0)
    def _():
        Q = S5_SW // 4
        masks = _s5_group_masks(LANE, S5_CH)
        for i in range(S5_L):
            cols = slice(i * LANE, (i + 1) * LANE)
            for j in range(S5_L):
                blk = k_ref[0 if i >= j else 1, abs(i - j)]
                for a in range(S5_GB):
                    r0 = j * LANE + a * S5_CH
                    wi_ref[r0:r0 + S5_CH, cols] = jnp.where(masks[a], blk, jnp.zeros_like(blk))
            for q in range(4):
                blk = c_ref[q, i]
                for b in range(S5_GB):
                    r0 = q * Q + b * S5_STATE
                    wc_ref[r0:r0 + S5_STATE, cols] = jnp.where(masks[b], blk, jnp.zeros_like(blk))

    rt = p_ref.shape[0]
    y = (jnp.dot(_s5_slab(u_ref, rt), wi_ref[...], preferred_element_type=F32)
         + jnp.dot(p_ref[...].astype(BF16), wc_ref[...], preferred_element_type=F32))
    for i in range(S5_L):
        y_ref[pl.ds(i, rt, stride=S5_L), :] = y[:, i * LANE:(i + 1) * LANE]


def _s5_scan_kernel(z_ref, dec_ref, x0_ref, p_ref, fin_ref, *, segments):
    Q = S5_SW // 4
    dec = [(dec_ref[0, 2 * d:2 * d + 1, :], dec_ref[0, 2 * d + 1:2 * d + 2, :]) for d in range(2)]
    seq_base = 0
    for (row0, n_seq, n_chunk) in segments:
        def seq_loop(sq, carry, row0=row0, n_chunk=n_chunk, seq_base=seq_base):
            r0 = row0 + sq * n_chunk
            sid = seq_base + sq
            x_init = tuple((x0_ref[0, pl.ds(4 * sid + 2 * d, 1), :], x0_ref[0, pl.ds(4 * sid + 2 * d + 1, 1), :])
                           for d in range(2))

            def step(n, xs):
                out = []
                for d in range(2):
                    xr, xi = xs[d]
                    ar, ai = dec[d]
                    r = r0 + (n if d == 0 else n_chunk - 1 - n)
                    cr = slice(2 * d * Q, (2 * d + 1) * Q)
                    ci = slice((2 * d + 1) * Q, (2 * d + 2) * Q)
                    zr = z_ref[pl.ds(r, 1), cr]
                    zi = z_ref[pl.ds(r, 1), ci]
                    p_ref[pl.ds(r, 1), cr] = xr
                    p_ref[pl.ds(r, 1), ci] = xi
                    out.append((ar * xr - ai * xi + zr, ar * xi + ai * xr + zi))
                return tuple(out)

            xs = lax.fori_loop(0, n_chunk, step, x_init)
            for d in range(2):
                fin_ref[0, pl.ds(4 * sid + 2 * d, 1), :] = xs[d][0]
                fin_ref[0, pl.ds(4 * sid + 2 * d + 1, 1), :] = xs[d][1]
            return carry

        lax.fori_loop(0, n_seq, seq_loop, 0)
        seq_base += n_seq


def _s5(p, weights, x0, segments):
    kk, pw, bb, cw, dec = weights
    N = p.shape[0]
    NC = N // S5_L
    SG = S5_SG
    Q = S5_SW // 4
    rt = _pick(256, NC)
    nrow = x0.shape[1]
    cb = C_S5 // LANE
    u_spec = pl.BlockSpec((rt * S5_L, LANE), lambda s, r: (r, cb + s))
    slab_spec = pl.BlockSpec((rt, S5_SW), lambda s, r: (r, s))
    z = pl.pallas_call(
        _s5_state_kernel, grid=(SG, NC // rt),
        in_specs=[u_spec, pl.BlockSpec((2, 2, S5_L + 1, Q), lambda s, r: (0, 0, 0, s)),
                  pl.BlockSpec((2, 2, S5_CH, Q), lambda s, r: (0, 0, 0, s))], out_specs=slab_spec,
        out_shape=jax.ShapeDtypeStruct((NC, SG * S5_SW), F32),
        scratch_shapes=[pltpu.VMEM((S5_W, S5_SW), BF16)],
        compiler_params=_cp(("parallel", "arbitrary")), name="s5_chunk_state",
    )(p, pw, bb)
    state_spec = pl.BlockSpec((1, nrow, Q), lambda s: (s, 0, 0))
    carry_in, fin = pl.pallas_call(
        functools.partial(_s5_scan_kernel, segments=segments), grid=(SG,),
        in_specs=[pl.BlockSpec((NC, S5_SW), lambda s: (0, s)), pl.BlockSpec((1, 4, Q), lambda s: (s, 0, 0)),
                  state_spec],
        out_specs=[pl.BlockSpec((NC, S5_SW), lambda s: (0, s)), state_spec],
        out_shape=[jax.ShapeDtypeStruct((NC, SG * S5_SW), F32), jax.ShapeDtypeStruct((SG, nrow, Q), F32)],
        compiler_params=_cp(("parallel",)), name="s5_chunk_scan",
    )(z, dec, x0)
    y = pl.pallas_call(
        _s5_out_kernel, grid=(SG, NC // rt),
        in_specs=[u_spec, slab_spec, pl.BlockSpec((2, S5_L, S5_CH, LANE), lambda s, r: (0, 0, 0, s)),
                  pl.BlockSpec((4, S5_L, S5_STATE, LANE), lambda s, r: (0, 0, 0, s))],
        out_specs=pl.BlockSpec((rt * S5_L, LANE), lambda s, r: (r, s)),
        out_shape=jax.ShapeDtypeStruct((N, S5_WIDTH), F32),
        scratch_shapes=[pltpu.VMEM((S5_W, S5_W), BF16), pltpu.VMEM((S5_SW, S5_W), BF16)],
        compiler_params=_cp(("parallel", "arbitrary"), 56), name="s5_chunk_out",
    )(p, carry_in, kk, cw)
    return y, fin


def _s5_post_kernel(y_ref, u_ref, d_ref, w_ref, b_ref, o_ref):
    yg = jax.nn.gelu(y_ref[...] + d_ref[...] * u_ref[...])
    z = jnp.dot(yg.astype(BF16), w_ref[...], preferred_element_type=F32) + b_ref[...]
    o_ref[...] = (yg * jax.nn.sigmoid(z)).astype(o_ref.dtype)


def _s5_post(y, p, d, w, b, tm):
    N = y.shape[0]
    W = S5_WIDTH
    return pl.pallas_call(
        _s5_post_kernel, grid=(N // tm,),
        in_specs=[pl.BlockSpec((tm, W), lambda i: (i, 0)), pl.BlockSpec((tm, W), lambda i: (i, C_S5 // W)),
                  pl.BlockSpec((1, W), lambda i: (0, 0)), pl.BlockSpec((W, W), lambda i: (0, 0)),
                  pl.BlockSpec((1, W), lambda i: (0, 0))],
        out_specs=pl.BlockSpec((tm, W), lambda i: (i, 0)),
        out_shape=jax.ShapeDtypeStruct((N, W), BF16),
        compiler_params=_cp(("parallel",)), name="s5_glu",
    )(y, p, d, w, b)


def _merge_kernel(h_ref, a_ref, b_ref, c_ref, wg0, wg1, wg2, bg0, bg1, bg2, wb_ref, o_ref):
    h = h_ref[...]
    acc = None
    for i, (br, wg, bg) in enumerate(((a_ref, wg0, bg0), (b_ref, wg1, bg1), (c_ref, wg2, bg2))):
        gate = jax.nn.sigmoid(jnp.dot(h, wg[...], preferred_element_type=F32) + bg[...])
        t = gate * jnp.dot(br[...], wb_ref[i], preferred_element_type=F32)
        acc = t if acc is None else acc + t
    o_ref[...] = acc.astype(o_ref.dtype)


def _merge(h, o_da, o_gla, o_s5, w_gate, b_gate, w_branch, tm, tn):
    N, D = h.shape
    W = o_da.shape[1]
    nt = D // tn
    act = pl.BlockSpec((tm, W), lambda j, i: (i, 0))
    in_specs = [pl.BlockSpec((tm, D), lambda j, i: (i, 0)), act, act, act]
    in_specs += [pl.BlockSpec((D, tn), lambda j, i, br=br: (0, br * nt + j)) for br in range(N_BRANCH)]
    in_specs += [pl.BlockSpec((1, tn), lambda j, i, br=br: (0, br * nt + j)) for br in range(N_BRANCH)]
    in_specs += [pl.BlockSpec((N_BRANCH, W, tn), lambda j, i: (0, 0, j))]
    return pl.pallas_call(
        _merge_kernel, grid=(nt, N // tm), in_specs=in_specs,
        out_specs=pl.BlockSpec((tm, tn), lambda j, i: (i, j)),
        out_shape=jax.ShapeDtypeStruct((N, D), BF16),
        compiler_params=_cp(("parallel", "parallel"), 56), name="gated_merge",
    )(h, o_da, o_gla, o_s5, w_gate, w_gate, w_gate, b_gate, b_gate, b_gate, w_branch)


def _wout_kernel(m_ref, w_ref, x_ref, g_ref, o_ref):
    o_ref[...] = x_ref[...] + g_ref[0] * jnp.dot(m_ref[...], w_ref[...], preferred_element_type=F32)


def _wout(merged, w, x, gate, midx, tm, tn):
    N, D = x.shape
    return pl.pallas_call(
        _wout_kernel, grid=(D // tn, N // tm),
        in_specs=[pl.BlockSpec((tm, D), lambda j, i: (i, 0)), pl.BlockSpec((D, tn), lambda j, i: (0, j)),
                  pl.BlockSpec((tm, tn), lambda j, i: (i, j)), pl.BlockSpec((1, 1, tn), lambda j, i: (midx(i), 0, j))],
        out_specs=pl.BlockSpec((tm, tn), lambda j, i: (i, j)),
        out_shape=jax.ShapeDtypeStruct((N, D), F32),
        compiler_params=_cp(("parallel", "parallel")), name="out_proj_residual",
    )(merged, w, x, gate)


def _route_kernel(lg_ref, bias_ref, asg_ref, tile_ref, rank_ref, *, tl):
    E, n_tok = lg_ref.shape
    EPG = EXPERTS_PER_GROUP
    ninf = -jnp.inf
    eio = lax.broadcasted_iota(jnp.int32, (E, tl), 0)
    upper = jnp.where(lax.broadcasted_iota(jnp.int32, (tl, tl), 0) < lax.broadcasted_iota(jnp.int32, (tl, tl), 1),
                      1.0, 0.0).astype(BF16)
    ecol = lax.broadcasted_iota(jnp.int32, (E, 1), 0)

    def select(j, carry):
        ls = pl.ds(pl.multiple_of(j * tl, tl), tl)
        s = jax.nn.sigmoid(lg_ref[:, ls])
        sel = s + bias_ref[...]
        rows = [sel[e:e + 1, :] for e in range(E)]
        srow = [s[e:e + 1, :] for e in range(E)]
        gi = jnp.zeros((1, tl), jnp.int32)
        best = None
        for g in range(N_EXPERT_GROUPS):
            a, b, c, d = rows[g * EPG:(g + 1) * EPG]
            hi1, lo1, hi2, lo2 = jnp.maximum(a, b), jnp.minimum(a, b), jnp.maximum(c, d), jnp.minimum(c, d)
            gs = jnp.maximum(hi1, hi2) + jnp.maximum(jnp.minimum(hi1, hi2), jnp.maximum(lo1, lo2))
            if best is None:
                best = gs
            else:
                upd = gs > best
                gi = jnp.where(upd, g, gi)
                best = jnp.where(upd, gs, best)

        def pick(rs, i):
            out = rs[(N_EXPERT_GROUPS - 1) * EPG + i]
            for g in range(N_EXPERT_GROUPS - 2, -1, -1):
                out = jnp.where(gi == g, rs[g * EPG + i], out)
            return out

        v = [pick(rows, i) for i in range(EPG)]
        sv = [pick(srow, i) for i in range(EPG)]
        i1, b1, s1 = jnp.zeros((1, tl), jnp.int32), v[0], sv[0]
        for i in range(1, EPG):
            upd = v[i] > b1
            i1, b1, s1 = jnp.where(upd, i, i1), jnp.where(upd, v[i], b1), jnp.where(upd, sv[i], s1)
        i2, b2, s2 = jnp.zeros((1, tl), jnp.int32), jnp.full((1, tl), ninf, F32), sv[0]
        for i in range(EPG):
            cand = jnp.where(i1 == i, ninf, v[i])
            upd = cand > b2
            i2, b2, s2 = jnp.where(upd, i, i2), jnp.where(upd, cand, b2), jnp.where(upd, sv[i], s2)
        e0 = gi * EPG + i1
        e1 = gi * EPG + i2
        den = s1 + s2
        asg_ref[2:3, ls] = s1 / den
        asg_ref[3:4, ls] = s2 / den
        asg_ref[4:5, ls] = e0.astype(F32)
        asg_ref[5:6, ls] = e1.astype(F32)
        m = jnp.where((eio == e0) | (eio == e1), 1.0, 0.0)
        rank_ref[:, ls] = jnp.dot(m.astype(BF16), upper, preferred_element_type=F32) + carry
        return carry + jnp.sum(m, axis=1, keepdims=True)

    counts = lax.fori_loop(0, n_tok // tl, select, jnp.zeros((E, 1), F32))
    ptiles = jnp.floor((counts + (MOE_TM - 1)) * (1.0 / MOE_TM))
    run = jnp.zeros((1, 1), F32)
    pstart_t = jnp.zeros((E, 1), F32)
    for e in range(E):
        pstart_t = jnp.where(ecol == e, run, pstart_t)
        run = run + ptiles[e:e + 1, :]
    pstart = pstart_t * MOE_TM

    def place(j, _):
        ls = pl.ds(pl.multiple_of(j * tl, tl), tl)
        dest = rank_ref[:, ls] + pstart
        e0 = asg_ref[4:5, ls].astype(jnp.int32)
        e1 = asg_ref[5:6, ls].astype(jnp.int32)
        asg_ref[0:1, ls] = jnp.sum(jnp.where(eio == e0, dest, 0.0), axis=0, keepdims=True)
        asg_ref[1:2, ls] = jnp.sum(jnp.where(eio == e1, dest, 0.0), axis=0, keepdims=True)
        asg_ref[6:8, ls] = jnp.zeros((2, tl), F32)
        return 0

    lax.fori_loop(0, n_tok // tl, place, 0)

    ntp = tile_ref.shape[1]
    pend_t = pstart_t + ptiles
    ti = lax.broadcasted_iota(jnp.int32, (1, ntp), 1).astype(F32)
    te = jnp.sum(jnp.where(pend_t <= ti, 1.0, 0.0), axis=0, keepdims=True)
    used = ti < run
    last_e = jnp.max(jnp.where(counts > 0.0, ecol.astype(F32), 0.0), axis=0, keepdims=True)
    te = jnp.where(used, jnp.minimum(te, E - 1.0), last_e)
    tile_ref[0:1, :] = te.astype(jnp.int32)
    tile_ref[1:2, :] = jnp.where(used, 1, 0).astype(jnp.int32)
    tile_ref[2:8, :] = jnp.zeros((6, ntp), jnp.int32)


def _route(logits_t, router_bias, ntile):
    E, N = logits_t.shape
    assert MOE_TM & (MOE_TM - 1) == 0
    ntp = -(-ntile // LANE) * LANE
    full = lambda shape: pl.BlockSpec(shape, lambda i: (0, 0))
    return pl.pallas_call(
        functools.partial(_route_kernel, tl=_pick(512, N)), grid=(1,),
        in_specs=[full((E, N)), full((E, 1))],
        out_specs=[full((8, N)), full((8, ntp))],
        out_shape=[jax.ShapeDtypeStruct((8, N), F32), jax.ShapeDtypeStruct((8, ntp), jnp.int32)],
        scratch_shapes=[pltpu.VMEM((E, N), F32)],
        compiler_params=_cp(("arbitrary",)), name="moe_route",
    )(logits_t, router_bias.astype(F32).reshape(E, 1))


def _moe_kernel(te_ref, tv_ref, x_ref, w1_ref, w3_ref, w2_ref, o_ref):
    valid = tv_ref[pl.program_id(0)] != 0

    @pl.when(valid)
    def _():
        x = x_ref[...]
        F = w1_ref.shape[2]
        acc = None
        for f0 in range(0, F, MOE_FC):
            fs = slice(f0, min(f0 + MOE_FC, F))
            a = jnp.dot(x, w1_ref[0, :, fs], preferred_element_type=F32)
            b = jnp.dot(x, w3_ref[0, :, fs], preferred_element_type=F32)
            hid = ((a * jax.nn.sigmoid(a)) * b).astype(BF16)
            t = jnp.dot(hid, w2_ref[0, fs, :], preferred_element_type=F32)
            acc = t if acc is None else acc + t
        o_ref[...] = acc.astype(o_ref.dtype)

    @pl.when(jnp.logical_not(valid))
    def _():
        o_ref[...] = jnp.zeros(o_ref.shape, o_ref.dtype)


def _moe(xin, w1, w3, w2, layer, tile_e, tile_valid):
    cap, D = xin.shape
    F = w1.shape[3]
    TM = MOE_TM
    grid_spec = pltpu.PrefetchScalarGridSpec(
        num_scalar_prefetch=2, grid=(cap // TM,),
        in_specs=[pl.BlockSpec((TM, D), lambda i, te, tv: (i, 0)),
                  pl.BlockSpec((None, 1, D, F), lambda i, te, tv: (layer, te[i], 0, 0)),
                  pl.BlockSpec((None, 1, D, F), lambda i, te, tv: (layer, te[i], 0, 0)),
                  pl.BlockSpec((None, 1, F, D), lambda i, te, tv: (layer, te[i], 0, 0))],
        out_specs=pl.BlockSpec((TM, D), lambda i, te, tv: (i, 0)))
    return pl.pallas_call(
        _moe_kernel, grid_spec=grid_spec,
        out_shape=jax.ShapeDtypeStruct((cap, D), BF16),
        compiler_params=_cp(("arbitrary",), 58), name="moe_experts",
    )(tile_e, tile_valid, xin, w1, w3, w2)


def _combine_kernel(x_ref, y0_ref, y1_ref, w0_ref, w1_ref, g_ref, o_ref):
    o_ref[...] = x_ref[...] + g_ref[0] * (y0_ref[...] * w0_ref[...] + y1_ref[...] * w1_ref[...])


def _combine(x, y0, y1, w0, w1, gate, midx, tm):
    N, D = x.shape
    blk = pl.BlockSpec((tm, D), lambda i: (i, 0))
    col = pl.BlockSpec((tm, 1), lambda i: (i, 0))
    return pl.pallas_call(
        _combine_kernel, grid=(N // tm,),
        in_specs=[blk, blk, blk, col, col, pl.BlockSpec((1, 1, D), lambda i: (midx(i), 0, 0))],
        out_specs=blk, out_shape=jax.ShapeDtypeStruct((N, D), F32),
        compiler_params=_cp(("parallel",)), name="moe_residual",
    )(x, y0, y1, w0, w1, gate)


def _rope_tables(n_tokens):
    rows = n_tokens // GRID_W
    row = jnp.repeat(jnp.arange(rows, dtype=F32), GRID_W)
    col = jnp.tile(jnp.arange(GRID_W, dtype=F32), rows)
    inv = ROPE_BASE ** (-jnp.arange(0, AX_DIM, 2, dtype=F32) / AX_DIM)
    ar = row[:, None] * inv[None, :]
    ac = col[:, None] * inv[None, :]
    ang = jnp.tile(jnp.concatenate([ar, ar, ac, ac], axis=-1), (1, LANE // DA_HEAD))
    first = (jnp.arange(LANE) % (AX_DIM)) < (AX_DIM // 2)
    return jnp.cos(ang), jnp.where(first[None, :], -jnp.sin(ang), jnp.sin(ang))


def kernel(x_prompt, x_sample, cache_k, cache_v, state_gla, state_s5_re, state_s5_im, c, c_ctx, w_mod, b_mod, g_norm, w_in, da_lambda_q, da_lambda_k, da_subln_g, gla_w_alpha, gla_b_alpha, gla_norm_g, s5_a_re, s5_a_im, s5_log_dt, s5_b_re, s5_b_im, s5_c_re, s5_c_im, s5_d, s5_w_glu, s5_b_glu, w_branch, w_gate, b_gate, w_out, router_w, router_bias, moe_w1, moe_w3, moe_w2, g_final):
    B, T, D = x_prompt.shape
    DB, DT, _ = x_sample.shape
    L = w_in.shape[0]
    PAST = cache_k.shape[2]
    NP, NS = B * T, DB * DT
    N = NP + NS
    nseq = B + DB
    assert D == D_MODEL and 1 + DB <= 8 and NP % DT == 0 and T % GLA_ROWS == 0 and DT % GLA_ROWS == 0
    tm = _pick(512, NP, DT)
    midx = functools.partial(_mod_index, tm=tm, n_ctx_tok=NP, dec_seq=DT)

    x = jnp.concatenate([x_prompt.reshape(NP, D), x_sample.reshape(NS, D)], axis=0)
    cv = jnp.zeros((8, D), F32).at[0].set(c_ctx).at[1:1 + DB].set(c)
    mods = _mod(cv, w_mod, b_mod).reshape(L, 8, N_MOD, 1, D)
    zero_mod = jnp.zeros((8, 1, D), F32)

    rope_tabs = _rope_tables(DT)
    ck = cache_k.reshape(DB, L, PAST, DA_HEADS * 2 * DA_HEAD)
    cvv = cache_v.reshape(DB, L, PAST, DA_HEADS * DA_VDIM)
    r = jnp.arange(GLA_ROWS)
    same = (r[:, None] // GLA_CHUNK) == (r[None, :] // GLA_CHUNK)
    tri = jnp.stack([same & (r[None, :] <= r[:, None]), same & (r[None, :] >= r[:, None])]).astype(BF16)
    rw = router_w.astype(F32).T
    rw_hi = rw.astype(BF16)
    rw_lo = (rw - rw_hi.astype(F32)).astype(BF16)
    ntile = N * TOP_K // MOE_TM + N_EXPERTS
    tok2 = jnp.tile(jnp.arange(N, dtype=jnp.int32), TOP_K)
    moe_w1_bf, moe_w3_bf, moe_w2_bf = moe_w1.astype(BF16), moe_w3.astype(BF16), _cast_bf16(moe_w2)
    tq_ctx = _pick(256, T)
    tq_lat = _pick(256, DT)
    s5_segments = ((0, B, T // S5_L), (NP // S5_L, DB, DT // S5_L))

    g_list, re_list, im_list = [], [], []
    for l in range(L):
        m = mods[l]
        shift1, scale1, gate1, shift2, scale2, gate2 = (m[:, i] for i in range(N_MOD))
        wi = w_in[l]
        w_in_p = jnp.concatenate([wi[:, :C_S5], wi[:, C_S5 + GLA_RANK:], wi[:, C_S5:C_S5 + GLA_RANK],
                                  jnp.zeros((D, P_COLS - C_GA - GLA_RANK), F32)], axis=1).astype(BF16)
        h = _norm(x, g_norm[l, 0][None], scale1, shift1, midx, tm, BF16)
        p = _matmul(h, w_in_p, tm, 1536, F32, "in_proj")

        lam_init = 0.8 - 0.6 * math.exp(-0.3 * l)
        lq, lk = da_lambda_q[l], da_lambda_k[l]
        lam = jnp.exp(jnp.sum(lq[0] * lk[0])) - jnp.exp(jnp.sum(lq[1] * lk[1])) + lam_init
        lam_row = jnp.full((1, LANE), lam, F32)
        g_row = (da_subln_g[l] * (1.0 - lam_init))[None]
        o_da = jnp.concatenate([
            _attention(p, lam_row, g_row, 0, B, T, tq_ctx, DA_HEADS, _pick(ATT_KB, T)),
            _attention_latent(p, lam_row, g_row, NP, DB, DT, tq_lat, rope_tabs, ck, cvv, l)], axis=0)

        wa = jnp.zeros((1, 2, LANE, GLA_KTOT), F32).at[0, :, :GLA_RANK].set(gla_w_alpha[l]).astype(BF16)
        ba = gla_b_alpha[l].reshape(1, 2, 1, GLA_KTOT)
        o_f, o_b, gla_fin = _gla(p, wa, ba, tri, state_gla, l, ((B, T), (DB, DT)))
        o_gla = _gla_post(o_f, o_b, p, gla_norm_g[l][None], tm)

        mats = _s5_matrices(s5_a_re[l], s5_a_im[l], s5_log_dt[l], s5_b_re[l], s5_b_im[l], s5_c_re[l], s5_c_im[l])
        x0 = jnp.stack([state_s5_re[:, l], state_s5_im[:, l]], axis=2)
        x0 = jnp.concatenate([jnp.zeros((B,) + x0.shape[1:], F32), x0], axis=0)
        x0 = x0.reshape(4 * nseq, S5_SG, S5_SW // 4).transpose(1, 0, 2)
        y, s5_fin = _s5(p, mats, x0, s5_segments)
        o_s5 = _s5_post(y, p, s5_d[l][None], s5_w_glu[l].astype(BF16), s5_b_glu[l][None], tm)
        s5_fin = s5_fin.transpose(1, 0, 2).reshape(nseq, 2, 2, S5_GROUPS, S5_STATE)[:B]

        merged = _merge(h, o_da, o_gla, o_s5, w_gate[l].astype(BF16), b_gate[l][None], w_branch[l].astype(BF16),
                        tm, 512)
        x = _wout(merged, w_out[l].astype(BF16), x, gate1, midx, tm, 1024)

        h2, logits_t = _norm(x, g_norm[l, 1][None], scale2, shift2, midx, tm, BF16, router=(rw_hi, rw_lo))
        asg, tiles = _route(logits_t, router_bias, ntile)
        slot = asg[0:2].astype(jnp.int32)
        buf_tok = jnp.zeros((ntile * MOE_TM,), jnp.int32).at[slot.reshape(-1)].set(tok2, unique_indices=True)
        xin = jnp.take(h2, buf_tok, axis=0, mode="clip")
        ye = _moe(xin, moe_w1_bf, moe_w3_bf, moe_w2_bf, l, tiles[0, :ntile], tiles[1, :ntile])
        x = _combine(x, jnp.take(ye, slot[0], axis=0, mode="clip"), jnp.take(ye, slot[1], axis=0, mode="clip"),
                     asg[2].reshape(N, 1),
                     asg[3].reshape(N, 1), gate2, midx, tm)

        kbuf, vbuf = _kv_store(p, None if l == 0 else (kbuf, vbuf), l, L, B, T)
        g_list.append(gla_fin[:B])
        re_list.append(s5_fin[:, :, 0])
        im_list.append(s5_fin[:, :, 1])

    y_ctx = _norm(x, g_final[None], zero_mod, zero_mod, midx, tm, F32, rows=(0, NP))
    y_lat = _norm(x, g_final[None], zero_mod, zero_mod, midx, tm, F32, rows=(NP, NS))
    return (y_ctx.reshape(B, T, D), y_lat.reshape(DB, DT, D),
            kbuf.reshape(B, L, T, DA_HEADS, 2 * DA_HEAD), vbuf.reshape(B, L, T, DA_HEADS, DA_VDIM),
            jnp.stack(g_list, axis=1),
            jnp.stack(re_list, axis=1), jnp.stack(im_list, axis=1))
```

```python
import functools
import math

import jax
import jax.numpy as jnp
from jax import lax
from jax.experimental import pallas as pl
from jax.experimental.pallas import tpu as pltpu

F32 = jnp.float32
BF16 = jnp.bfloat16

D_MODEL = 2048
GRID_W = 64
EPS = 1e-6
N_MOD = 6
N_BRANCH = 3
DA_HEADS = 8
DA_HEAD = 64
DA_VDIM = 128
ROPE_BASE = 10000.0
AX_DIM = DA_HEAD // 2
GLA_HEADS = 4
GLA_DK = 128
GLA_DV = 256
GLA_KTOT = GLA_HEADS * GLA_DK
GLA_WIDTH = GLA_HEADS * GLA_DV
GLA_RANK = 16
GLA_TAU = 16.0
GLA_CHUNK = 64
S5_WIDTH = 1024
S5_CH = 16
S5_GROUPS = S5_WIDTH // S5_CH
S5_STATE = 64
N_EXPERTS = 16
N_EXPERT_GROUPS = 4
EXPERTS_PER_GROUP = N_EXPERTS // N_EXPERT_GROUPS
TOPK_GROUP = 1
TOP_K = 2
D_FF_EXPERT = 1408

LANE = 128
C_Q, C_K, C_V, C_GQ, C_GK, C_GV, C_GR, C_S5, C_GA = 0, 1024, 2048, 3072, 3584, 4096, 5120, 6144, 7168
P_COLS = 7680
GLA_ROWS = 256
S5_L = 16
S5_SG = S5_WIDTH // LANE
S5_GB = S5_GROUPS // S5_SG
S5_W = S5_L * LANE
S5_SW = 4 * S5_GB * S5_STATE
MOE_TM = 512
MOE_FC = 512
ATT_KB = 512
VMEM_MB = 1024 * 1024


def _cp(sem, vmem_mb=48):
    return pltpu.CompilerParams(dimension_semantics=sem, vmem_limit_bytes=vmem_mb * VMEM_MB)


def _pick(pref, *ns):
    t = pref
    while any(n % t for n in ns):
        t //= 2
    return t


def _mod_index(i, tm, n_ctx_tok, dec_seq):
    t0 = i * tm
    return jnp.where(t0 < n_ctx_tok, 0, 1 + (t0 - n_ctx_tok) // dec_seq)


def _mod_kernel(c_ref, w_ref, b_ref, o_ref):
    c = c_ref[...]
    s = (c * jax.nn.sigmoid(c)).astype(BF16)
    o_ref[0] = jnp.dot(s, w_ref[0].astype(BF16), preferred_element_type=F32) + b_ref[0]


def _mod(cv, w_mod, b_mod):
    L, D, M = w_mod.shape
    tn = 1024
    return pl.pallas_call(
        _mod_kernel,
        grid=(L, M // tn),
        in_specs=[pl.BlockSpec((8, D), lambda l, j: (0, 0)),
                  pl.BlockSpec((1, D, tn), lambda l, j: (l, 0, j)),
                  pl.BlockSpec((1, 1, tn), lambda l, j: (l, 0, j))],
        out_specs=pl.BlockSpec((1, 8, tn), lambda l, j: (l, 0, j)),
        out_shape=jax.ShapeDtypeStruct((L, 8, M), F32),
        compiler_params=_cp(("parallel", "parallel")),
        name="mod_vectors",
    )(cv, w_mod, b_mod.reshape(L, 1, M))


def _cast_kernel(x_ref, o_ref):
    o_ref[...] = x_ref[...].astype(o_ref.dtype)


def _cast_bf16(w):
    lead, (R, C) = w.shape[:-2], w.shape[-2:]
    n = math.prod(lead)
    out = pl.pallas_call(
        _cast_kernel, grid=(n,),
        in_specs=[pl.BlockSpec((1, R, C), lambda i: (i, 0, 0))], out_specs=pl.BlockSpec((1, R, C), lambda i: (i, 0, 0)),
        out_shape=jax.ShapeDtypeStruct((n, R, C), BF16),
        compiler_params=_cp(("parallel",)), name="cast_bf16",
    )(w.reshape(n, R, C))
    return out.reshape(lead + (R, C))


def _norm_core(x_ref, g_ref, sc_ref, sh_ref):
    x = x_ref[...]
    y = x * lax.rsqrt(jnp.mean(x * x, axis=-1, keepdims=True) + EPS)
    return (y * g_ref[...]) * (1.0 + sc_ref[0]) + sh_ref[0]


def _norm_kernel(x_ref, g_ref, sc_ref, sh_ref, o_ref):
    o_ref[...] = _norm_core(x_ref, g_ref, sc_ref, sh_ref).astype(o_ref.dtype)


def _norm_router_kernel(x_ref, g_ref, sc_ref, sh_ref, rh_ref, rl_ref, o_ref, lg_ref):
    h = _norm_core(x_ref, g_ref, sc_ref, sh_ref)
    hh = h.astype(BF16)
    hl = (h - hh.astype(F32)).astype(BF16)
    o_ref[...] = hh
    nt = (((1,), (1,)), ((), ()))
    lg_ref[...] = (lax.dot_general(rh_ref[...], hh, nt, preferred_element_type=F32)
                   + lax.dot_general(rh_ref[...], hl, nt, preferred_element_type=F32)
                   + lax.dot_general(rl_ref[...], hh, nt, preferred_element_type=F32))


def _norm(x, g, sc, sh, midx, tm, out_dtype, router=None, rows=None):
    N, D = x.shape
    r0, n = (0, N) if rows is None else rows
    t0 = r0 // tm
    in_specs = [pl.BlockSpec((tm, D), lambda i: (t0 + i, 0)),
                pl.BlockSpec((1, D), lambda i: (0, 0)),
                pl.BlockSpec((1, 1, D), lambda i: (midx(t0 + i), 0, 0)),
                pl.BlockSpec((1, 1, D), lambda i: (midx(t0 + i), 0, 0))]
    if router is None:
        return pl.pallas_call(
            _norm_kernel, grid=(n // tm,), in_specs=in_specs,
            out_specs=pl.BlockSpec((tm, D), lambda i: (i, 0)),
            out_shape=jax.ShapeDtypeStruct((n, D), out_dtype),
            compiler_params=_cp(("parallel",)), name="norm_mod",
        )(x, g, sc, sh)
    assert rows is None
    rh, rl = router
    E = rh.shape[0]
    in_specs += [pl.BlockSpec((E, D), lambda i: (0, 0)), pl.BlockSpec((E, D), lambda i: (0, 0))]
    return pl.pallas_call(
        _norm_router_kernel, grid=(N // tm,), in_specs=in_specs,
        out_specs=[pl.BlockSpec((tm, D), lambda i: (i, 0)), pl.BlockSpec((E, tm), lambda i: (0, i))],
        out_shape=[jax.ShapeDtypeStruct((N, D), BF16), jax.ShapeDtypeStruct((E, N), F32)],
        compiler_params=_cp(("parallel",)), name="norm_mod_router",
    )(x, g, sc, sh, rh, rl)


def _mm_kernel(a_ref, w_ref, o_ref):
    o_ref[...] = jnp.dot(a_ref[...], w_ref[...], preferred_element_type=F32).astype(o_ref.dtype)


def _matmul(a, w, tm, tn, out_dtype, name):
    M, K = a.shape
    Nc = w.shape[1]
    return pl.pallas_call(
        _mm_kernel, grid=(Nc // tn, M // tm),
        in_specs=[pl.BlockSpec((tm, K), lambda j, i: (i, 0)), pl.BlockSpec((K, tn), lambda j, i: (0, j))],
        out_specs=pl.BlockSpec((tm, tn), lambda j, i: (i, j)),
        out_shape=jax.ShapeDtypeStruct((M, Nc), out_dtype),
        compiler_params=_cp(("parallel", "parallel")), name=name,
    )(a, w)


def _kv_store_kernel(k_ref, v_ref, *refs):
    ko_ref, vo_ref = refs[-2:]
    ko_ref[...] = k_ref[...]
    vo_ref[...] = v_ref[...]


def _kv_store(p, bufs, layer, n_layers, B, T):
    W = DA_HEADS * DA_VDIM
    in_specs = [pl.BlockSpec((T, W), lambda b: (b, C_K // W)), pl.BlockSpec((T, W), lambda b: (b, C_V // W))]
    args = [p, p]
    aliases = {}
    if bufs is not None:
        in_specs += [pl.BlockSpec(memory_space=pl.ANY), pl.BlockSpec(memory_space=pl.ANY)]
        args += list(bufs)
        aliases = {2: 0, 3: 1}
    out_spec = pl.BlockSpec((None, None, T, W), lambda b: (b, layer, 0, 0))
    shape = jax.ShapeDtypeStruct((B, n_layers, T, W), F32)
    return pl.pallas_call(
        _kv_store_kernel, grid=(B,), in_specs=in_specs, out_specs=[out_spec, out_spec], out_shape=[shape, shape],
        input_output_aliases=aliases, compiler_params=_cp(("parallel",)), name="kv_store",
    )(*args)


def _rope(x, cos, sin_signed):
    lane = lax.broadcasted_iota(jnp.int32, (1, LANE), 1)
    first = (lane % (2 * (AX_DIM // 2))) < (AX_DIM // 2)
    xr = jnp.where(first, pltpu.roll(x, LANE - AX_DIM // 2, 1), pltpu.roll(x, AX_DIM // 2, 1))
    return x * cos + xr * sin_signed


def _attn_kernel(*refs, rope, ctx, hs, kb):
    refs = list(refs)
    q_ref, k_ref, v_ref = refs[:3]
    refs = refs[3:]
    if ctx:
        ck_ref, cv_ref = refs[:2]
        refs = refs[2:]
    if rope:
        cq_ref, sq_ref, ckk_ref, skk_ref = refs[:4]
        refs = refs[4:]
    lam_ref, g_ref, o_ref, kb_ref, vb_ref = refs[:5]
    if ctx:
        ckb_ref, cvb_ref = refs[5:]

    hsl = [slice(h * LANE, (h + 1) * LANE) for h in range(hs)]

    @pl.when(pl.program_id(2) == 0)
    def _():
        for h in range(hs):
            k = k_ref[:, hsl[h]]
            if rope:
                k = _rope(k, ckk_ref[...], skk_ref[...])
            kb_ref[:, hsl[h]] = k.astype(BF16)
        vb_ref[...] = v_ref[...].astype(BF16)
        if ctx:
            ckb_ref[...] = ck_ref[...].astype(BF16)
            cvb_ref[...] = cv_ref[...].astype(BF16)

    tq = q_ref.shape[0]
    n_new = kb_ref.shape[0]
    lane = lax.broadcasted_iota(jnp.int32, (1, LANE), 1)
    lo = lane < DA_HEAD
    nt = (((1,), (1,)), ((), ()))
    qqs = []
    for h in range(hs):
        q = q_ref[:, hsl[h]]
        if rope:
            q = _rope(q, cq_ref[...], sq_ref[...])
        q = q * (DA_HEAD ** -0.5 * math.log2(math.e))
        qqs.append(jnp.concatenate([jnp.where(lo, q, 0.0), jnp.where(lo, 0.0, q)], axis=0).astype(BF16))

    def block(qq, kblk, vblk, carry):
        m, l, acc = carry
        s = lax.dot_general(qq, kblk, nt, preferred_element_type=F32)
        mn = jnp.maximum(m, jnp.max(s, axis=-1, keepdims=True))
        e = jnp.exp2(s - mn)
        r = jnp.exp2(m - mn)
        l = r * l + jnp.sum(e, axis=-1, keepdims=True)
        acc = r * acc + jnp.dot(e.astype(BF16), vblk, preferred_element_type=F32)
        return mn, l, acc

    def all_heads(kref, vref, rows, carries):
        return tuple(block(qqs[h], kref[rows, hsl[h]], vref[rows, hsl[h]], carries[h]) for h in range(hs))

    carries = tuple((jnp.full((2 * tq, 1), -jnp.inf, F32), jnp.zeros((2 * tq, 1), F32),
                     jnp.zeros((2 * tq, LANE), F32)) for _ in range(hs))
    if ctx:
        for c in range(ckb_ref.shape[0] // kb):
            carries = all_heads(ckb_ref, cvb_ref, slice(c * kb, (c + 1) * kb), carries)
    if n_new // kb == 1:
        carries = all_heads(kb_ref, vb_ref, slice(None), carries)
    else:
        def body(j, carries):
            return all_heads(kb_ref, vb_ref, pl.ds(pl.multiple_of(j * kb, kb), kb), carries)
        carries = lax.fori_loop(0, n_new // kb, body, carries)
    for h in range(hs):
        _, l, acc = carries[h]
        o = acc[:tq] * (1.0 / l[:tq]) - acc[tq:] * (lam_ref[:, :1] / l[tq:])
        y = o * lax.rsqrt(jnp.mean(o * o, axis=-1, keepdims=True) + EPS)
        o_ref[:, hsl[h]] = (y * g_ref[...]).astype(o_ref.dtype)


def _attention(p, lam_row, g_row, row0, B, T, tq, hs, kb, rope_tabs=None, ctx=None):
    nq = T // tq
    rb0 = row0 // tq
    kb0 = row0 // T
    W = hs * LANE
    hq, hk, hv = C_Q // W, C_K // W, C_V // W
    assert T % kb == 0
    in_specs = [pl.BlockSpec((tq, W), lambda b, h, i: (rb0 + b * nq + i, hq + h)),
                pl.BlockSpec((T, W), lambda b, h, i: (kb0 + b, hk + h)),
                pl.BlockSpec((T, W), lambda b, h, i: (kb0 + b, hv + h))]
    args = [p, p, p]
    scratch = [pltpu.VMEM((T, W), BF16), pltpu.VMEM((T, W), BF16)]
    if ctx is not None:
        ck, cv, layer = ctx
        past = ck.shape[2]
        assert past % kb == 0
        in_specs += [pl.BlockSpec((None, None, past, W), lambda b, h, i: (b, layer, 0, h)),
                     pl.BlockSpec((None, None, past, W), lambda b, h, i: (b, layer, 0, h))]
        args += [ck, cv]
        scratch += [pltpu.VMEM((past, W), BF16), pltpu.VMEM((past, W), BF16)]
    if rope_tabs is not None:
        cos, sin = rope_tabs
        in_specs += [pl.BlockSpec((tq, LANE), lambda b, h, i: (i, 0)), pl.BlockSpec((tq, LANE), lambda b, h, i: (i, 0)),
                     pl.BlockSpec((T, LANE), lambda b, h, i: (0, 0)), pl.BlockSpec((T, LANE), lambda b, h, i: (0, 0))]
        args += [cos, sin, cos, sin]
    in_specs += [pl.BlockSpec((1, LANE), lambda b, h, i: (0, 0)), pl.BlockSpec((1, LANE), lambda b, h, i: (0, 0))]
    args += [lam_row, g_row]
    return pl.pallas_call(
        functools.partial(_attn_kernel, rope=rope_tabs is not None, ctx=ctx is not None, hs=hs, kb=kb),
        grid=(B, DA_HEADS // hs, nq),
        in_specs=in_specs,
        out_specs=pl.BlockSpec((tq, W), lambda b, h, i: (b * nq + i, h)),
        out_shape=jax.ShapeDtypeStruct((B * T, DA_HEADS * DA_VDIM), BF16),
        scratch_shapes=scratch,
        compiler_params=_cp(("parallel", "parallel", "arbitrary"), 56),
        name="diff_attention_latent" if ctx is not None else "diff_attention_context",
    )(*args)


def _attn_latent_kernel(q_ref, k_ref, v_ref, ck_ref, cv_ref, cq_ref, sq_ref, ckk_ref, skk_ref, lam_ref, g_ref, o_ref,
                        kb_ref, vb_ref):
    @pl.when(pl.program_id(2) == 0)
    def _():
        kb_ref[...] = _rope(k_ref[...], ckk_ref[...], skk_ref[...]).astype(BF16)
        vb_ref[...] = v_ref[...].astype(BF16)

    q = _rope(q_ref[...], cq_ref[...], sq_ref[...]) * (DA_HEAD ** -0.5)
    lane = lax.broadcasted_iota(jnp.int32, (1, LANE), 1)
    lo = lane < DA_HEAD
    q0 = jnp.where(lo, q, 0.0).astype(BF16)
    q1 = jnp.where(lo, 0.0, q).astype(BF16)
    nt = (((1,), (1,)), ((), ()))
    kb = kb_ref[...]
    ckb = ck_ref[...].astype(BF16)
    s0 = lax.dot_general(q0, kb, nt, preferred_element_type=F32)
    s1 = lax.dot_general(q1, kb, nt, preferred_element_type=F32)
    c0 = lax.dot_general(q0, ckb, nt, preferred_element_type=F32)
    c1 = lax.dot_general(q1, ckb, nt, preferred_element_type=F32)
    m0 = jnp.maximum(jnp.max(s0, axis=-1, keepdims=True), jnp.max(c0, axis=-1, keepdims=True))
    m1 = jnp.maximum(jnp.max(s1, axis=-1, keepdims=True), jnp.max(c1, axis=-1, keepdims=True))
    e0 = jnp.exp(s0 - m0)
    e1 = jnp.exp(s1 - m1)
    ec0 = jnp.exp(c0 - m0)
    ec1 = jnp.exp(c1 - m1)
    l0 = jnp.sum(e0, axis=-1, keepdims=True) + jnp.sum(ec0, axis=-1, keepdims=True)
    l1 = jnp.sum(e1, axis=-1, keepdims=True) + jnp.sum(ec1, axis=-1, keepdims=True)
    a0 = 1.0 / l0
    a1 = lam_ref[:, :1] / l1
    o = jnp.dot((e0 * a0 - e1 * a1).astype(BF16), vb_ref[...], preferred_element_type=F32)
    o = o + jnp.dot((ec0 * a0 - ec1 * a1).astype(BF16), cv_ref[...].astype(BF16), preferred_element_type=F32)
    y = o * lax.rsqrt(jnp.mean(o * o, axis=-1, keepdims=True) + EPS)
    o_ref[...] = (y * g_ref[...]).astype(o_ref.dtype)


def _attention_latent(p, lam_row, g_row, row0, B, T, tq, rope_tabs, ck, cv, layer):
    nq = T // tq
    rb0 = row0 // tq
    kb0 = row0 // T
    hq, hk, hv = C_Q // LANE, C_K // LANE, C_V // LANE
    past = ck.shape[2]
    cos, sin = rope_tabs
    in_specs = [pl.BlockSpec((tq, LANE), lambda b, h, i: (rb0 + b * nq + i, hq + h)),
                pl.BlockSpec((T, LANE), lambda b, h, i: (kb0 + b, hk + h)),
                pl.BlockSpec((T, LANE), lambda b, h, i: (kb0 + b, hv + h)),
                pl.BlockSpec((None, None, past, LANE), lambda b, h, i: (b, layer, 0, h)),
                pl.BlockSpec((None, None, past, LANE), lambda b, h, i: (b, layer, 0, h)),
                pl.BlockSpec((tq, LANE), lambda b, h, i: (i, 0)), pl.BlockSpec((tq, LANE), lambda b, h, i: (i, 0)),
                pl.BlockSpec((T, LANE), lambda b, h, i: (0, 0)), pl.BlockSpec((T, LANE), lambda b, h, i: (0, 0)),
                pl.BlockSpec((1, LANE), lambda b, h, i: (0, 0)), pl.BlockSpec((1, LANE), lambda b, h, i: (0, 0))]
    return pl.pallas_call(
        _attn_latent_kernel, grid=(B, DA_HEADS, nq), in_specs=in_specs,
        out_specs=pl.BlockSpec((tq, LANE), lambda b, h, i: (b * nq + i, h)),
        out_shape=jax.ShapeDtypeStruct((B * T, DA_HEADS * DA_VDIM), BF16),
        scratch_shapes=[pltpu.VMEM((T, LANE), BF16), pltpu.VMEM((T, LANE), BF16)],
        compiler_params=_cp(("parallel", "parallel", "arbitrary"), 56),
        name="diff_attention_latent",
    )(p, p, p, ck, cv, cos, sin, cos, sin, lam_row, g_row)


def _log_sigmoid(z):
    return jnp.minimum(z, 0.0) - jnp.log1p(jnp.exp(-jnp.abs(z)))


def _gla_direction(d, q_ref, k_ref, v_ref, ga_ref, wa_ref, ba_ref, tri_ref, o_ref, st_ref):
    C = GLA_CHUNK
    nsub = GLA_ROWS // C
    z = jnp.dot(ga_ref[...].astype(BF16), wa_ref[0, d], preferred_element_type=F32) + ba_ref[0, d]
    la = _log_sigmoid(z) / GLA_TAU
    hi = la.astype(BF16)
    r1 = la - hi.astype(F32)
    mid = r1.astype(BF16)
    low = (r1 - mid.astype(F32)).astype(BF16)
    tri = tri_ref[d]
    bsum = (jnp.dot(tri, hi, preferred_element_type=F32) + jnp.dot(tri, mid, preferred_element_type=F32)
            + jnp.dot(tri, low, preferred_element_type=F32))
    row = lax.broadcasted_iota(jnp.int32, (C, C), 0)
    col = lax.broadcasted_iota(jnp.int32, (C, C), 1)
    mask = (row >= col) if d == 0 else (row <= col)
    nt = (((1,), (1,)), ((), ()))
    tn = (((0,), (0,)), ((), ()))
    subs = range(nsub) if d == 0 else range(nsub - 1, -1, -1)
    for h in range(GLA_HEADS):
        hk = slice(h * GLA_DK, (h + 1) * GLA_DK)
        hv = slice(h * GLA_DV, (h + 1) * GLA_DV)
        st = st_ref[d, h]
        for s in subs:
            rs = slice(s * C, (s + 1) * C)
            end = s * C + C - 1 if d == 0 else s * C
            bb = bsum[rs, hk]
            bl = bsum[end:end + 1, hk]
            q = q_ref[rs, hk] * (GLA_DK ** -0.5)
            k = k_ref[rs, hk]
            v = v_ref[rs, hv].astype(BF16)
            qt = (q * jnp.exp(bb)).astype(BF16)
            kt = (k * jnp.exp(-bb)).astype(BF16)
            kend = (k * jnp.exp(bl - bb)).astype(BF16)
            att = jnp.where(mask, lax.dot_general(qt, kt, nt, preferred_element_type=F32), 0.0).astype(BF16)
            o = jnp.dot(att, v, preferred_element_type=F32)
            o = o + lax.dot_general(qt, st.astype(BF16), nt, preferred_element_type=F32)
            o_ref[rs, hv] = o
            st = st * jnp.exp(bl) + lax.dot_general(v, kend, tn, preferred_element_type=F32)
        st_ref[d, h] = st


def _gla_kernel(qf, kf, vf, gaf, qb, kb, vb, gab, wa_ref, ba_ref, tri_ref, s0_ref, of_ref, ob_ref, fin_ref, st_ref,
                *, seq_pos):
    first, last, stateless = seq_pos(pl.program_id(0))

    @pl.when(first & stateless)
    def _():
        st_ref[...] = jnp.zeros(st_ref.shape, F32)

    @pl.when(first & jnp.logical_not(stateless))
    def _():
        for d in range(2):
            for h in range(GLA_HEADS):
                st_ref[d, h] = s0_ref[d, h].T

    _gla_direction(0, qf, kf, vf, gaf, wa_ref, ba_ref, tri_ref, of_ref, st_ref)
    _gla_direction(1, qb, kb, vb, gab, wa_ref, ba_ref, tri_ref, ob_ref, st_ref)

    @pl.when(last & stateless)
    def _():
        for d in range(2):
            for h in range(GLA_HEADS):
                fin_ref[d, h] = st_ref[d, h].T


def _gla(p, wa, ba, tri, state, layer, seqs):
    N = p.shape[0]
    R = GLA_ROWS
    (n1, t1), (n2, t2) = seqs
    nb1, nb2 = t1 // R, t2 // R
    steps1 = n1 * nb1

    def locate(s):
        in1 = s < steps1
        s2 = s - steps1
        nb = jnp.where(in1, nb1, nb2)
        seq = jnp.where(in1, s // nb1, n1 + s2 // nb2)
        n = jnp.where(in1, s % nb1, s2 % nb2)
        base = jnp.where(in1, (s // nb1) * nb1, steps1 + (s2 // nb2) * nb2)
        return seq, n, nb, base

    def fwd_blk(s):
        _, n, _, base = locate(s)
        return base + n

    def bwd_blk(s):
        _, n, nb, base = locate(s)
        return base + nb - 1 - n

    def seq_pos(s):
        _, n, nb, _ = locate(s)
        return n == 0, n == nb - 1, s < steps1

    def specs(blk):
        return [pl.BlockSpec((R, GLA_KTOT), lambda s: (blk(s), C_GQ // GLA_KTOT)),
                pl.BlockSpec((R, GLA_KTOT), lambda s: (blk(s), C_GK // GLA_KTOT)),
                pl.BlockSpec((R, GLA_WIDTH), lambda s: (blk(s), C_GV // GLA_WIDTH)),
                pl.BlockSpec((R, LANE), lambda s: (blk(s), C_GA // LANE))]

    s0_spec = pl.BlockSpec((None, None, 2, GLA_HEADS, GLA_DK, GLA_DV),
                           lambda s: (jnp.maximum(locate(s)[0] - n1, 0), layer, 0, 0, 0, 0))
    fin_spec = pl.BlockSpec((None, 2, GLA_HEADS, GLA_DK, GLA_DV),
                            lambda s: (jnp.minimum(locate(s)[0], n1 - 1), 0, 0, 0, 0))
    return pl.pallas_call(
        functools.partial(_gla_kernel, seq_pos=seq_pos),
        grid=(N // R,),
        in_specs=specs(fwd_blk) + specs(bwd_blk) + [
            pl.BlockSpec((1, 2, LANE, GLA_KTOT), lambda s: (0, 0, 0, 0)),
            pl.BlockSpec((1, 2, 1, GLA_KTOT), lambda s: (0, 0, 0, 0)),
            pl.BlockSpec((2, R, R), lambda s: (0, 0, 0)),
            s0_spec],
        out_specs=[pl.BlockSpec((R, GLA_WIDTH), lambda s: (fwd_blk(s), 0)),
                   pl.BlockSpec((R, GLA_WIDTH), lambda s: (bwd_blk(s), 0)),
                   fin_spec],
        out_shape=[jax.ShapeDtypeStruct((N, GLA_WIDTH), F32), jax.ShapeDtypeStruct((N, GLA_WIDTH), F32),
                   jax.ShapeDtypeStruct((n1, 2, GLA_HEADS, GLA_DK, GLA_DV), F32)],
        scratch_shapes=[pltpu.VMEM((2, GLA_HEADS, GLA_DV, GLA_DK), F32)],
        compiler_params=_cp(("arbitrary",)),
        name="gla_bidir",
    )(p, p, p, p, p, p, p, p, wa, ba, tri, state)


def _gla_post_kernel(of_ref, ob_ref, r_ref, g_ref, o_ref):
    for h in range(GLA_HEADS):
        hv = slice(h * GLA_DV, (h + 1) * GLA_DV)
        o = of_ref[:, hv] + ob_ref[:, hv]
        y = o * lax.rsqrt(jnp.mean(o * o, axis=-1, keepdims=True) + EPS) * g_ref[...]
        r = r_ref[:, hv]
        o_ref[:, hv] = (y * (r * jax.nn.sigmoid(r))).astype(o_ref.dtype)


def _gla_post(of, ob, p, g, tm):
    N = of.shape[0]
    return pl.pallas_call(
        _gla_post_kernel, grid=(N // tm,),
        in_specs=[pl.BlockSpec((tm, GLA_WIDTH), lambda i: (i, 0)), pl.BlockSpec((tm, GLA_WIDTH), lambda i: (i, 0)),
                  pl.BlockSpec((tm, GLA_WIDTH), lambda i: (i, C_GR // GLA_WIDTH)),
                  pl.BlockSpec((1, GLA_DV), lambda i: (0, 0))],
        out_specs=pl.BlockSpec((tm, GLA_WIDTH), lambda i: (i, 0)),
        out_shape=jax.ShapeDtypeStruct((N, GLA_WIDTH), BF16),
        compiler_params=_cp(("parallel",)), name="gla_norm_gate",
    )(of, ob, p, g)


def _s5_matrices(a_re, a_im, log_dt, b_re, b_im, c_re, c_im):
    hp = lax.Precision.HIGHEST
    L, G, P, CH = S5_L, S5_GROUPS, S5_STATE, S5_CH
    taus = jnp.arange(L + 1, dtype=F32)
    ii = jnp.arange(L)
    kerns, powers, bbars, carry, decay = [], [], [], [], []
    c_re_t, c_im_t = c_re.transpose(2, 0, 1), c_im.transpose(2, 0, 1)
    for d in range(2):
        dt = jnp.exp(log_dt[d])[:, None]
        ld_re, ld_im = a_re[d] * dt, a_im[d] * dt
        mag = jnp.exp(ld_re)
        lb_re, lb_im = mag * jnp.cos(ld_im), mag * jnp.sin(ld_im)
        den = a_re[d] * a_re[d] + a_im[d] * a_im[d]
        q_re = ((lb_re - 1.0) * a_re[d] + lb_im * a_im[d]) / den
        q_im = (lb_im * a_re[d] - (lb_re - 1.0) * a_im[d]) / den
        bb_re = q_re[..., None] * b_re - q_im[..., None] * b_im
        bb_im = q_re[..., None] * b_im + q_im[..., None] * b_re
        pmag = jnp.exp(ld_re[None] * taus[:, None, None])
        pw_re = pmag * jnp.cos(ld_im[None] * taus[:, None, None])
        pw_im = pmag * jnp.sin(ld_im[None] * taus[:, None, None])
        cp_re = c_re[None] * pw_re[:L, :, None, :] - c_im[None] * pw_im[:L, :, None, :]
        cp_im = c_re[None] * pw_im[:L, :, None, :] + c_im[None] * pw_re[:L, :, None, :]
        kern = (jnp.einsum('tgop,gpc->tcgo', cp_re, bb_re, precision=hp)
                - jnp.einsum('tgop,gpc->tcgo', cp_im, bb_im, precision=hp)).reshape(L, CH, G * CH)
        kerns.append(kern)
        powers.append(jnp.stack([pw_re.reshape(L + 1, G * P), pw_im.reshape(L + 1, G * P)]))
        bbars.append(jnp.stack([bb_re.transpose(2, 0, 1).reshape(CH, G * P), bb_im.transpose(2, 0, 1).reshape(CH, G * P)]))
        e = (ii + 1) if d == 0 else (L - ii)
        pt_re, pt_im = pw_re[e].transpose(0, 2, 1)[..., None], pw_im[e].transpose(0, 2, 1)[..., None]
        cw_re = c_re_t[None] * pt_re - c_im_t[None] * pt_im
        cw_im = c_re_t[None] * pt_im + c_im_t[None] * pt_re
        carry += [cw_re.reshape(L, P, G * CH), (-cw_im).reshape(L, P, G * CH)]
        decay.append((pw_re[L], pw_im[L]))
    kk = jnp.stack([kerns[0].at[0].add(kerns[1][0]), kerns[1]]).astype(BF16)
    pw = jnp.stack(powers)
    bb = jnp.stack(bbars)
    cw = jnp.stack(carry).astype(BF16)
    dec = jnp.stack([jnp.stack([x[0], x[1]]) for x in decay])
    dec = dec.reshape(4, S5_SG, S5_SW // 4).transpose(1, 0, 2)
    return kk, pw, bb, cw, dec


def _s5_slab(u_ref, rt):
    return jnp.concatenate([u_ref[pl.ds(j, rt, stride=S5_L), :] for j in range(S5_L)], axis=1).astype(BF16)


def _s5_group_masks(width, per_group):
    grp = (lax.broadcasted_iota(jnp.int32, (1, width), 1) // per_group) % S5_GB
    return [grp == a for a in range(S5_GB)]


def _s5_state_kernel(u_ref, pw_ref, bb_ref, z_ref, w_ref):
    @pl.when(pl.program_id(1) == 0)
    def _():
        Q = S5_SW // 4
        masks = _s5_group_masks(Q, S5_STATE)
        for d in range(2):
            b_re, b_im = bb_ref[d, 0], bb_ref[d, 1]
            for j in range(S5_L):
                e = S5_L - 1 - j if d == 0 else j
                p_re, p_im = pw_ref[d, 0, e:e + 1, :], pw_ref[d, 1, e:e + 1, :]
                parts = ((p_re * b_re - p_im * b_im).astype(BF16), (p_re * b_im + p_im * b_re).astype(BF16))
                for ri in range(2):
                    q = 2 * d + ri
                    for a in range(S5_GB):
                        r0 = j * LANE + a * S5_CH
                        w_ref[r0:r0 + S5_CH, q * Q:(q + 1) * Q] = jnp.where(masks[a], parts[ri],
                                                                            jnp.zeros_like(parts[ri]))

    z_ref[...] = jnp.dot(_s5_slab(u_ref, z_ref.shape[0]), w_ref[...], preferred_element_type=F32)


def _s5_out_kernel(u_ref, p_ref, k_ref, c_ref, y_ref, wi_ref, wc_ref):
    @pl.when(pl.program_id(1) == 0)
    def _():
        Q = S5_SW // 4
        masks = _s5_group_masks(LANE, S5_CH)
        for i in range(S5_L):
            cols = slice(i * LANE, (i + 1) * LANE)
            for j in range(S5_L):
                blk = k_ref[0 if i >= j else 1, abs(i - j)]
                for a in range(S5_GB):
                    r0 = j * LANE + a * S5_CH
                    wi_ref[r0:r0 + S5_CH, cols] = jnp.where(masks[a], blk, jnp.zeros_like(blk))
            for q in range(4):
                blk = c_ref[q, i]
                for b in range(S5_GB):
                    r0 = q * Q + b * S5_STATE
                    wc_ref[r0:r0 + S5_STATE, cols] = jnp.where(masks[b], blk, jnp.zeros_like(blk))

    rt = p_ref.shape[0]
    y = (jnp.dot(_s5_slab(u_ref, rt), wi_ref[...], preferred_element_type=F32)
         + jnp.dot(p_ref[...].astype(BF16), wc_ref[...], preferred_element_type=F32))
    for i in range(S5_L):
        y_ref[pl.ds(i, rt, stride=S5_L), :] = y[:, i * LANE:(i + 1) * LANE]


def _s5_scan_kernel(z_ref, dec_ref, x0_ref, p_ref, fin_ref, *, segments):
    Q = S5_SW // 4
    dec = [(dec_ref[0, 2 * d:2 * d + 1, :], dec_ref[0, 2 * d + 1:2 * d + 2, :]) for d in range(2)]
    seq_base = 0
    for (row0, n_seq, n_chunk) in segments:
        def seq_loop(sq, carry, row0=row0, n_chunk=n_chunk, seq_base=seq_base):
            r0 = row0 + sq * n_chunk
            sid = seq_base + sq
            x_init = tuple((x0_ref[0, pl.ds(4 * sid + 2 * d, 1), :], x0_ref[0, pl.ds(4 * sid + 2 * d + 1, 1), :])
                           for d in range(2))

            def step(n, xs):
                out = []
                for d in range(2):
                    xr, xi = xs[d]
                    ar, ai = dec[d]
                    r = r0 + (n if d == 0 else n_chunk - 1 - n)
                    cr = slice(2 * d * Q, (2 * d + 1) * Q)
                    ci = slice((2 * d + 1) * Q, (2 * d + 2) * Q)
                    zr = z_ref[pl.ds(r, 1), cr]
                    zi = z_ref[pl.ds(r, 1), ci]
                    p_ref[pl.ds(r, 1), cr] = xr
                    p_ref[pl.ds(r, 1), ci] = xi
                    out.append((ar * xr - ai * xi + zr, ar * xi + ai * xr + zi))
                return tuple(out)

            xs = lax.fori_loop(0, n_chunk, step, x_init)
            for d in range(2):
                fin_ref[0, pl.ds(4 * sid + 2 * d, 1), :] = xs[d][0]
                fin_ref[0, pl.ds(4 * sid + 2 * d + 1, 1), :] = xs[d][1]
            return carry

        lax.fori_loop(0, n_seq, seq_loop, 0)
        seq_base += n_seq


def _s5(p, weights, x0, segments):
    kk, pw, bb, cw, dec = weights
    N = p.shape[0]
    NC = N // S5_L
    SG = S5_SG
    Q = S5_SW // 4
    rt = _pick(256, NC)
    nrow = x0.shape[1]
    cb = C_S5 // LANE
    u_spec = pl.BlockSpec((rt * S5_L, LANE), lambda s, r: (r, cb + s))
    slab_spec = pl.BlockSpec((rt, S5_SW), lambda s, r: (r, s))
    z = pl.pallas_call(
        _s5_state_kernel, grid=(SG, NC // rt),
        in_specs=[u_spec, pl.BlockSpec((2, 2, S5_L + 1, Q), lambda s, r: (0, 0, 0, s)),
                  pl.BlockSpec((2, 2, S5_CH, Q), lambda s, r: (0, 0, 0, s))], out_specs=slab_spec,
        out_shape=jax.ShapeDtypeStruct((NC, SG * S5_SW), F32),
        scratch_shapes=[pltpu.VMEM((S5_W, S5_SW), BF16)],
        compiler_params=_cp(("parallel", "arbitrary")), name="s5_chunk_state",
    )(p, pw, bb)
    state_spec = pl.BlockSpec((1, nrow, Q), lambda s: (s, 0, 0))
    carry_in, fin = pl.pallas_call(
        functools.partial(_s5_scan_kernel, segments=segments), grid=(SG,),
        in_specs=[pl.BlockSpec((NC, S5_SW), lambda s: (0, s)), pl.BlockSpec((1, 4, Q), lambda s: (s, 0, 0)),
                  state_spec],
        out_specs=[pl.BlockSpec((NC, S5_SW), lambda s: (0, s)), state_spec],
        out_shape=[jax.ShapeDtypeStruct((NC, SG * S5_SW), F32), jax.ShapeDtypeStruct((SG, nrow, Q), F32)],
        compiler_params=_cp(("parallel",)), name="s5_chunk_scan",
    )(z, dec, x0)
    y = pl.pallas_call(
        _s5_out_kernel, grid=(SG, NC // rt),
        in_specs=[u_spec, slab_spec, pl.BlockSpec((2, S5_L, S5_CH, LANE), lambda s, r: (0, 0, 0, s)),
                  pl.BlockSpec((4, S5_L, S5_STATE, LANE), lambda s, r: (0, 0, 0, s))],
        out_specs=pl.BlockSpec((rt * S5_L, LANE), lambda s, r: (r, s)),
        out_shape=jax.ShapeDtypeStruct((N, S5_WIDTH), F32),
        scratch_shapes=[pltpu.VMEM((S5_W, S5_W), BF16), pltpu.VMEM((S5_SW, S5_W), BF16)],
        compiler_params=_cp(("parallel", "arbitrary"), 56), name="s5_chunk_out",
    )(p, carry_in, kk, cw)
    return y, fin


def _s5_post_kernel(y_ref, u_ref, d_ref, w_ref, b_ref, o_ref):
    yg = jax.nn.gelu(y_ref[...] + d_ref[...] * u_ref[...])
    z = jnp.dot(yg.astype(BF16), w_ref[...], preferred_element_type=F32) + b_ref[...]
    o_ref[...] = (yg * jax.nn.sigmoid(z)).astype(o_ref.dtype)


def _s5_post(y, p, d, w, b, tm):
    N = y.shape[0]
    W = S5_WIDTH
    return pl.pallas_call(
        _s5_post_kernel, grid=(N // tm,),
        in_specs=[pl.BlockSpec((tm, W), lambda i: (i, 0)), pl.BlockSpec((tm, W), lambda i: (i, C_S5 // W)),
                  pl.BlockSpec((1, W), lambda i: (0, 0)), pl.BlockSpec((W, W), lambda i: (0, 0)),
                  pl.BlockSpec((1, W), lambda i: (0, 0))],
        out_specs=pl.BlockSpec((tm, W), lambda i: (i, 0)),
        out_shape=jax.ShapeDtypeStruct((N, W), BF16),
        compiler_params=_cp(("parallel",)), name="s5_glu",
    )(y, p, d, w, b)


def _merge_kernel(h_ref, a_ref, b_ref, c_ref, wg0, wg1, wg2, bg0, bg1, bg2, wb_ref, o_ref):
    h = h_ref[...]
    acc = None
    for i, (br, wg, bg) in enumerate(((a_ref, wg0, bg0), (b_ref, wg1, bg1), (c_ref, wg2, bg2))):
        gate = jax.nn.sigmoid(jnp.dot(h, wg[...], preferred_element_type=F32) + bg[...])
        t = gate * jnp.dot(br[...], wb_ref[i], preferred_element_type=F32)
        acc = t if acc is None else acc + t
    o_ref[...] = acc.astype(o_ref.dtype)


def _merge(h, o_da, o_gla, o_s5, w_gate, b_gate, w_branch, tm, tn):
    N, D = h.shape
    W = o_da.shape[1]
    nt = D // tn
    act = pl.BlockSpec((tm, W), lambda j, i: (i, 0))
    in_specs = [pl.BlockSpec((tm, D), lambda j, i: (i, 0)), act, act, act]
    in_specs += [pl.BlockSpec((D, tn), lambda j, i, br=br: (0, br * nt + j)) for br in range(N_BRANCH)]
    in_specs += [pl.BlockSpec((1, tn), lambda j, i, br=br: (0, br * nt + j)) for br in range(N_BRANCH)]
    in_specs += [pl.BlockSpec((N_BRANCH, W, tn), lambda j, i: (0, 0, j))]
    return pl.pallas_call(
        _merge_kernel, grid=(nt, N // tm), in_specs=in_specs,
        out_specs=pl.BlockSpec((tm, tn), lambda j, i: (i, j)),
        out_shape=jax.ShapeDtypeStruct((N, D), BF16),
        compiler_params=_cp(("parallel", "parallel"), 56), name="gated_merge",
    )(h, o_da, o_gla, o_s5, w_gate, w_gate, w_gate, b_gate, b_gate, b_gate, w_branch)


def _wout_kernel(m_ref, w_ref, x_ref, g_ref, o_ref):
    o_ref[...] = x_ref[...] + g_ref[0] * jnp.dot(m_ref[...], w_ref[...], preferred_element_type=F32)


def _wout(merged, w, x, gate, midx, tm, tn):
    N, D = x.shape
    return pl.pallas_call(
        _wout_kernel, grid=(D // tn, N // tm),
        in_specs=[pl.BlockSpec((tm, D), lambda j, i: (i, 0)), pl.BlockSpec((D, tn), lambda j, i: (0, j)),
                  pl.BlockSpec((tm, tn), lambda j, i: (i, j)), pl.BlockSpec((1, 1, tn), lambda j, i: (midx(i), 0, j))],
        out_specs=pl.BlockSpec((tm, tn), lambda j, i: (i, j)),
        out_shape=jax.ShapeDtypeStruct((N, D), F32),
        compiler_params=_cp(("parallel", "parallel")), name="out_proj_residual",
    )(merged, w, x, gate)


def _route_kernel(lg_ref, bias_ref, asg_ref, tile_ref, rank_ref, *, tl):
    E, n_tok = lg_ref.shape
    EPG = EXPERTS_PER_GROUP
    ninf = -jnp.inf
    eio = lax.broadcasted_iota(jnp.int32, (E, tl), 0)
    upper = jnp.where(lax.broadcasted_iota(jnp.int32, (tl, tl), 0) < lax.broadcasted_iota(jnp.int32, (tl, tl), 1),
                      1.0, 0.0).astype(BF16)
    ecol = lax.broadcasted_iota(jnp.int32, (E, 1), 0)

    def select(j, carry):
        ls = pl.ds(pl.multiple_of(j * tl, tl), tl)
        s = jax.nn.sigmoid(lg_ref[:, ls])
        sel = s + bias_ref[...]
        rows = [sel[e:e + 1, :] for e in range(E)]
        srow = [s[e:e + 1, :] for e in range(E)]
        gi = jnp.zeros((1, tl), jnp.int32)
        best = None
        for g in range(N_EXPERT_GROUPS):
            a, b, c, d = rows[g * EPG:(g + 1) * EPG]
            hi1, lo1, hi2, lo2 = jnp.maximum(a, b), jnp.minimum(a, b), jnp.maximum(c, d), jnp.minimum(c, d)
            gs = jnp.maximum(hi1, hi2) + jnp.maximum(jnp.minimum(hi1, hi2), jnp.maximum(lo1, lo2))
            if best is None:
                best = gs
            else:
                upd = gs > best
                gi = jnp.where(upd, g, gi)
                best = jnp.where(upd, gs, best)

        def pick(rs, i):
            out = rs[(N_EXPERT_GROUPS - 1) * EPG + i]
            for g in range(N_EXPERT_GROUPS - 2, -1, -1):
                out = jnp.where(gi == g, rs[g * EPG + i], out)
            return out

        v = [pick(rows, i) for i in range(EPG)]
        sv = [pick(srow, i) for i in range(EPG)]
        i1, b1, s1 = jnp.zeros((1, tl), jnp.int32), v[0], sv[0]
        for i in range(1, EPG):
            upd = v[i] > b1
            i1, b1, s1 = jnp.where(upd, i, i1), jnp.where(upd, v[i], b1), jnp.where(upd, sv[i], s1)
        i2, b2, s2 = jnp.zeros((1, tl), jnp.int32), jnp.full((1, tl), ninf, F32), sv[0]
        for i in range(EPG):
            cand = jnp.where(i1 == i, ninf, v[i])
            upd = cand > b2
            i2, b2, s2 = jnp.where(upd, i, i2), jnp.where(upd, cand, b2), jnp.where(upd, sv[i], s2)
        e0 = gi * EPG + i1
        e1 = gi * EPG + i2
        den = s1 + s2
        asg_ref[2:3, ls] = s1 / den
        asg_ref[3:4, ls] = s2 / den
        asg_ref[4:5, ls] = e0.astype(F32)
        asg_ref[5:6, ls] = e1.astype(F32)
        m = jnp.where((eio == e0) | (eio == e1), 1.0, 0.0)
        rank_ref[:, ls] = jnp.dot(m.astype(BF16), upper, preferred_element_type=F32) + carry
        return carry + jnp.sum(m, axis=1, keepdims=True)

    counts = lax.fori_loop(0, n_tok // tl, select, jnp.zeros((E, 1), F32))
    ptiles = jnp.floor((counts + (MOE_TM - 1)) * (1.0 / MOE_TM))
    run = jnp.zeros((1, 1), F32)
    pstart_t = jnp.zeros((E, 1), F32)
    for e in range(E):
        pstart_t = jnp.where(ecol == e, run, pstart_t)
        run = run + ptiles[e:e + 1, :]
    pstart = pstart_t * MOE_TM

    def place(j, _):
        ls = pl.ds(pl.multiple_of(j * tl, tl), tl)
        dest = rank_ref[:, ls] + pstart
        e0 = asg_ref[4:5, ls].astype(jnp.int32)
        e1 = asg_ref[5:6, ls].astype(jnp.int32)
        asg_ref[0:1, ls] = jnp.sum(jnp.where(eio == e0, dest, 0.0), axis=0, keepdims=True)
        asg_ref[1:2, ls] = jnp.sum(jnp.where(eio == e1, dest, 0.0), axis=0, keepdims=True)
        asg_ref[6:8, ls] = jnp.zeros((2, tl), F32)
        return 0

    lax.fori_loop(0, n_tok // tl, place, 0)

    ntp = tile_ref.shape[1]
    pend_t = pstart_t + ptiles
    ti = lax.broadcasted_iota(jnp.int32, (1, ntp), 1).astype(F32)
    te = jnp.sum(jnp.where(pend_t <= ti, 1.0, 0.0), axis=0, keepdims=True)
    used = ti < run
    last_e = jnp.max(jnp.where(counts > 0.0, ecol.astype(F32), 0.0), axis=0, keepdims=True)
    te = jnp.where(used, jnp.minimum(te, E - 1.0), last_e)
    tile_ref[0:1, :] = te.astype(jnp.int32)
    tile_ref[1:2, :] = jnp.where(used, 1, 0).astype(jnp.int32)
    tile_ref[2:8, :] = jnp.zeros((6, ntp), jnp.int32)


def _route(logits_t, router_bias, ntile):
    E, N = logits_t.shape
    assert MOE_TM & (MOE_TM - 1) == 0
    ntp = -(-ntile // LANE) * LANE
    full = lambda shape: pl.BlockSpec(shape, lambda i: (0, 0))
    return pl.pallas_call(
        functools.partial(_route_kernel, tl=_pick(512, N)), grid=(1,),
        in_specs=[full((E, N)), full((E, 1))],
        out_specs=[full((8, N)), full((8, ntp))],
        out_shape=[jax.ShapeDtypeStruct((8, N), F32), jax.ShapeDtypeStruct((8, ntp), jnp.int32)],
        scratch_shapes=[pltpu.VMEM((E, N), F32)],
        compiler_params=_cp(("arbitrary",)), name="moe_route",
    )(logits_t, router_bias.astype(F32).reshape(E, 1))


def _moe_kernel(te_ref, tv_ref, x_ref, w1_ref, w3_ref, w2_ref, o_ref):
    valid = tv_ref[pl.program_id(0)] != 0

    @pl.when(valid)
    def _():
        x = x_ref[...]
        F = w1_ref.shape[2]
        acc = None
        for f0 in range(0, F, MOE_FC):
            fs = slice(f0, min(f0 + MOE_FC, F))
            a = jnp.dot(x, w1_ref[0, :, fs], preferred_element_type=F32)
            b = jnp.dot(x, w3_ref[0, :, fs], preferred_element_type=F32)
            hid = ((a * jax.nn.sigmoid(a)) * b).astype(BF16)
            t = jnp.dot(hid, w2_ref[0, fs, :], preferred_element_type=F32)
            acc = t if acc is None else acc + t
        o_ref[...] = acc.astype(o_ref.dtype)

    @pl.when(jnp.logical_not(valid))
    def _():
        o_ref[...] = jnp.zeros(o_ref.shape, o_ref.dtype)


def _moe(xin, w1, w3, w2, layer, tile_e, tile_valid):
    cap, D = xin.shape
    F = w1.shape[3]
    TM = MOE_TM
    grid_spec = pltpu.PrefetchScalarGridSpec(
        num_scalar_prefetch=2, grid=(cap // TM,),
        in_specs=[pl.BlockSpec((TM, D), lambda i, te, tv: (i, 0)),
                  pl.BlockSpec((None, 1, D, F), lambda i, te, tv: (layer, te[i], 0, 0)),
                  pl.BlockSpec((None, 1, D, F), lambda i, te, tv: (layer, te[i], 0, 0)),
                  pl.BlockSpec((None, 1, F, D), lambda i, te, tv: (layer, te[i], 0, 0))],
        out_specs=pl.BlockSpec((TM, D), lambda i, te, tv: (i, 0)))
    return pl.pallas_call(
        _moe_kernel, grid_spec=grid_spec,
        out_shape=jax.ShapeDtypeStruct((cap, D), BF16),
        compiler_params=_cp(("arbitrary",), 58), name="moe_experts",
    )(tile_e, tile_valid, xin, w1, w3, w2)


def _combine_kernel(x_ref, y0_ref, y1_ref, w0_ref, w1_ref, g_ref, o_ref):
    o_ref[...] = x_ref[...] + g_ref[0] * (y0_ref[...] * w0_ref[...] + y1_ref[...] * w1_ref[...])


def _combine(x, y0, y1, w0, w1, gate, midx, tm):
    N, D = x.shape
    blk = pl.BlockSpec((tm, D), lambda i: (i, 0))
    col = pl.BlockSpec((tm, 1), lambda i: (i, 0))
    return pl.pallas_call(
        _combine_kernel, grid=(N // tm,),
        in_specs=[blk, blk, blk, col, col, pl.BlockSpec((1, 1, D), lambda i: (midx(i), 0, 0))],
        out_specs=blk, out_shape=jax.ShapeDtypeStruct((N, D), F32),
        compiler_params=_cp(("parallel",)), name="moe_residual",
    )(x, y0, y1, w0, w1, gate)


def _rope_tables(n_tokens):
    rows = n_tokens // GRID_W
    row = jnp.repeat(jnp.arange(rows, dtype=F32), GRID_W)
    col = jnp.tile(jnp.arange(GRID_W, dtype=F32), rows)
    inv = ROPE_BASE ** (-jnp.arange(0, AX_DIM, 2, dtype=F32) / AX_DIM)
    ar = row[:, None] * inv[None, :]
    ac = col[:, None] * inv[None, :]
    ang = jnp.tile(jnp.concatenate([ar, ar, ac, ac], axis=-1), (1, LANE // DA_HEAD))
    first = (jnp.arange(LANE) % (AX_DIM)) < (AX_DIM // 2)
    return jnp.cos(ang), jnp.where(first[None, :], -jnp.sin(ang), jnp.sin(ang))


def kernel(x_prompt, x_sample, cache_k, cache_v, state_gla, state_s5_re, state_s5_im, c, c_ctx, w_mod, b_mod, g_norm, w_in, da_lambda_q, da_lambda_k, da_subln_g, gla_w_alpha, gla_b_alpha, gla_norm_g, s5_a_re, s5_a_im, s5_log_dt, s5_b_re, s5_b_im, s5_c_re, s5_c_im, s5_d, s5_w_glu, s5_b_glu, w_branch, w_gate, b_gate, w_out, router_w, router_bias, moe_w1, moe_w3, moe_w2, g_final):
    B, T, D = x_prompt.shape
    DB, DT, _ = x_sample.shape
    L = w_in.shape[0]
    PAST = cache_k.shape[2]
    NP, NS = B * T, DB * DT
    N = NP + NS
    nseq = B + DB
    assert D == D_MODEL and 1 + DB <= 8 and NP % DT == 0 and T % GLA_ROWS == 0 and DT % GLA_ROWS == 0
    tm = _pick(512, NP, DT)
    midx = functools.partial(_mod_index, tm=tm, n_ctx_tok=NP, dec_seq=DT)

    x = jnp.concatenate([x_prompt.reshape(NP, D), x_sample.reshape(NS, D)], axis=0)
    cv = jnp.zeros((8, D), F32).at[0].set(c_ctx).at[1:1 + DB].set(c)
    mods = _mod(cv, w_mod, b_mod).reshape(L, 8, N_MOD, 1, D)
    zero_mod = jnp.zeros((8, 1, D), F32)

    rope_tabs = _rope_tables(DT)
    ck = cache_k.reshape(DB, L, PAST, DA_HEADS * 2 * DA_HEAD)
    cvv = cache_v.reshape(DB, L, PAST, DA_HEADS * DA_VDIM)
    r = jnp.arange(GLA_ROWS)
    same = (r[:, None] // GLA_CHUNK) == (r[None, :] // GLA_CHUNK)
    tri = jnp.stack([same & (r[None, :] <= r[:, None]), same & (r[None, :] >= r[:, None])]).astype(BF16)
    rw = router_w.astype(F32).T
    rw_hi = rw.astype(BF16)
    rw_lo = (rw - rw_hi.astype(F32)).astype(BF16)
    ntile = N * TOP_K // MOE_TM + N_EXPERTS
    tok2 = jnp.tile(jnp.arange(N, dtype=jnp.int32), TOP_K)
    moe_w1_bf, moe_w3_bf, moe_w2_bf = _cast_bf16(moe_w1), _cast_bf16(moe_w3), _cast_bf16(moe_w2)
    tq_ctx = _pick(256, T)
    tq_lat = _pick(256, DT)
    s5_segments = ((0, B, T // S5_L), (NP // S5_L, DB, DT // S5_L))

    g_list, re_list, im_list = [], [], []
    for l in range(L):
        m = mods[l]
        shift1, scale1, gate1, shift2, scale2, gate2 = (m[:, i] for i in range(N_MOD))
        wi = w_in[l]
        w_in_p = jnp.concatenate([wi[:, :C_S5], wi[:, C_S5 + GLA_RANK:], wi[:, C_S5:C_S5 + GLA_RANK],
                                  jnp.zeros((D, P_COLS - C_GA - GLA_RANK), F32)], axis=1).astype(BF16)
        h = _norm(x, g_norm[l, 0][None], scale1, shift1, midx, tm, BF16)
        p = _matmul(h, w_in_p, tm, 1536, F32, "in_proj")

        lam_init = 0.8 - 0.6 * math.exp(-0.3 * l)
        lq, lk = da_lambda_q[l], da_lambda_k[l]
        lam = jnp.exp(jnp.sum(lq[0] * lk[0])) - jnp.exp(jnp.sum(lq[1] * lk[1])) + lam_init
        lam_row = jnp.full((1, LANE), lam, F32)
        g_row = (da_subln_g[l] * (1.0 - lam_init))[None]
        o_da = jnp.concatenate([
            _attention(p, lam_row, g_row, 0, B, T, tq_ctx, DA_HEADS, _pick(ATT_KB, T)),
            _attention_latent(p, lam_row, g_row, NP, DB, DT, tq_lat, rope_tabs, ck, cvv, l)], axis=0)

        wa = jnp.zeros((1, 2, LANE, GLA_KTOT), F32).at[0, :, :GLA_RANK].set(gla_w_alpha[l]).astype(BF16)
        ba = gla_b_alpha[l].reshape(1, 2, 1, GLA_KTOT)
        o_f, o_b, gla_fin = _gla(p, wa, ba, tri, state_gla, l, ((B, T), (DB, DT)))
        o_gla = _gla_post(o_f, o_b, p, gla_norm_g[l][None], tm)

        mats = _s5_matrices(s5_a_re[l], s5_a_im[l], s5_log_dt[l], s5_b_re[l], s5_b_im[l], s5_c_re[l], s5_c_im[l])
        x0 = jnp.stack([state_s5_re[:, l], state_s5_im[:, l]], axis=2)
        x0 = jnp.concatenate([jnp.zeros((B,) + x0.shape[1:], F32), x0], axis=0)
        x0 = x0.reshape(4 * nseq, S5_SG, S5_SW // 4).transpose(1, 0, 2)
        y, s5_fin = _s5(p, mats, x0, s5_segments)
        o_s5 = _s5_post(y, p, s5_d[l][None], s5_w_glu[l].astype(BF16), s5_b_glu[l][None], tm)
        s5_fin = s5_fin.transpose(1, 0, 2).reshape(nseq, 2, 2, S5_GROUPS, S5_STATE)[:B]

        merged = _merge(h, o_da, o_gla, o_s5, w_gate[l].astype(BF16), b_gate[l][None], w_branch[l].astype(BF16),
                        tm, 512)
        x = _wout(merged, w_out[l].astype(BF16), x, gate1, midx, tm, 1024)

        h2, logits_t = _norm(x, g_norm[l, 1][None], scale2, shift2, midx, tm, BF16, router=(rw_hi, rw_lo))
        asg, tiles = _route(logits_t, router_bias, ntile)
        slot = asg[0:2].astype(jnp.int32)
        buf_tok = jnp.zeros((ntile * MOE_TM,), jnp.int32).at[slot.reshape(-1)].set(tok2, unique_indices=True)
        xin = jnp.take(h2, buf_tok, axis=0, mode="clip")
        ye = _moe(xin, moe_w1_bf, moe_w3_bf, moe_w2_bf, l, tiles[0, :ntile], tiles[1, :ntile])
        x = _combine(x, jnp.take(ye, slot[0], axis=0, mode="clip"), jnp.take(ye, slot[1], axis=0, mode="clip"),
                     asg[2].reshape(N, 1),
                     asg[3].reshape(N, 1), gate2, midx, tm)

        kbuf, vbuf = _kv_store(p, None if l == 0 else (kbuf, vbuf), l, L, B, T)
        g_list.append(gla_fin[:B])
        re_list.append(s5_fin[:, :, 0])
        im_list.append(s5_fin[:, :, 1])

    y_ctx = _norm(x, g_final[None], zero_mod, zero_mod, midx, tm, F32, rows=(0, NP))
    y_lat = _norm(x, g_final[None], zero_mod, zero_mod, midx, tm, F32, rows=(NP, NS))
    return (y_ctx.reshape(B, T, D), y_lat.reshape(DB, DT, D),
            kbuf.reshape(B, L, T, DA_HEADS, 2 * DA_HEAD), vbuf.reshape(B, L, T, DA_HEADS, DA_VDIM),
            jnp.stack(g_list, axis=1),
            jnp.stack(re_list, axis=1), jnp.stack(im_list, axis=1))
```

```python
import functools
import math

import jax
import jax.numpy as jnp
from jax import lax
from jax.experimental import pallas as pl
from jax.experimental.pallas import tpu as pltpu

F32 = jnp.float32
BF16 = jnp.bfloat16

D_MODEL = 2048
GRID_W = 64
EPS = 1e-6
N_MOD = 6
N_BRANCH = 3
DA_HEADS = 8
DA_HEAD = 64
DA_VDIM = 128
ROPE_BASE = 10000.0
AX_DIM = DA_HEAD // 2
GLA_HEADS = 4
GLA_DK = 128
GLA_DV = 256
GLA_KTOT = GLA_HEADS * GLA_DK
GLA_WIDTH = GLA_HEADS * GLA_DV
GLA_RANK = 16
GLA_TAU = 16.0
GLA_CHUNK = 64
S5_WIDTH = 1024
S5_CH = 16
S5_GROUPS = S5_WIDTH // S5_CH
S5_STATE = 64
N_EXPERTS = 16
N_EXPERT_GROUPS = 4
EXPERTS_PER_GROUP = N_EXPERTS // N_EXPERT_GROUPS
TOPK_GROUP = 1
TOP_K = 2
D_FF_EXPERT = 1408

LANE = 128
C_Q, C_K, C_V, C_GQ, C_GK, C_GV, C_GR, C_S5, C_GA = 0, 1024, 2048, 3072, 3584, 4096, 5120, 6144, 7168
P_COLS = 7680
GLA_ROWS = 256
S5_L = 16
S5_SG = S5_WIDTH // LANE
S5_GB = S5_GROUPS // S5_SG
S5_W = S5_L * LANE
S5_SW = 4 * S5_GB * S5_STATE
MOE_TM = 512
MOE_FC = 512
ATT_KB = 512
VMEM_MB = 1024 * 1024


def _cp(sem, vmem_mb=48):
    return pltpu.CompilerParams(dimension_semantics=sem, vmem_limit_bytes=vmem_mb * VMEM_MB)


def _pick(pref, *ns):
    t = pref
    while any(n % t for n in ns):
        t //= 2
    return t


def _mod_index(i, tm, n_ctx_tok, dec_seq):
    t0 = i * tm
    return jnp.where(t0 < n_ctx_tok, 0, 1 + (t0 - n_ctx_tok) // dec_seq)


def _mod_kernel(c_ref, w_ref, b_ref, o_ref):
    c = c_ref[...]
    s = (c * jax.nn.sigmoid(c)).astype(BF16)
    o_ref[0] = jnp.dot(s, w_ref[0].astype(BF16), preferred_element_type=F32) + b_ref[0]


def _mod(cv, w_mod, b_mod):
    L, D, M = w_mod.shape
    tn = 1024
    return pl.pallas_call(
        _mod_kernel,
        grid=(L, M // tn),
        in_specs=[pl.BlockSpec((8, D), lambda l, j: (0, 0)),
                  pl.BlockSpec((1, D, tn), lambda l, j: (l, 0, j)),
                  pl.BlockSpec((1, 1, tn), lambda l, j: (l, 0, j))],
        out_specs=pl.BlockSpec((1, 8, tn), lambda l, j: (l, 0, j)),
        out_shape=jax.ShapeDtypeStruct((L, 8, M), F32),
        compiler_params=_cp(("parallel", "parallel")),
        name="mod_vectors",
    )(cv, w_mod, b_mod.reshape(L, 1, M))


def _cast_kernel(x_ref, o_ref):
    o_ref[...] = x_ref[...].astype(o_ref.dtype)


def _cast_bf16(w):
    lead, (R, C) = w.shape[:-2], w.shape[-2:]
    n = math.prod(lead)
    out = pl.pallas_call(
        _cast_kernel, grid=(n,),
        in_specs=[pl.BlockSpec((1, R, C), lambda i: (i, 0, 0))], out_specs=pl.BlockSpec((1, R, C), lambda i: (i, 0, 0)),
        out_shape=jax.ShapeDtypeStruct((n, R, C), BF16),
        compiler_params=_cp(("parallel",)), name="cast_bf16",
    )(w.reshape(n, R, C))
    return out.reshape(lead + (R, C))


def _norm_core(x_ref, g_ref, sc_ref, sh_ref):
    x = x_ref[...]
    y = x * lax.rsqrt(jnp.mean(x * x, axis=-1, keepdims=True) + EPS)
    return (y * g_ref[...]) * (1.0 + sc_ref[0]) + sh_ref[0]


def _norm_kernel(x_ref, g_ref, sc_ref, sh_ref, o_ref):
    o_ref[...] = _norm_core(x_ref, g_ref, sc_ref, sh_ref).astype(o_ref.dtype)


def _norm_router_kernel(x_ref, g_ref, sc_ref, sh_ref, rh_ref, rl_ref, o_ref, lg_ref):
    h = _norm_core(x_ref, g_ref, sc_ref, sh_ref)
    hh = h.astype(BF16)
    hl = (h - hh.astype(F32)).astype(BF16)
    o_ref[...] = hh
    nt = (((1,), (1,)), ((), ()))
    lg_ref[...] = (lax.dot_general(rh_ref[...], hh, nt, preferred_element_type=F32)
                   + lax.dot_general(rh_ref[...], hl, nt, preferred_element_type=F32)
                   + lax.dot_general(rl_ref[...], hh, nt, preferred_element_type=F32))


def _norm(x, g, sc, sh, midx, tm, out_dtype, router=None, rows=None):
    N, D = x.shape
    r0, n = (0, N) if rows is None else rows
    t0 = r0 // tm
    in_specs = [pl.BlockSpec((tm, D), lambda i: (t0 + i, 0)),
                pl.BlockSpec((1, D), lambda i: (0, 0)),
                pl.BlockSpec((1, 1, D), lambda i: (midx(t0 + i), 0, 0)),
                pl.BlockSpec((1, 1, D), lambda i: (midx(t0 + i), 0, 0))]
    if router is None:
        return pl.pallas_call(
            _norm_kernel, grid=(n // tm,), in_specs=in_specs,
            out_specs=pl.BlockSpec((tm, D), lambda i: (i, 0)),
            out_shape=jax.ShapeDtypeStruct((n, D), out_dtype),
            compiler_params=_cp(("parallel",)), name="norm_mod",
        )(x, g, sc, sh)
    assert rows is None
    rh, rl = router
    E = rh.shape[0]
    in_specs += [pl.BlockSpec((E, D), lambda i: (0, 0)), pl.BlockSpec((E, D), lambda i: (0, 0))]
    return pl.pallas_call(
        _norm_router_kernel, grid=(N // tm,), in_specs=in_specs,
        out_specs=[pl.BlockSpec((tm, D), lambda i: (i, 0)), pl.BlockSpec((E, tm), lambda i: (0, i))],
        out_shape=[jax.ShapeDtypeStruct((N, D), BF16), jax.ShapeDtypeStruct((E, N), F32)],
        compiler_params=_cp(("parallel",)), name="norm_mod_router",
    )(x, g, sc, sh, rh, rl)


def _mm_kernel(a_ref, w_ref, o_ref):
    o_ref[...] = jnp.dot(a_ref[...], w_ref[...], preferred_element_type=F32).astype(o_ref.dtype)


def _matmul(a, w, tm, tn, out_dtype, name):
    M, K = a.shape
    Nc = w.shape[1]
    return pl.pallas_call(
        _mm_kernel, grid=(Nc // tn, M // tm),
        in_specs=[pl.BlockSpec((tm, K), lambda j, i: (i, 0)), pl.BlockSpec((K, tn), lambda j, i: (0, j))],
        out_specs=pl.BlockSpec((tm, tn), lambda j, i: (i, j)),
        out_shape=jax.ShapeDtypeStruct((M, Nc), out_dtype),
        compiler_params=_cp(("parallel", "parallel")), name=name,
    )(a, w)


def _kv_store_kernel(k_ref, v_ref, *refs):
    ko_ref, vo_ref = refs[-2:]
    ko_ref[...] = k_ref[...]
    vo_ref[...] = v_ref[...]


def _kv_store(p, kbuf, vbuf, layer, B, T):
    W = DA_HEADS * DA_VDIM
    in_specs = [pl.BlockSpec((T, W), lambda b: (b, C_K // W)), pl.BlockSpec((T, W), lambda b: (b, C_V // W)),
                pl.BlockSpec(memory_space=pl.ANY), pl.BlockSpec(memory_space=pl.ANY)]
    out_spec = pl.BlockSpec((None, None, T, W), lambda b: (b, layer, 0, 0))
    shape = jax.ShapeDtypeStruct(kbuf.shape, F32)
    return pl.pallas_call(
        _kv_store_kernel, grid=(B,), in_specs=in_specs, out_specs=[out_spec, out_spec], out_shape=[shape, shape],
        input_output_aliases={2: 0, 3: 1}, compiler_params=_cp(("parallel",)), name="kv_store",
    )(p, p, kbuf, vbuf)


def _rope(x, cos, sin_signed):
    lane = lax.broadcasted_iota(jnp.int32, (1, LANE), 1)
    first = (lane % (2 * (AX_DIM // 2))) < (AX_DIM // 2)
    xr = jnp.where(first, pltpu.roll(x, LANE - AX_DIM // 2, 1), pltpu.roll(x, AX_DIM // 2, 1))
    return x * cos + xr * sin_signed


def _attn_kernel(*refs, rope, ctx, hs, kb):
    refs = list(refs)
    q_ref, k_ref, v_ref = refs[:3]
    refs = refs[3:]
    if ctx:
        ck_ref, cv_ref = refs[:2]
        refs = refs[2:]
    if rope:
        cq_ref, sq_ref, ckk_ref, skk_ref = refs[:4]
        refs = refs[4:]
    lam_ref, g_ref, o_ref, kb_ref, vb_ref = refs[:5]
    if ctx:
        ckb_ref, cvb_ref = refs[5:]

    hsl = [slice(h * LANE, (h + 1) * LANE) for h in range(hs)]

    @pl.when(pl.program_id(2) == 0)
    def _():
        for h in range(hs):
            k = k_ref[:, hsl[h]]
            if rope:
                k = _rope(k, ckk_ref[...], skk_ref[...])
            kb_ref[:, hsl[h]] = k.astype(BF16)
        vb_ref[...] = v_ref[...].astype(BF16)
        if ctx:
            ckb_ref[...] = ck_ref[...].astype(BF16)
            cvb_ref[...] = cv_ref[...].astype(BF16)

    tq = q_ref.shape[0]
    n_new = kb_ref.shape[0]
    lane = lax.broadcasted_iota(jnp.int32, (1, LANE), 1)
    lo = lane < DA_HEAD
    nt = (((1,), (1,)), ((), ()))
    qqs = []
    for h in range(hs):
        q = q_ref[:, hsl[h]]
        if rope:
            q = _rope(q, cq_ref[...], sq_ref[...])
        q = q * (DA_HEAD ** -0.5 * math.log2(math.e))
        qqs.append(jnp.concatenate([jnp.where(lo, q, 0.0), jnp.where(lo, 0.0, q)], axis=0).astype(BF16))

    def block(qq, kblk, vblk, carry):
        m, l, acc = carry
        s = lax.dot_general(qq, kblk, nt, preferred_element_type=F32)
        mn = jnp.maximum(m, jnp.max(s, axis=-1, keepdims=True))
        e = jnp.exp2(s - mn)
        r = jnp.exp2(m - mn)
        l = r * l + jnp.sum(e, axis=-1, keepdims=True)
        acc = r * acc + jnp.dot(e.astype(BF16), vblk, preferred_element_type=F32)
        return mn, l, acc

    def all_heads(kref, vref, rows, carries):
        return tuple(block(qqs[h], kref[rows, hsl[h]], vref[rows, hsl[h]], carries[h]) for h in range(hs))

    carries = tuple((jnp.full((2 * tq, 1), -jnp.inf, F32), jnp.zeros((2 * tq, 1), F32),
                     jnp.zeros((2 * tq, LANE), F32)) for _ in range(hs))
    if ctx:
        for c in range(ckb_ref.shape[0] // kb):
            carries = all_heads(ckb_ref, cvb_ref, slice(c * kb, (c + 1) * kb), carries)
    if n_new // kb == 1:
        carries = all_heads(kb_ref, vb_ref, slice(None), carries)
    else:
        def body(j, carries):
            return all_heads(kb_ref, vb_ref, pl.ds(pl.multiple_of(j * kb, kb), kb), carries)
        carries = lax.fori_loop(0, n_new // kb, body, carries)
    for h in range(hs):
        _, l, acc = carries[h]
        o = acc[:tq] * (1.0 / l[:tq]) - acc[tq:] * (lam_ref[:, :1] / l[tq:])
        y = o * lax.rsqrt(jnp.mean(o * o, axis=-1, keepdims=True) + EPS)
        o_ref[:, hsl[h]] = (y * g_ref[...]).astype(o_ref.dtype)


def _attention(p, lam_row, g_row, row0, B, T, tq, hs, kb, rope_tabs=None, ctx=None):
    nq = T // tq
    rb0 = row0 // tq
    kb0 = row0 // T
    W = hs * LANE
    hq, hk, hv = C_Q // W, C_K // W, C_V // W
    assert T % kb == 0
    in_specs = [pl.BlockSpec((tq, W), lambda b, h, i: (rb0 + b * nq + i, hq + h)),
                pl.BlockSpec((T, W), lambda b, h, i: (kb0 + b, hk + h)),
                pl.BlockSpec((T, W), lambda b, h, i: (kb0 + b, hv + h))]
    args = [p, p, p]
    scratch = [pltpu.VMEM((T, W), BF16), pltpu.VMEM((T, W), BF16)]
    if ctx is not None:
        ck, cv, layer = ctx
        past = ck.shape[2]
        assert past % kb == 0
        in_specs += [pl.BlockSpec((None, None, past, W), lambda b, h, i: (b, layer, 0, h)),
                     pl.BlockSpec((None, None, past, W), lambda b, h, i: (b, layer, 0, h))]
        args += [ck, cv]
        scratch += [pltpu.VMEM((past, W), BF16), pltpu.VMEM((past, W), BF16)]
    if rope_tabs is not None:
        cos, sin = rope_tabs
        in_specs += [pl.BlockSpec((tq, LANE), lambda b, h, i: (i, 0)), pl.BlockSpec((tq, LANE), lambda b, h, i: (i, 0)),
                     pl.BlockSpec((T, LANE), lambda b, h, i: (0, 0)), pl.BlockSpec((T, LANE), lambda b, h, i: (0, 0))]
        args += [cos, sin, cos, sin]
    in_specs += [pl.BlockSpec((1, LANE), lambda b, h, i: (0, 0)), pl.BlockSpec((1, LANE), lambda b, h, i: (0, 0))]
    args += [lam_row, g_row]
    return pl.pallas_call(
        functools.partial(_attn_kernel, rope=rope_tabs is not None, ctx=ctx is not None, hs=hs, kb=kb),
        grid=(B, DA_HEADS // hs, nq),
        in_specs=in_specs,
        out_specs=pl.BlockSpec((tq, W), lambda b, h, i: (b * nq + i, h)),
        out_shape=jax.ShapeDtypeStruct((B * T, DA_HEADS * DA_VDIM), BF16),
        scratch_shapes=scratch,
        compiler_params=_cp(("parallel", "parallel", "arbitrary"), 56),
        name="diff_attention_latent" if ctx is not None else "diff_attention_context",
    )(*args)


def _attn_latent_kernel(q_ref, k_ref, v_ref, ck_ref, cv_ref, cq_ref, sq_ref, ckk_ref, skk_ref, lam_ref, g_ref, o_ref,
                        kb_ref, vb_ref):
    @pl.when(pl.program_id(2) == 0)
    def _():
        kb_ref[...] = _rope(k_ref[...], ckk_ref[...], skk_ref[...]).astype(BF16)
        vb_ref[...] = v_ref[...].astype(BF16)

    q = _rope(q_ref[...], cq_ref[...], sq_ref[...]) * (DA_HEAD ** -0.5)
    lane = lax.broadcasted_iota(jnp.int32, (1, LANE), 1)
    lo = lane < DA_HEAD
    q0 = jnp.where(lo, q, 0.0).astype(BF16)
    q1 = jnp.where(lo, 0.0, q).astype(BF16)
    nt = (((1,), (1,)), ((), ()))
    kb = kb_ref[...]
    ckb = ck_ref[...].astype(BF16)
    s0 = lax.dot_general(q0, kb, nt, preferred_element_type=F32)
    s1 = lax.dot_general(q1, kb, nt, preferred_element_type=F32)
    c0 = lax.dot_general(q0, ckb, nt, preferred_element_type=F32)
    c1 = lax.dot_general(q1, ckb, nt, preferred_element_type=F32)
    m0 = jnp.maximum(jnp.max(s0, axis=-1, keepdims=True), jnp.max(c0, axis=-1, keepdims=True))
    m1 = jnp.maximum(jnp.max(s1, axis=-1, keepdims=True), jnp.max(c1, axis=-1, keepdims=True))
    e0 = jnp.exp(s0 - m0)
    e1 = jnp.exp(s1 - m1)
    ec0 = jnp.exp(c0 - m0)
    ec1 = jnp.exp(c1 - m1)
    l0 = jnp.sum(e0, axis=-1, keepdims=True) + jnp.sum(ec0, axis=-1, keepdims=True)
    l1 = jnp.sum(e1, axis=-1, keepdims=True) + jnp.sum(ec1, axis=-1, keepdims=True)
    a0 = 1.0 / l0
    a1 = lam_ref[:, :1] / l1
    o = jnp.dot((e0 * a0 - e1 * a1).astype(BF16), vb_ref[...], preferred_element_type=F32)
    o = o + jnp.dot((ec0 * a0 - ec1 * a1).astype(BF16), cv_ref[...].astype(BF16), preferred_element_type=F32)
    y = o * lax.rsqrt(jnp.mean(o * o, axis=-1, keepdims=True) + EPS)
    o_ref[...] = (y * g_ref[...]).astype(o_ref.dtype)


def _attention_latent(p, lam_row, g_row, row0, B, T, tq, rope_tabs, ck, cv, layer):
    nq = T // tq
    rb0 = row0 // tq
    kb0 = row0 // T
    hq, hk, hv = C_Q // LANE, C_K // LANE, C_V // LANE
    past = ck.shape[2]
    cos, sin = rope_tabs
    in_specs = [pl.BlockSpec((tq, LANE), lambda b, h, i: (rb0 + b * nq + i, hq + h)),
                pl.BlockSpec((T, LANE), lambda b, h, i: (kb0 + b, hk + h)),
                pl.BlockSpec((T, LANE), lambda b, h, i: (kb0 + b, hv + h)),
                pl.BlockSpec((None, None, past, LANE), lambda b, h, i: (b, layer, 0, h)),
                pl.BlockSpec((None, None, past, LANE), lambda b, h, i: (b, layer, 0, h)),
                pl.BlockSpec((tq, LANE), lambda b, h, i: (i, 0)), pl.BlockSpec((tq, LANE), lambda b, h, i: (i, 0)),
                pl.BlockSpec((T, LANE), lambda b, h, i: (0, 0)), pl.BlockSpec((T, LANE), lambda b, h, i: (0, 0)),
                pl.BlockSpec((1, LANE), lambda b, h, i: (0, 0)), pl.BlockSpec((1, LANE), lambda b, h, i: (0, 0))]
    return pl.pallas_call(
        _attn_latent_kernel, grid=(B, DA_HEADS, nq), in_specs=in_specs,
        out_specs=pl.BlockSpec((tq, LANE), lambda b, h, i: (b * nq + i, h)),
        out_shape=jax.ShapeDtypeStruct((B * T, DA_HEADS * DA_VDIM), BF16),
        scratch_shapes=[pltpu.VMEM((T, LANE), BF16), pltpu.VMEM((T, LANE), BF16)],
        compiler_params=_cp(("parallel", "parallel", "arbitrary"), 56),
        name="diff_attention_latent",
    )(p, p, p, ck, cv, cos, sin, cos, sin, lam_row, g_row)


def _log_sigmoid(z):
    return jnp.minimum(z, 0.0) - jnp.log1p(jnp.exp(-jnp.abs(z)))


def _gla_direction(d, q_ref, k_ref, v_ref, ga_ref, wa_ref, ba_ref, tri_ref, o_ref, st_ref):
    C = GLA_CHUNK
    nsub = GLA_ROWS // C
    z = jnp.dot(ga_ref[...].astype(BF16), wa_ref[0, d], preferred_element_type=F32) + ba_ref[0, d]
    la = _log_sigmoid(z) / GLA_TAU
    hi = la.astype(BF16)
    r1 = la - hi.astype(F32)
    mid = r1.astype(BF16)
    low = (r1 - mid.astype(F32)).astype(BF16)
    tri = tri_ref[d]
    bsum = (jnp.dot(tri, hi, preferred_element_type=F32) + jnp.dot(tri, mid, preferred_element_type=F32)
            + jnp.dot(tri, low, preferred_element_type=F32))
    row = lax.broadcasted_iota(jnp.int32, (C, C), 0)
    col = lax.broadcasted_iota(jnp.int32, (C, C), 1)
    mask = (row >= col) if d == 0 else (row <= col)
    nt = (((1,), (1,)), ((), ()))
    tn = (((0,), (0,)), ((), ()))
    subs = range(nsub) if d == 0 else range(nsub - 1, -1, -1)
    for h in range(GLA_HEADS):
        hk = slice(h * GLA_DK, (h + 1) * GLA_DK)
        hv = slice(h * GLA_DV, (h + 1) * GLA_DV)
        st = st_ref[d, h]
        for s in subs:
            rs = slice(s * C, (s + 1) * C)
            end = s * C + C - 1 if d == 0 else s * C
            bb = bsum[rs, hk]
            bl = bsum[end:end + 1, hk]
            q = q_ref[rs, hk] * (GLA_DK ** -0.5)
            k = k_ref[rs, hk]
            v = v_ref[rs, hv].astype(BF16)
            qt = (q * jnp.exp(bb)).astype(BF16)
            kt = (k * jnp.exp(-bb)).astype(BF16)
            kend = (k * jnp.exp(bl - bb)).astype(BF16)
            att = jnp.where(mask, lax.dot_general(qt, kt, nt, preferred_element_type=F32), 0.0).astype(BF16)
            o = jnp.dot(att, v, preferred_element_type=F32)
            o = o + lax.dot_general(qt, st.astype(BF16), nt, preferred_element_type=F32)
            o_ref[rs, hv] = o
            st = st * jnp.exp(bl) + lax.dot_general(v, kend, tn, preferred_element_type=F32)
        st_ref[d, h] = st


def _gla_kernel(qf, kf, vf, gaf, qb, kb, vb, gab, wa_ref, ba_ref, tri_ref, s0_ref, of_ref, ob_ref, fin_ref, st_ref,
                *, seq_pos):
    first, last, stateless = seq_pos(pl.program_id(0))

    @pl.when(first & stateless)
    def _():
        st_ref[...] = jnp.zeros(st_ref.shape, F32)

    @pl.when(first & jnp.logical_not(stateless))
    def _():
        for d in range(2):
            for h in range(GLA_HEADS):
                st_ref[d, h] = s0_ref[d, h].T

    _gla_direction(0, qf, kf, vf, gaf, wa_ref, ba_ref, tri_ref, of_ref, st_ref)
    _gla_direction(1, qb, kb, vb, gab, wa_ref, ba_ref, tri_ref, ob_ref, st_ref)

    @pl.when(last & stateless)
    def _():
        for d in range(2):
            for h in range(GLA_HEADS):
                fin_ref[d, h] = st_ref[d, h].T


def _gla(p, wa, ba, tri, state, layer, seqs):
    N = p.shape[0]
    R = GLA_ROWS
    (n1, t1), (n2, t2) = seqs
    nb1, nb2 = t1 // R, t2 // R
    steps1 = n1 * nb1

    def locate(s):
        in1 = s < steps1
        s2 = s - steps1
        nb = jnp.where(in1, nb1, nb2)
        seq = jnp.where(in1, s // nb1, n1 + s2 // nb2)
        n = jnp.where(in1, s % nb1, s2 % nb2)
        base = jnp.where(in1, (s // nb1) * nb1, steps1 + (s2 // nb2) * nb2)
        return seq, n, nb, base

    def fwd_blk(s):
        _, n, _, base = locate(s)
        return base + n

    def bwd_blk(s):
        _, n, nb, base = locate(s)
        return base + nb - 1 - n

    def seq_pos(s):
        _, n, nb, _ = locate(s)
        return n == 0, n == nb - 1, s < steps1

    def specs(blk):
        return [pl.BlockSpec((R, GLA_KTOT), lambda s: (blk(s), C_GQ // GLA_KTOT)),
                pl.BlockSpec((R, GLA_KTOT), lambda s: (blk(s), C_GK // GLA_KTOT)),
                pl.BlockSpec((R, GLA_WIDTH), lambda s: (blk(s), C_GV // GLA_WIDTH)),
                pl.BlockSpec((R, LANE), lambda s: (blk(s), C_GA // LANE))]

    s0_spec = pl.BlockSpec((None, None, 2, GLA_HEADS, GLA_DK, GLA_DV),
                           lambda s: (jnp.maximum(locate(s)[0] - n1, 0), layer, 0, 0, 0, 0))
    fin_spec = pl.BlockSpec((None, 2, GLA_HEADS, GLA_DK, GLA_DV),
                            lambda s: (jnp.minimum(locate(s)[0], n1 - 1), 0, 0, 0, 0))
    return pl.pallas_call(
        functools.partial(_gla_kernel, seq_pos=seq_pos),
        grid=(N // R,),
        in_specs=specs(fwd_blk) + specs(bwd_blk) + [
            pl.BlockSpec((1, 2, LANE, GLA_KTOT), lambda s: (0, 0, 0, 0)),
            pl.BlockSpec((1, 2, 1, GLA_KTOT), lambda s: (0, 0, 0, 0)),
            pl.BlockSpec((2, R, R), lambda s: (0, 0, 0)),
            s0_spec],
        out_specs=[pl.BlockSpec((R, GLA_WIDTH), lambda s: (fwd_blk(s), 0)),
                   pl.BlockSpec((R, GLA_WIDTH), lambda s: (bwd_blk(s), 0)),
                   fin_spec],
        out_shape=[jax.ShapeDtypeStruct((N, GLA_WIDTH), F32), jax.ShapeDtypeStruct((N, GLA_WIDTH), F32),
                   jax.ShapeDtypeStruct((n1, 2, GLA_HEADS, GLA_DK, GLA_DV), F32)],
        scratch_shapes=[pltpu.VMEM((2, GLA_HEADS, GLA_DV, GLA_DK), F32)],
        compiler_params=_cp(("arbitrary",)),
        name="gla_bidir",
    )(p, p, p, p, p, p, p, p, wa, ba, tri, state)


def _gla_post_kernel(of_ref, ob_ref, r_ref, g_ref, o_ref):
    for h in range(GLA_HEADS):
        hv = slice(h * GLA_DV, (h + 1) * GLA_DV)
        o = of_ref[:, hv] + ob_ref[:, hv]
        y = o * lax.rsqrt(jnp.mean(o * o, axis=-1, keepdims=True) + EPS) * g_ref[...]
        r = r_ref[:, hv]
        o_ref[:, hv] = (y * (r * jax.nn.sigmoid(r))).astype(o_ref.dtype)


def _gla_post(of, ob, p, g, tm):
    N = of.shape[0]
    return pl.pallas_call(
        _gla_post_kernel, grid=(N // tm,),
        in_specs=[pl.BlockSpec((tm, GLA_WIDTH), lambda i: (i, 0)), pl.BlockSpec((tm, GLA_WIDTH), lambda i: (i, 0)),
                  pl.BlockSpec((tm, GLA_WIDTH), lambda i: (i, C_GR // GLA_WIDTH)),
                  pl.BlockSpec((1, GLA_DV), lambda i: (0, 0))],
        out_specs=pl.BlockSpec((tm, GLA_WIDTH), lambda i: (i, 0)),
        out_shape=jax.ShapeDtypeStruct((N, GLA_WIDTH), BF16),
        compiler_params=_cp(("parallel",)), name="gla_norm_gate",
    )(of, ob, p, g)


def _s5_matrices(a_re, a_im, log_dt, b_re, b_im, c_re, c_im):
    hp = lax.Precision.HIGHEST
    L, G, P, CH = S5_L, S5_GROUPS, S5_STATE, S5_CH
    taus = jnp.arange(L + 1, dtype=F32)
    ii = jnp.arange(L)
    kerns, powers, bbars, carry, decay = [], [], [], [], []
    c_re_t, c_im_t = c_re.transpose(2, 0, 1), c_im.transpose(2, 0, 1)
    for d in range(2):
        dt = jnp.exp(log_dt[d])[:, None]
        ld_re, ld_im = a_re[d] * dt, a_im[d] * dt
        mag = jnp.exp(ld_re)
        lb_re, lb_im = mag * jnp.cos(ld_im), mag * jnp.sin(ld_im)
        den = a_re[d] * a_re[d] + a_im[d] * a_im[d]
        q_re = ((lb_re - 1.0) * a_re[d] + lb_im * a_im[d]) / den
        q_im = (lb_im * a_re[d] - (lb_re - 1.0) * a_im[d]) / den
        bb_re = q_re[..., None] * b_re - q_im[..., None] * b_im
        bb_im = q_re[..., None] * b_im + q_im[..., None] * b_re
        pmag = jnp.exp(ld_re[None] * taus[:, None, None])
        pw_re = pmag * jnp.cos(ld_im[None] * taus[:, None, None])
        pw_im = pmag * jnp.sin(ld_im[None] * taus[:, None, None])
        cp_re = c_re[None] * pw_re[:L, :, None, :] - c_im[None] * pw_im[:L, :, None, :]
        cp_im = c_re[None] * pw_im[:L, :, None, :] + c_im[None] * pw_re[:L, :, None, :]
        kern = (jnp.einsum('tgop,gpc->tcgo', cp_re, bb_re, precision=hp)
                - jnp.einsum('tgop,gpc->tcgo', cp_im, bb_im, precision=hp)).reshape(L, CH, G * CH)
        kerns.append(kern)
        powers.append(jnp.stack([pw_re.reshape(L + 1, G * P), pw_im.reshape(L + 1, G * P)]))
        bbars.append(jnp.stack([bb_re.transpose(2, 0, 1).reshape(CH, G * P), bb_im.transpose(2, 0, 1).reshape(CH, G * P)]))
        e = (ii + 1) if d == 0 else (L - ii)
        pt_re, pt_im = pw_re[e].transpose(0, 2, 1)[..., None], pw_im[e].transpose(0, 2, 1)[..., None]
        cw_re = c_re_t[None] * pt_re - c_im_t[None] * pt_im
        cw_im = c_re_t[None] * pt_im + c_im_t[None] * pt_re
        carry += [cw_re.reshape(L, P, G * CH), (-cw_im).reshape(L, P, G * CH)]
        decay.append((pw_re[L], pw_im[L]))
    kk = jnp.stack([kerns[0].at[0].add(kerns[1][0]), kerns[1]]).astype(BF16)
    pw = jnp.stack(powers)
    bb = jnp.stack(bbars)
    cw = jnp.stack(carry).astype(BF16)
    dec = jnp.stack([jnp.stack([x[0], x[1]]) for x in decay])
    dec = dec.reshape(4, S5_SG, S5_SW // 4).transpose(1, 0, 2)
    return kk, pw, bb, cw, dec


def _s5_slab(u_ref, rt):
    return jnp.concatenate([u_ref[pl.ds(j, rt, stride=S5_L), :] for j in range(S5_L)], axis=1).astype(BF16)


def _s5_group_masks(width, per_group):
    grp = (lax.broadcasted_iota(jnp.int32, (1, width), 1) // per_group) % S5_GB
    return [grp == a for a in range(S5_GB)]


def _s5_state_kernel(u_ref, pw_ref, bb_ref, z_ref, w_ref):
    @pl.when(pl.program_id(1) == 0)
    def _():
        Q = S5_SW // 4
        masks = _s5_group_masks(Q, S5_STATE)
        for d in range(2):
            b_re, b_im = bb_ref[d, 0], bb_ref[d, 1]
            for j in range(S5_L):
                e = S5_L - 1 - j if d == 0 else j
                p_re, p_im = pw_ref[d, 0, e:e + 1, :], pw_ref[d, 1, e:e + 1, :]
                parts = ((p_re * b_re - p_im * b_im).astype(BF16), (p_re * b_im + p_im * b_re).astype(BF16))
                for ri in range(2):
                    q = 2 * d + ri
                    for a in range(S5_GB):
                        r0 = j * LANE + a * S5_CH
                        w_ref[r0:r0 + S5_CH, q * Q:(q + 1) * Q] = jnp.where(masks[a], parts[ri],
                                                                            jnp.zeros_like(parts[ri]))

    z_ref[...] = jnp.dot(_s5_slab(u_ref, z_ref.shape[0]), w_ref[...], preferred_element_type=F32)


def _s5_out_kernel(u_ref, p_ref, k_ref, c_ref, y_ref, wi_ref, wc_ref):
    @pl.when(pl.program_id(1) == 0)
    def _():
        Q = S5_SW // 4
        masks = _s5_group_masks(LANE, S5_CH)
        for i in range(S5_L):
            cols = slice(i * LANE, (i + 1) * LANE)
            for j in range(S5_L):
                blk = k_ref[0 if i >= j else 1, abs(i - j)]
                for a in range(S5_GB):
                    r0 = j * LANE + a * S5_CH
                    wi_ref[r0:r0 + S5_CH, cols] = jnp.where(masks[a], blk, jnp.zeros_like(blk))
            for q in range(4):
                blk = c_ref[q, i]
                for b in range(S5_GB):
                    r0 = q * Q + b * S5_STATE
                    wc_ref[r0:r0 + S5_STATE, cols] = jnp.where(masks[b], blk, jnp.zeros_like(blk))

    rt = p_ref.shape[0]
    y = (jnp.dot(_s5_slab(u_ref, rt), wi_ref[...], preferred_element_type=F32)
         + jnp.dot(p_ref[...].astype(BF16), wc_ref[...], preferred_element_type=F32))
    for i in range(S5_L):
        y_ref[pl.ds(i, rt, stride=S5_L), :] = y[:, i * LANE:(i + 1) * LANE]


def _s5_scan_kernel(z_ref, dec_ref, x0_ref, p_ref, fin_ref, *, segments):
    Q = S5_SW // 4
    dec = [(dec_ref[0, 2 * d:2 * d + 1, :], dec_ref[0, 2 * d + 1:2 * d + 2, :]) for d in range(2)]
    seq_base = 0
    for (row0, n_seq, n_chunk) in segments:
        def seq_loop(sq, carry, row0=row0, n_chunk=n_chunk, seq_base=seq_base):
            r0 = row0 + sq * n_chunk
            sid = seq_base + sq
            x_init = tuple((x0_ref[0, pl.ds(4 * sid + 2 * d, 1), :], x0_ref[0, pl.ds(4 * sid + 2 * d + 1, 1), :])
                           for d in range(2))

            def step(n, xs):
                out = []
                for d in range(2):
                    xr, xi = xs[d]
                    ar, ai = dec[d]
                    r = r0 + (n if d == 0 else n_chunk - 1 - n)
                    cr = slice(2 * d * Q, (2 * d + 1) * Q)
                    ci = slice((2 * d + 1) * Q, (2 * d + 2) * Q)
                    zr = z_ref[pl.ds(r, 1), cr]
                    zi = z_ref[pl.ds(r, 1), ci]
                    p_ref[pl.ds(r, 1), cr] = xr
                    p_ref[pl.ds(r, 1), ci] = xi
                    out.append((ar * xr - ai * xi + zr, ar * xi + ai * xr + zi))
                return tuple(out)

            xs = lax.fori_loop(0, n_chunk, step, x_init)
            for d in range(2):
                fin_ref[0, pl.ds(4 * sid + 2 * d, 1), :] = xs[d][0]
                fin_ref[0, pl.ds(4 * sid + 2 * d + 1, 1), :] = xs[d][1]
            return carry

        lax.fori_loop(0, n_seq, seq_loop, 0)
        seq_base += n_seq


def _s5(p, weights, x0, segments):
    kk, pw, bb, cw, dec = weights
    N = p.shape[0]
    NC = N // S5_L
    SG = S5_SG
    Q = S5_SW // 4
    rt = _pick(256, NC)
    nrow = x0.shape[1]
    cb = C_S5 // LANE
    u_spec = pl.BlockSpec((rt * S5_L, LANE), lambda s, r: (r, cb + s))
    slab_spec = pl.BlockSpec((rt, S5_SW), lambda s, r: (r, s))
    z = pl.pallas_call(
        _s5_state_kernel, grid=(SG, NC // rt),
        in_specs=[u_spec, pl.BlockSpec((2, 2, S5_L + 1, Q), lambda s, r: (0, 0, 0, s)),
                  pl.BlockSpec((2, 2, S5_CH, Q), lambda s, r: (0, 0, 0, s))], out_specs=slab_spec,
        out_shape=jax.ShapeDtypeStruct((NC, SG * S5_SW), F32),
        scratch_shapes=[pltpu.VMEM((S5_W, S5_SW), BF16)],
        compiler_params=_cp(("parallel", "arbitrary")), name="s5_chunk_state",
    )(p, pw, bb)
    state_spec = pl.BlockSpec((1, nrow, Q), lambda s: (s, 0, 0))
    carry_in, fin = pl.pallas_call(
        functools.partial(_s5_scan_kernel, segments=segments), grid=(SG,),
        in_specs=[pl.BlockSpec((NC, S5_SW), lambda s: (0, s)), pl.BlockSpec((1, 4, Q), lambda s: (s, 0, 0)),
                  state_spec],
        out_specs=[pl.BlockSpec((NC, S5_SW), lambda s: (0, s)), state_spec],
        out_shape=[jax.ShapeDtypeStruct((NC, SG * S5_SW), F32), jax.ShapeDtypeStruct((SG, nrow, Q), F32)],
        compiler_params=_cp(("parallel",)), name="s5_chunk_scan",
    )(z, dec, x0)
    y = pl.pallas_call(
        _s5_out_kernel, grid=(SG, NC // rt),
        in_specs=[u_spec, slab_spec, pl.BlockSpec((2, S5_L, S5_CH, LANE), lambda s, r: (0, 0, 0, s)),
                  pl.BlockSpec((4, S5_L, S5_STATE, LANE), lambda s, r: (0, 0, 0, s))],
        out_specs=pl.BlockSpec((rt * S5_L, LANE), lambda s, r: (r, s)),
        out_shape=jax.ShapeDtypeStruct((N, S5_WIDTH), F32),
        scratch_shapes=[pltpu.VMEM((S5_W, S5_W), BF16), pltpu.VMEM((S5_SW, S5_W), BF16)],
        compiler_params=_cp(("parallel", "arbitrary"), 56), name="s5_chunk_out",
    )(p, carry_in, kk, cw)
    return y, fin


def _s5_post_kernel(y_ref, u_ref, d_ref, w_ref, b_ref, o_ref):
    yg = jax.nn.gelu(y_ref[...] + d_ref[...] * u_ref[...])
    z = jnp.dot(yg.astype(BF16), w_ref[...], preferred_element_type=F32) + b_ref[...]
    o_ref[...] = (yg * jax.nn.sigmoid(z)).astype(o_ref.dtype)


def _s5_post(y, p, d, w, b, tm):
    N = y.shape[0]
    W = S5_WIDTH
    return pl.pallas_call(
        _s5_post_kernel, grid=(N // tm,),
        in_specs=[pl.BlockSpec((tm, W), lambda i: (i, 0)), pl.BlockSpec((tm, W), lambda i: (i, C_S5 // W)),
                  pl.BlockSpec((1, W), lambda i: (0, 0)), pl.BlockSpec((W, W), lambda i: (0, 0)),
                  pl.BlockSpec((1, W), lambda i: (0, 0))],
        out_specs=pl.BlockSpec((tm, W), lambda i: (i, 0)),
        out_shape=jax.ShapeDtypeStruct((N, W), BF16),
        compiler_params=_cp(("parallel",)), name="s5_glu",
    )(y, p, d, w, b)


def _merge_kernel(h_ref, a_ref, b_ref, c_ref, wg0, wg1, wg2, bg0, bg1, bg2, wb_ref, o_ref):
    h = h_ref[...]
    acc = None
    for i, (br, wg, bg) in enumerate(((a_ref, wg0, bg0), (b_ref, wg1, bg1), (c_ref, wg2, bg2))):
        gate = jax.nn.sigmoid(jnp.dot(h, wg[...], preferred_element_type=F32) + bg[...])
        t = gate * jnp.dot(br[...], wb_ref[i], preferred_element_type=F32)
        acc = t if acc is None else acc + t
    o_ref[...] = acc.astype(o_ref.dtype)


def _merge(h, o_da, o_gla, o_s5, w_gate, b_gate, w_branch, tm, tn):
    N, D = h.shape
    W = o_da.shape[1]
    nt = D // tn
    act = pl.BlockSpec((tm, W), lambda j, i: (i, 0))
    in_specs = [pl.BlockSpec((tm, D), lambda j, i: (i, 0)), act, act, act]
    in_specs += [pl.BlockSpec((D, tn), lambda j, i, br=br: (0, br * nt + j)) for br in range(N_BRANCH)]
    in_specs += [pl.BlockSpec((1, tn), lambda j, i, br=br: (0, br * nt + j)) for br in range(N_BRANCH)]
    in_specs += [pl.BlockSpec((N_BRANCH, W, tn), lambda j, i: (0, 0, j))]
    return pl.pallas_call(
        _merge_kernel, grid=(nt, N // tm), in_specs=in_specs,
        out_specs=pl.BlockSpec((tm, tn), lambda j, i: (i, j)),
        out_shape=jax.ShapeDtypeStruct((N, D), BF16),
        compiler_params=_cp(("parallel", "parallel"), 56), name="gated_merge",
    )(h, o_da, o_gla, o_s5, w_gate, w_gate, w_gate, b_gate, b_gate, b_gate, w_branch)


def _wout_kernel(m_ref, w_ref, x_ref, g_ref, o_ref):
    o_ref[...] = x_ref[...] + g_ref[0] * jnp.dot(m_ref[...], w_ref[...], preferred_element_type=F32)


def _wout(merged, w, x, gate, midx, tm, tn):
    N, D = x.shape
    return pl.pallas_call(
        _wout_kernel, grid=(D // tn, N // tm),
        in_specs=[pl.BlockSpec((tm, D), lambda j, i: (i, 0)), pl.BlockSpec((D, tn), lambda j, i: (0, j)),
                  pl.BlockSpec((tm, tn), lambda j, i: (i, j)), pl.BlockSpec((1, 1, tn), lambda j, i: (midx(i), 0, j))],
        out_specs=pl.BlockSpec((tm, tn), lambda j, i: (i, j)),
        out_shape=jax.ShapeDtypeStruct((N, D), F32),
        compiler_params=_cp(("parallel", "parallel")), name="out_proj_residual",
    )(merged, w, x, gate)


def _route_kernel(lg_ref, bias_ref, asg_ref, tile_ref, rank_ref, *, tl):
    E, n_tok = lg_ref.shape
    EPG = EXPERTS_PER_GROUP
    ninf = -jnp.inf
    eio = lax.broadcasted_iota(jnp.int32, (E, tl), 0)
    upper = jnp.where(lax.broadcasted_iota(jnp.int32, (tl, tl), 0) < lax.broadcasted_iota(jnp.int32, (tl, tl), 1),
                      1.0, 0.0).astype(BF16)
    ecol = lax.broadcasted_iota(jnp.int32, (E, 1), 0)

    def select(j, carry):
        ls = pl.ds(pl.multiple_of(j * tl, tl), tl)
        s = jax.nn.sigmoid(lg_ref[:, ls])
        sel = s + bias_ref[...]
        rows = [sel[e:e + 1, :] for e in range(E)]
        srow = [s[e:e + 1, :] for e in range(E)]
        gi = jnp.zeros((1, tl), jnp.int32)
        best = None
        for g in range(N_EXPERT_GROUPS):
            a, b, c, d = rows[g * EPG:(g + 1) * EPG]
            hi1, lo1, hi2, lo2 = jnp.maximum(a, b), jnp.minimum(a, b), jnp.maximum(c, d), jnp.minimum(c, d)
            gs = jnp.maximum(hi1, hi2) + jnp.maximum(jnp.minimum(hi1, hi2), jnp.maximum(lo1, lo2))
            if best is None:
                best = gs
            else:
                upd = gs > best
                gi = jnp.where(upd, g, gi)
                best = jnp.where(upd, gs, best)

        def pick(rs, i):
            out = rs[(N_EXPERT_GROUPS - 1) * EPG + i]
            for g in range(N_EXPERT_GROUPS - 2, -1, -1):
                out = jnp.where(gi == g, rs[g * EPG + i], out)
            return out

        v = [pick(rows, i) for i in range(EPG)]
        sv = [pick(srow, i) for i in range(EPG)]
        i1, b1, s1 = jnp.zeros((1, tl), jnp.int32), v[0], sv[0]
        for i in range(1, EPG):
            upd = v[i] > b1
            i1, b1, s1 = jnp.where(upd, i, i1), jnp.where(upd, v[i], b1), jnp.where(upd, sv[i], s1)
        i2, b2, s2 = jnp.zeros((1, tl), jnp.int32), jnp.full((1, tl), ninf, F32), sv[0]
        for i in range(EPG):
            cand = jnp.where(i1 == i, ninf, v[i])
            upd = cand > b2
            i2, b2, s2 = jnp.where(upd, i, i2), jnp.where(upd, cand, b2), jnp.where(upd, sv[i], s2)
        e0 = gi * EPG + i1
        e1 = gi * EPG + i2
        den = s1 + s2
        asg_ref[2:3, ls] = s1 / den
        asg_ref[3:4, ls] = s2 / den
        asg_ref[4:5, ls] = e0.astype(F32)
        asg_ref[5:6, ls] = e1.astype(F32)
        m = jnp.where((eio == e0) | (eio == e1), 1.0, 0.0)
        rank_ref[:, ls] = jnp.dot(m.astype(BF16), upper, preferred_element_type=F32) + carry
        return carry + jnp.sum(m, axis=1, keepdims=True)

    counts = lax.fori_loop(0, n_tok // tl, select, jnp.zeros((E, 1), F32))
    ptiles = jnp.floor((counts + (MOE_TM - 1)) * (1.0 / MOE_TM))
    run = jnp.zeros((1, 1), F32)
    pstart_t = jnp.zeros((E, 1), F32)
    for e in range(E):
        pstart_t = jnp.where(ecol == e, run, pstart_t)
        run = run + ptiles[e:e + 1, :]
    pstart = pstart_t * MOE_TM

    def place(j, _):
        ls = pl.ds(pl.multiple_of(j * tl, tl), tl)
        dest = rank_ref[:, ls] + pstart
        e0 = asg_ref[4:5, ls].astype(jnp.int32)
        e1 = asg_ref[5:6, ls].astype(jnp.int32)
        asg_ref[0:1, ls] = jnp.sum(jnp.where(eio == e0, dest, 0.0), axis=0, keepdims=True)
        asg_ref[1:2, ls] = jnp.sum(jnp.where(eio == e1, dest, 0.0), axis=0, keepdims=True)
        asg_ref[6:8, ls] = jnp.zeros((2, tl), F32)
        return 0

    lax.fori_loop(0, n_tok // tl, place, 0)

    ntp = tile_ref.shape[1]
    pend_t = pstart_t + ptiles
    ti = lax.broadcasted_iota(jnp.int32, (1, ntp), 1).astype(F32)
    te = jnp.sum(jnp.where(pend_t <= ti, 1.0, 0.0), axis=0, keepdims=True)
    used = ti < run
    last_e = jnp.max(jnp.where(counts > 0.0, ecol.astype(F32), 0.0), axis=0, keepdims=True)
    te = jnp.where(used, jnp.minimum(te, E - 1.0), last_e)
    tile_ref[0:1, :] = te.astype(jnp.int32)
    tile_ref[1:2, :] = jnp.where(used, 1, 0).astype(jnp.int32)
    tile_ref[2:8, :] = jnp.zeros((6, ntp), jnp.int32)


def _route(logits_t, router_bias, ntile):
    E, N = logits_t.shape
    assert MOE_TM & (MOE_TM - 1) == 0
    ntp = -(-ntile // LANE) * LANE
    full = lambda shape: pl.BlockSpec(shape, lambda i: (0, 0))
    return pl.pallas_call(
        functools.partial(_route_kernel, tl=_pick(512, N)), grid=(1,),
        in_specs=[full((E, N)), full((E, 1))],
        out_specs=[full((8, N)), full((8, ntp))],
        out_shape=[jax.ShapeDtypeStruct((8, N), F32), jax.ShapeDtypeStruct((8, ntp), jnp.int32)],
        scratch_shapes=[pltpu.VMEM((E, N), F32)],
        compiler_params=_cp(("arbitrary",)), name="moe_route",
    )(logits_t, router_bias.astype(F32).reshape(E, 1))


def _moe_kernel(te_ref, tv_ref, x_ref, w1_ref, w3_ref, w2_ref, o_ref):
    valid = tv_ref[pl.program_id(0)] != 0

    @pl.when(valid)
    def _():
        x = x_ref[...]
        F = w1_ref.shape[2]
        acc = None
        for f0 in range(0, F, MOE_FC):
            fs = slice(f0, min(f0 + MOE_FC, F))
            a = jnp.dot(x, w1_ref[0, :, fs], preferred_element_type=F32)
            b = jnp.dot(x, w3_ref[0, :, fs], preferred_element_type=F32)
            hid = ((a * jax.nn.sigmoid(a)) * b).astype(BF16)
            t = jnp.dot(hid, w2_ref[0, fs, :], preferred_element_type=F32)
            acc = t if acc is None else acc + t
        o_ref[...] = acc.astype(o_ref.dtype)

    @pl.when(jnp.logical_not(valid))
    def _():
        o_ref[...] = jnp.zeros(o_ref.shape, o_ref.dtype)


def _moe(xin, w1, w3, w2, layer, tile_e, tile_valid):
    cap, D = xin.shape
    F = w1.shape[3]
    TM = MOE_TM
    grid_spec = pltpu.PrefetchScalarGridSpec(
        num_scalar_prefetch=2, grid=(cap // TM,),
        in_specs=[pl.BlockSpec((TM, D), lambda i, te, tv: (i, 0)),
                  pl.BlockSpec((None, 1, D, F), lambda i, te, tv: (layer, te[i], 0, 0)),
                  pl.BlockSpec((None, 1, D, F), lambda i, te, tv: (layer, te[i], 0, 0)),
                  pl.BlockSpec((None, 1, F, D), lambda i, te, tv: (layer, te[i], 0, 0))],
        out_specs=pl.BlockSpec((TM, D), lambda i, te, tv: (i, 0)))
    return pl.pallas_call(
        _moe_kernel, grid_spec=grid_spec,
        out_shape=jax.ShapeDtypeStruct((cap, D), BF16),
        compiler_params=_cp(("arbitrary",), 58), name="moe_experts",
    )(tile_e, tile_valid, xin, w1, w3, w2)


def _combine_kernel(x_ref, y0_ref, y1_ref, w0_ref, w1_ref, g_ref, o_ref):
    o_ref[...] = x_ref[...] + g_ref[0] * (y0_ref[...] * w0_ref[...] + y1_ref[...] * w1_ref[...])


def _combine(x, y0, y1, w0, w1, gate, midx, tm):
    N, D = x.shape
    blk = pl.BlockSpec((tm, D), lambda i: (i, 0))
    col = pl.BlockSpec((tm, 1), lambda i: (i, 0))
    return pl.pallas_call(
        _combine_kernel, grid=(N // tm,),
        in_specs=[blk, blk, blk, col, col, pl.BlockSpec((1, 1, D), lambda i: (midx(i), 0, 0))],
        out_specs=blk, out_shape=jax.ShapeDtypeStruct((N, D), F32),
        compiler_params=_cp(("parallel",)), name="moe_residual",
    )(x, y0, y1, w0, w1, gate)


def _rope_tables(n_tokens):
    rows = n_tokens // GRID_W
    row = jnp.repeat(jnp.arange(rows, dtype=F32), GRID_W)
    col = jnp.tile(jnp.arange(GRID_W, dtype=F32), rows)
    inv = ROPE_BASE ** (-jnp.arange(0, AX_DIM, 2, dtype=F32) / AX_DIM)
    ar = row[:, None] * inv[None, :]
    ac = col[:, None] * inv[None, :]
    ang = jnp.tile(jnp.concatenate([ar, ar, ac, ac], axis=-1), (1, LANE // DA_HEAD))
    first = (jnp.arange(LANE) % (AX_DIM)) < (AX_DIM // 2)
    return jnp.cos(ang), jnp.where(first[None, :], -jnp.sin(ang), jnp.sin(ang))


def kernel(x_prompt, x_sample, cache_k, cache_v, state_gla, state_s5_re, state_s5_im, c, c_ctx, w_mod, b_mod, g_norm, w_in, da_lambda_q, da_lambda_k, da_subln_g, gla_w_alpha, gla_b_alpha, gla_norm_g, s5_a_re, s5_a_im, s5_log_dt, s5_b_re, s5_b_im, s5_c_re, s5_c_im, s5_d, s5_w_glu, s5_b_glu, w_branch, w_gate, b_gate, w_out, router_w, router_bias, moe_w1, moe_w3, moe_w2, g_final):
    B, T, D = x_prompt.shape
    DB, DT, _ = x_sample.shape
    L = w_in.shape[0]
    PAST = cache_k.shape[2]
    NP, NS = B * T, DB * DT
    N = NP + NS
    nseq = B + DB
    assert D == D_MODEL and 1 + DB <= 8 and NP % DT == 0 and T % GLA_ROWS == 0 and DT % GLA_ROWS == 0
    tm = _pick(512, NP, DT)
    midx = functools.partial(_mod_index, tm=tm, n_ctx_tok=NP, dec_seq=DT)

    x = jnp.concatenate([x_prompt.reshape(NP, D), x_sample.reshape(NS, D)], axis=0)
    cv = jnp.zeros((8, D), F32).at[0].set(c_ctx).at[1:1 + DB].set(c)
    mods = _mod(cv, w_mod, b_mod).reshape(L, 8, N_MOD, 1, D)
    zero_mod = jnp.zeros((8, 1, D), F32)

    rope_tabs = _rope_tables(DT)
    ck = cache_k.reshape(DB, L, PAST, DA_HEADS * 2 * DA_HEAD)
    cvv = cache_v.reshape(DB, L, PAST, DA_HEADS * DA_VDIM)
    r = jnp.arange(GLA_ROWS)
    same = (r[:, None] // GLA_CHUNK) == (r[None, :] // GLA_CHUNK)
    tri = jnp.stack([same & (r[None, :] <= r[:, None]), same & (r[None, :] >= r[:, None])]).astype(BF16)
    rw = router_w.astype(F32).T
    rw_hi = rw.astype(BF16)
    rw_lo = (rw - rw_hi.astype(F32)).astype(BF16)
    ntile = N * TOP_K // MOE_TM + N_EXPERTS
    tok2 = jnp.tile(jnp.arange(N, dtype=jnp.int32), TOP_K)
    moe_w1_bf, moe_w3_bf, moe_w2_bf = _cast_bf16(moe_w1), _cast_bf16(moe_w3), _cast_bf16(moe_w2)
    tq_ctx = _pick(256, T)
    tq_lat = _pick(256, DT)
    s5_segments = ((0, B, T // S5_L), (NP // S5_L, DB, DT // S5_L))

    g_list, re_list, im_list = [], [], []
    kbuf = jnp.zeros((B, L, T, DA_HEADS * DA_VDIM), F32)
    vbuf = jnp.zeros((B, L, T, DA_HEADS * DA_VDIM), F32)
    for l in range(L):
        m = mods[l]
        shift1, scale1, gate1, shift2, scale2, gate2 = (m[:, i] for i in range(N_MOD))
        wi = w_in[l]
        w_in_p = jnp.concatenate([wi[:, :C_S5], wi[:, C_S5 + GLA_RANK:], wi[:, C_S5:C_S5 + GLA_RANK],
                                  jnp.zeros((D, P_COLS - C_GA - GLA_RANK), F32)], axis=1).astype(BF16)
        h = _norm(x, g_norm[l, 0][None], scale1, shift1, midx, tm, BF16)
        p = _matmul(h, w_in_p, tm, 1536, F32, "in_proj")

        lam_init = 0.8 - 0.6 * math.exp(-0.3 * l)
        lq, lk = da_lambda_q[l], da_lambda_k[l]
        lam = jnp.exp(jnp.sum(lq[0] * lk[0])) - jnp.exp(jnp.sum(lq[1] * lk[1])) + lam_init
        lam_row = jnp.full((1, LANE), lam, F32)
        g_row = (da_subln_g[l] * (1.0 - lam_init))[None]
        o_da = jnp.concatenate([
            _attention(p, lam_row, g_row, 0, B, T, tq_ctx, DA_HEADS, _pick(ATT_KB, T)),
            _attention_latent(p, lam_row, g_row, NP, DB, DT, tq_lat, rope_tabs, ck, cvv, l)], axis=0)

        wa = jnp.zeros((1, 2, LANE, GLA_KTOT), F32).at[0, :, :GLA_RANK].set(gla_w_alpha[l]).astype(BF16)
        ba = gla_b_alpha[l].reshape(1, 2, 1, GLA_KTOT)
        o_f, o_b, gla_fin = _gla(p, wa, ba, tri, state_gla, l, ((B, T), (DB, DT)))
        o_gla = _gla_post(o_f, o_b, p, gla_norm_g[l][None], tm)

        mats = _s5_matrices(s5_a_re[l], s5_a_im[l], s5_log_dt[l], s5_b_re[l], s5_b_im[l], s5_c_re[l], s5_c_im[l])
        x0 = jnp.stack([state_s5_re[:, l], state_s5_im[:, l]], axis=2)
        x0 = jnp.concatenate([jnp.zeros((B,) + x0.shape[1:], F32), x0], axis=0)
        x0 = x0.reshape(4 * nseq, S5_SG, S5_SW // 4).transpose(1, 0, 2)
        y, s5_fin = _s5(p, mats, x0, s5_segments)
        o_s5 = _s5_post(y, p, s5_d[l][None], s5_w_glu[l].astype(BF16), s5_b_glu[l][None], tm)
        s5_fin = s5_fin.transpose(1, 0, 2).reshape(nseq, 2, 2, S5_GROUPS, S5_STATE)[:B]

        merged = _merge(h, o_da, o_gla, o_s5, w_gate[l].astype(BF16), b_gate[l][None], w_branch[l].astype(BF16),
                        tm, 512)
        x = _wout(merged, w_out[l].astype(BF16), x, gate1, midx, tm, 1024)

        h2, logits_t = _norm(x, g_norm[l, 1][None], scale2, shift2, midx, tm, BF16, router=(rw_hi, rw_lo))
        asg, tiles = _route(logits_t, router_bias, ntile)
        slot = asg[0:2].astype(jnp.int32)
        buf_tok = jnp.zeros((ntile * MOE_TM,), jnp.int32).at[slot.reshape(-1)].set(tok2, unique_indices=True)
        xin = jnp.take(h2, buf_tok, axis=0, mode="clip")
        ye = _moe(xin, moe_w1_bf, moe_w3_bf, moe_w2_bf, l, tiles[0, :ntile], tiles[1, :ntile])
        x = _combine(x, jnp.take(ye, slot[0], axis=0, mode="clip"), jnp.take(ye, slot[1], axis=0, mode="clip"),
                     asg[2].reshape(N, 1),
                     asg[3].reshape(N, 1), gate2, midx, tm)

        kbuf, vbuf = _kv_store(p, kbuf, vbuf, l, B, T)
        g_list.append(gla_fin[:B])
        re_list.append(s5_fin[:, :, 0])
        im_list.append(s5_fin[:, :, 1])

    y_ctx = _norm(x, g_final[None], zero_mod, zero_mod, midx, tm, F32, rows=(0, NP))
    y_lat = _norm(x, g_final[None], zero_mod, zero_mod, midx, tm, F32, rows=(NP, NS))
    return (y_ctx.reshape(B, T, D), y_lat.reshape(DB, DT, D),
            kbuf.reshape(B, L, T, DA_HEADS, 2 * DA_HEAD), vbuf.reshape(B, L, T, DA_HEADS, DA_VDIM),
            jnp.stack(g_list, axis=1),
            jnp.stack(re_list, axis=1), jnp.stack(im_list, axis=1))
```

```python
import functools
import math

import jax
import jax.numpy as jnp
from jax import lax
from jax.experimental import pallas as pl
from jax.experimental.pallas import tpu as pltpu

F32 = jnp.float32
BF16 = jnp.bfloat16

D_MODEL = 2048
GRID_W = 64
EPS = 1e-6
N_MOD = 6
N_BRANCH = 3
DA_HEADS = 8
DA_HEAD = 64
DA_VDIM = 128
ROPE_BASE = 10000.0
AX_DIM = DA_HEAD // 2
GLA_HEADS = 4
GLA_DK = 128
GLA_DV = 256
GLA_KTOT = GLA_HEADS * GLA_DK
GLA_WIDTH = GLA_HEADS * GLA_DV
GLA_RANK = 16
GLA_TAU = 16.0
GLA_CHUNK = 64
S5_WIDTH = 1024
S5_CH = 16
S5_GROUPS = S5_WIDTH // S5_CH
S5_STATE = 64
N_EXPERTS = 16
N_EXPERT_GROUPS = 4
EXPERTS_PER_GROUP = N_EXPERTS // N_EXPERT_GROUPS
TOPK_GROUP = 1
TOP_K = 2
D_FF_EXPERT = 1408

LANE = 128
C_Q, C_K, C_V, C_GQ, C_GK, C_GV, C_GR, C_S5, C_GA = 0, 1024, 2048, 3072, 3584, 4096, 5120, 6144, 7168
P_COLS = 7680
GLA_ROWS = 256
S5_L = 16
S5_SG = S5_WIDTH // LANE
S5_GB = S5_GROUPS // S5_SG
S5_W = S5_L * LANE
S5_SW = 4 * S5_GB * S5_STATE
MOE_TM = 512
MOE_FC = 512
ATT_KB = 512
VMEM_MB = 1024 * 1024


def _cp(sem, vmem_mb=48):
    return pltpu.CompilerParams(dimension_semantics=sem, vmem_limit_bytes=vmem_mb * VMEM_MB)


def _pick(pref, *ns):
    t = pref
    while any(n % t for n in ns):
        t //= 2
    return t


def _mod_index(i, tm, n_ctx_tok, dec_seq):
    t0 = i * tm
    return jnp.where(t0 < n_ctx_tok, 0, 1 + (t0 - n_ctx_tok) // dec_seq)


def _mod_kernel(c_ref, w_ref, b_ref, o_ref):
    c = c_ref[...]
    s = (c * jax.nn.sigmoid(c)).astype(BF16)
    o_ref[0] = jnp.dot(s, w_ref[0].astype(BF16), preferred_element_type=F32) + b_ref[0]


def _mod(cv, w_mod, b_mod):
    L, D, M = w_mod.shape
    tn = 1024
    return pl.pallas_call(
        _mod_kernel,
        grid=(L, M // tn),
        in_specs=[pl.BlockSpec((8, D), lambda l, j: (0, 0)),
                  pl.BlockSpec((1, D, tn), lambda l, j: (l, 0, j)),
                  pl.BlockSpec((1, 1, tn), lambda l, j: (l, 0, j))],
        out_specs=pl.BlockSpec((1, 8, tn), lambda l, j: (l, 0, j)),
        out_shape=jax.ShapeDtypeStruct((L, 8, M), F32),
        compiler_params=_cp(("parallel", "parallel")),
        name="mod_vectors",
    )(cv, w_mod, b_mod.reshape(L, 1, M))


def _cast_kernel(x_ref, o_ref):
    o_ref[...] = x_ref[...].astype(o_ref.dtype)


def _cast_bf16(w):
    lead, (R, C) = w.shape[:-2], w.shape[-2:]
    n = math.prod(lead)
    out = pl.pallas_call(
        _cast_kernel, grid=(n,),
        in_specs=[pl.BlockSpec((1, R, C), lambda i: (i, 0, 0))], out_specs=pl.BlockSpec((1, R, C), lambda i: (i, 0, 0)),
        out_shape=jax.ShapeDtypeStruct((n, R, C), BF16),
        compiler_params=_cp(("parallel",)), name="cast_bf16",
    )(w.reshape(n, R, C))
    return out.reshape(lead + (R, C))


def _norm_core(x_ref, g_ref, sc_ref, sh_ref):
    x = x_ref[...]
    y = x * lax.rsqrt(jnp.mean(x * x, axis=-1, keepdims=True) + EPS)
    return (y * g_ref[...]) * (1.0 + sc_ref[0]) + sh_ref[0]


def _norm_kernel(x_ref, g_ref, sc_ref, sh_ref, o_ref):
    o_ref[...] = _norm_core(x_ref, g_ref, sc_ref, sh_ref).astype(o_ref.dtype)


def _norm_router_kernel(x_ref, g_ref, sc_ref, sh_ref, rh_ref, rl_ref, o_ref, lg_ref):
    h = _norm_core(x_ref, g_ref, sc_ref, sh_ref)
    hh = h.astype(BF16)
    hl = (h - hh.astype(F32)).astype(BF16)
    o_ref[...] = hh
    nt = (((1,), (1,)), ((), ()))
    lg_ref[...] = (lax.dot_general(rh_ref[...], hh, nt, preferred_element_type=F32)
                   + lax.dot_general(rh_ref[...], hl, nt, preferred_element_type=F32)
                   + lax.dot_general(rl_ref[...], hh, nt, preferred_element_type=F32))


def _norm(x, g, sc, sh, midx, tm, out_dtype, router=None, rows=None):
    N, D = x.shape
    r0, n = (0, N) if rows is None else rows
    t0 = r0 // tm
    in_specs = [pl.BlockSpec((tm, D), lambda i: (t0 + i, 0)),
                pl.BlockSpec((1, D), lambda i: (0, 0)),
                pl.BlockSpec((1, 1, D), lambda i: (midx(t0 + i), 0, 0)),
                pl.BlockSpec((1, 1, D), lambda i: (midx(t0 + i), 0, 0))]
    if router is None:
        return pl.pallas_call(
            _norm_kernel, grid=(n // tm,), in_specs=in_specs,
            out_specs=pl.BlockSpec((tm, D), lambda i: (i, 0)),
            out_shape=jax.ShapeDtypeStruct((n, D), out_dtype),
            compiler_params=_cp(("parallel",)), name="norm_mod",
        )(x, g, sc, sh)
    assert rows is None
    rh, rl = router
    E = rh.shape[0]
    in_specs += [pl.BlockSpec((E, D), lambda i: (0, 0)), pl.BlockSpec((E, D), lambda i: (0, 0))]
    return pl.pallas_call(
        _norm_router_kernel, grid=(N // tm,), in_specs=in_specs,
        out_specs=[pl.BlockSpec((tm, D), lambda i: (i, 0)), pl.BlockSpec((E, tm), lambda i: (0, i))],
        out_shape=[jax.ShapeDtypeStruct((N, D), BF16), jax.ShapeDtypeStruct((E, N), F32)],
        compiler_params=_cp(("parallel",)), name="norm_mod_router",
    )(x, g, sc, sh, rh, rl)


def _mm_kernel(a_ref, w_ref, o_ref):
    o_ref[...] = jnp.dot(a_ref[...], w_ref[...], preferred_element_type=F32).astype(o_ref.dtype)


def _matmul(a, w, tm, tn, out_dtype, name):
    M, K = a.shape
    Nc = w.shape[1]
    return pl.pallas_call(
        _mm_kernel, grid=(Nc // tn, M // tm),
        in_specs=[pl.BlockSpec((tm, K), lambda j, i: (i, 0)), pl.BlockSpec((K, tn), lambda j, i: (0, j))],
        out_specs=pl.BlockSpec((tm, tn), lambda j, i: (i, j)),
        out_shape=jax.ShapeDtypeStruct((M, Nc), out_dtype),
        compiler_params=_cp(("parallel", "parallel")), name=name,
    )(a, w)


def _kv_store_kernel(k_ref, v_ref, *refs):
    ko_ref, vo_ref = refs[-2:]
    ko_ref[...] = k_ref[...]
    vo_ref[...] = v_ref[...]


def _kv_store(p, kbuf, vbuf, layer, B, T):
    W = DA_HEADS * DA_VDIM
    in_specs = [pl.BlockSpec((T, W), lambda b: (b, C_K // W)), pl.BlockSpec((T, W), lambda b: (b, C_V // W)),
                pl.BlockSpec(memory_space=pl.ANY), pl.BlockSpec(memory_space=pl.ANY)]
    out_spec = pl.BlockSpec((None, None, T, W), lambda b: (b, layer, 0, 0))
    shape = jax.ShapeDtypeStruct(kbuf.shape, F32)
    return pl.pallas_call(
        _kv_store_kernel, grid=(B,), in_specs=in_specs, out_specs=[out_spec, out_spec], out_shape=[shape, shape],
        input_output_aliases={2: 0, 3: 1}, compiler_params=_cp(("parallel",)), name="kv_store",
    )(p, p, kbuf, vbuf)


def _rope(x, cos, sin_signed):
    lane = lax.broadcasted_iota(jnp.int32, (1, LANE), 1)
    first = (lane % (2 * (AX_DIM // 2))) < (AX_DIM // 2)
    xr = jnp.where(first, pltpu.roll(x, LANE - AX_DIM // 2, 1), pltpu.roll(x, AX_DIM // 2, 1))
    return x * cos + xr * sin_signed


def _attn_kernel(*refs, rope, ctx, hs, kb):
    refs = list(refs)
    q_ref, k_ref, v_ref = refs[:3]
    refs = refs[3:]
    if ctx:
        ck_ref, cv_ref = refs[:2]
        refs = refs[2:]
    if rope:
        cq_ref, sq_ref, ckk_ref, skk_ref = refs[:4]
        refs = refs[4:]
    lam_ref, g_ref, o_ref, kb_ref, vb_ref = refs[:5]
    if ctx:
        ckb_ref, cvb_ref = refs[5:]

    hsl = [slice(h * LANE, (h + 1) * LANE) for h in range(hs)]

    @pl.when(pl.program_id(2) == 0)
    def _():
        for h in range(hs):
            k = k_ref[:, hsl[h]]
            if rope:
                k = _rope(k, ckk_ref[...], skk_ref[...])
            kb_ref[:, hsl[h]] = k.astype(BF16)
        vb_ref[...] = v_ref[...].astype(BF16)
        if ctx:
            ckb_ref[...] = ck_ref[...].astype(BF16)
            cvb_ref[...] = cv_ref[...].astype(BF16)

    tq = q_ref.shape[0]
    n_new = kb_ref.shape[0]
    lane = lax.broadcasted_iota(jnp.int32, (1, LANE), 1)
    lo = lane < DA_HEAD
    nt = (((1,), (1,)), ((), ()))
    qqs = []
    for h in range(hs):
        q = q_ref[:, hsl[h]]
        if rope:
            q = _rope(q, cq_ref[...], sq_ref[...])
        q = q * (DA_HEAD ** -0.5 * math.log2(math.e))
        qqs.append(jnp.concatenate([jnp.where(lo, q, 0.0), jnp.where(lo, 0.0, q)], axis=0).astype(BF16))

    def block(qq, kblk, vblk, carry):
        m, l, acc = carry
        s = lax.dot_general(qq, kblk, nt, preferred_element_type=F32)
        mn = jnp.maximum(m, jnp.max(s, axis=-1, keepdims=True))
        e = jnp.exp2(s - mn)
        r = jnp.exp2(m - mn)
        l = r * l + jnp.sum(e, axis=-1, keepdims=True)
        acc = r * acc + jnp.dot(e.astype(BF16), vblk, preferred_element_type=F32)
        return mn, l, acc

    def all_heads(kref, vref, rows, carries):
        return tuple(block(qqs[h], kref[rows, hsl[h]], vref[rows, hsl[h]], carries[h]) for h in range(hs))

    carries = tuple((jnp.full((2 * tq, 1), -jnp.inf, F32), jnp.zeros((2 * tq, 1), F32),
                     jnp.zeros((2 * tq, LANE), F32)) for _ in range(hs))
    if ctx:
        for c in range(ckb_ref.shape[0] // kb):
            carries = all_heads(ckb_ref, cvb_ref, slice(c * kb, (c + 1) * kb), carries)
    if n_new // kb == 1:
        carries = all_heads(kb_ref, vb_ref, slice(None), carries)
    else:
        def body(j, carries):
            return all_heads(kb_ref, vb_ref, pl.ds(pl.multiple_of(j * kb, kb), kb), carries)
        carries = lax.fori_loop(0, n_new // kb, body, carries)
    for h in range(hs):
        _, l, acc = carries[h]
        o = acc[:tq] * (1.0 / l[:tq]) - acc[tq:] * (lam_ref[:, :1] / l[tq:])
        y = o * lax.rsqrt(jnp.mean(o * o, axis=-1, keepdims=True) + EPS)
        o_ref[:, hsl[h]] = (y * g_ref[...]).astype(o_ref.dtype)


def _attention(p, lam_row, g_row, row0, B, T, tq, hs, kb, rope_tabs=None, ctx=None):
    nq = T // tq
    rb0 = row0 // tq
    kb0 = row0 // T
    W = hs * LANE
    hq, hk, hv = C_Q // W, C_K // W, C_V // W
    assert T % kb == 0
    in_specs = [pl.BlockSpec((tq, W), lambda b, h, i: (rb0 + b * nq + i, hq + h)),
                pl.BlockSpec((T, W), lambda b, h, i: (kb0 + b, hk + h)),
                pl.BlockSpec((T, W), lambda b, h, i: (kb0 + b, hv + h))]
    args = [p, p, p]
    scratch = [pltpu.VMEM((T, W), BF16), pltpu.VMEM((T, W), BF16)]
    if ctx is not None:
        ck, cv, layer = ctx
        past = ck.shape[2]
        assert past % kb == 0
        in_specs += [pl.BlockSpec((None, None, past, W), lambda b, h, i: (b, layer, 0, h)),
                     pl.BlockSpec((None, None, past, W), lambda b, h, i: (b, layer, 0, h))]
        args += [ck, cv]
        scratch += [pltpu.VMEM((past, W), BF16), pltpu.VMEM((past, W), BF16)]
    if rope_tabs is not None:
        cos, sin = rope_tabs
        in_specs += [pl.BlockSpec((tq, LANE), lambda b, h, i: (i, 0)), pl.BlockSpec((tq, LANE), lambda b, h, i: (i, 0)),
                     pl.BlockSpec((T, LANE), lambda b, h, i: (0, 0)), pl.BlockSpec((T, LANE), lambda b, h, i: (0, 0))]
        args += [cos, sin, cos, sin]
    in_specs += [pl.BlockSpec((1, LANE), lambda b, h, i: (0, 0)), pl.BlockSpec((1, LANE), lambda b, h, i: (0, 0))]
    args += [lam_row, g_row]
    return pl.pallas_call(
        functools.partial(_attn_kernel, rope=rope_tabs is not None, ctx=ctx is not None, hs=hs, kb=kb),
        grid=(B, DA_HEADS // hs, nq),
        in_specs=in_specs,
        out_specs=pl.BlockSpec((tq, W), lambda b, h, i: (b * nq + i, h)),
        out_shape=jax.ShapeDtypeStruct((B * T, DA_HEADS * DA_VDIM), BF16),
        scratch_shapes=scratch,
        compiler_params=_cp(("parallel", "parallel", "arbitrary"), 56),
        name="diff_attention_latent" if ctx is not None else "diff_attention_context",
    )(*args)


def _attn_latent_kernel(q_ref, k_ref, v_ref, ck_ref, cv_ref, cq_ref, sq_ref, ckk_ref, skk_ref, lam_ref, g_ref, o_ref,
                        kb_ref, vb_ref):
    @pl.when(pl.program_id(2) == 0)
    def _():
        kb_ref[...] = _rope(k_ref[...], ckk_ref[...], skk_ref[...]).astype(BF16)
        vb_ref[...] = v_ref[...].astype(BF16)

    q = _rope(q_ref[...], cq_ref[...], sq_ref[...]) * (DA_HEAD ** -0.5)
    lane = lax.broadcasted_iota(jnp.int32, (1, LANE), 1)
    lo = lane < DA_HEAD
    q0 = jnp.where(lo, q, 0.0).astype(BF16)
    q1 = jnp.where(lo, 0.0, q).astype(BF16)
    nt = (((1,), (1,)), ((), ()))
    kb = kb_ref[...]
    ckb = ck_ref[...].astype(BF16)
    s0 = lax.dot_general(q0, kb, nt, preferred_element_type=F32)
    s1 = lax.dot_general(q1, kb, nt, preferred_element_type=F32)
    c0 = lax.dot_general(q0, ckb, nt, preferred_element_type=F32)
    c1 = lax.dot_general(q1, ckb, nt, preferred_element_type=F32)
    m0 = jnp.maximum(jnp.max(s0, axis=-1, keepdims=True), jnp.max(c0, axis=-1, keepdims=True))
    m1 = jnp.maximum(jnp.max(s1, axis=-1, keepdims=True), jnp.max(c1, axis=-1, keepdims=True))
    e0 = jnp.exp(s0 - m0)
    e1 = jnp.exp(s1 - m1)
    ec0 = jnp.exp(c0 - m0)
    ec1 = jnp.exp(c1 - m1)
    l0 = jnp.sum(e0, axis=-1, keepdims=True) + jnp.sum(ec0, axis=-1, keepdims=True)
    l1 = jnp.sum(e1, axis=-1, keepdims=True) + jnp.sum(ec1, axis=-1, keepdims=True)
    a0 = 1.0 / l0
    a1 = lam_ref[:, :1] / l1
    o = jnp.dot((e0 * a0 - e1 * a1).astype(BF16), vb_ref[...], preferred_element_type=F32)
    o = o + jnp.dot((ec0 * a0 - ec1 * a1).astype(BF16), cv_ref[...].astype(BF16), preferred_element_type=F32)
    y = o * lax.rsqrt(jnp.mean(o * o, axis=-1, keepdims=True) + EPS)
    o_ref[...] = (y * g_ref[...]).astype(o_ref.dtype)


def _attention_latent(p, lam_row, g_row, row0, B, T, tq, rope_tabs, ck, cv, layer):
    nq = T // tq
    rb0 = row0 // tq
    kb0 = row0 // T
    hq, hk, hv = C_Q // LANE, C_K // LANE, C_V // LANE
    past = ck.shape[2]
    cos, sin = rope_tabs
    in_specs = [pl.BlockSpec((tq, LANE), lambda b, h, i: (rb0 + b * nq + i, hq + h)),
                pl.BlockSpec((T, LANE), lambda b, h, i: (kb0 + b, hk + h)),
                pl.BlockSpec((T, LANE), lambda b, h, i: (kb0 + b, hv + h)),
                pl.BlockSpec((None, None, past, LANE), lambda b, h, i: (b, layer, 0, h)),
                pl.BlockSpec((None, None, past, LANE), lambda b, h, i: (b, layer, 0, h)),
                pl.BlockSpec((tq, LANE), lambda b, h, i: (i, 0)), pl.BlockSpec((tq, LANE), lambda b, h, i: (i, 0)),
                pl.BlockSpec((T, LANE), lambda b, h, i: (0, 0)), pl.BlockSpec((T, LANE), lambda b, h, i: (0, 0)),
                pl.BlockSpec((1, LANE), lambda b, h, i: (0, 0)), pl.BlockSpec((1, LANE), lambda b, h, i: (0, 0))]
    return pl.pallas_call(
        _attn_latent_kernel, grid=(B, DA_HEADS, nq), in_specs=in_specs,
        out_specs=pl.BlockSpec((tq, LANE), lambda b, h, i: (b * nq + i, h)),
        out_shape=jax.ShapeDtypeStruct((B * T, DA_HEADS * DA_VDIM), BF16),
        scratch_shapes=[pltpu.VMEM((T, LANE), BF16), pltpu.VMEM((T, LANE), BF16)],
        compiler_params=_cp(("parallel", "parallel", "arbitrary"), 56),
        name="diff_attention_latent",
    )(p, p, p, ck, cv, cos, sin, cos, sin, lam_row, g_row)


def _log_sigmoid(z):
    return jnp.minimum(z, 0.0) - jnp.log1p(jnp.exp(-jnp.abs(z)))


def _gla_direction(d, q_ref, k_ref, v_ref, ga_ref, wa_ref, ba_ref, tri_ref, o_ref, st_ref):
    C = GLA_CHUNK
    nsub = GLA_ROWS // C
    z = jnp.dot(ga_ref[...].astype(BF16), wa_ref[0, d], preferred_element_type=F32) + ba_ref[0, d]
    la = _log_sigmoid(z) / GLA_TAU
    hi = la.astype(BF16)
    r1 = la - hi.astype(F32)
    mid = r1.astype(BF16)
    low = (r1 - mid.astype(F32)).astype(BF16)
    tri = tri_ref[d]
    bsum = (jnp.dot(tri, hi, preferred_element_type=F32) + jnp.dot(tri, mid, preferred_element_type=F32)
            + jnp.dot(tri, low, preferred_element_type=F32))
    row = lax.broadcasted_iota(jnp.int32, (C, C), 0)
    col = lax.broadcasted_iota(jnp.int32, (C, C), 1)
    mask = (row >= col) if d == 0 else (row <= col)
    nt = (((1,), (1,)), ((), ()))
    tn = (((0,), (0,)), ((), ()))
    subs = range(nsub) if d == 0 else range(nsub - 1, -1, -1)
    for h in range(GLA_HEADS):
        hk = slice(h * GLA_DK, (h + 1) * GLA_DK)
        hv = slice(h * GLA_DV, (h + 1) * GLA_DV)
        st = st_ref[d, h]
        for s in subs:
            rs = slice(s * C, (s + 1) * C)
            end = s * C + C - 1 if d == 0 else s * C
            bb = bsum[rs, hk]
            bl = bsum[end:end + 1, hk]
            q = q_ref[rs, hk] * (GLA_DK ** -0.5)
            k = k_ref[rs, hk]
            v = v_ref[rs, hv].astype(BF16)
            qt = (q * jnp.exp(bb)).astype(BF16)
            kt = (k * jnp.exp(-bb)).astype(BF16)
            kend = (k * jnp.exp(bl - bb)).astype(BF16)
            att = jnp.where(mask, lax.dot_general(qt, kt, nt, preferred_element_type=F32), 0.0).astype(BF16)
            o = jnp.dot(att, v, preferred_element_type=F32)
            o = o + lax.dot_general(qt, st.astype(BF16), nt, preferred_element_type=F32)
            o_ref[rs, hv] = o
            st = st * jnp.exp(bl) + lax.dot_general(v, kend, tn, preferred_element_type=F32)
        st_ref[d, h] = st


def _gla_kernel(qf, kf, vf, gaf, qb, kb, vb, gab, wa_ref, ba_ref, tri_ref, s0_ref, of_ref, ob_ref, fin_ref, st_ref,
                *, seq_pos):
    first, last, stateless = seq_pos(pl.program_id(0))

    @pl.when(first & stateless)
    def _():
        st_ref[...] = jnp.zeros(st_ref.shape, F32)

    @pl.when(first & jnp.logical_not(stateless))
    def _():
        for d in range(2):
            for h in range(GLA_HEADS):
                st_ref[d, h] = s0_ref[d, h].T

    _gla_direction(0, qf, kf, vf, gaf, wa_ref, ba_ref, tri_ref, of_ref, st_ref)
    _gla_direction(1, qb, kb, vb, gab, wa_ref, ba_ref, tri_ref, ob_ref, st_ref)

    @pl.when(last & stateless)
    def _():
        for d in range(2):
            for h in range(GLA_HEADS):
                fin_ref[d, h] = st_ref[d, h].T


def _gla(p, wa, ba, tri, state, layer, seqs):
    N = p.shape[0]
    R = GLA_ROWS
    (n1, t1), (n2, t2) = seqs
    nb1, nb2 = t1 // R, t2 // R
    steps1 = n1 * nb1

    def locate(s):
        in1 = s < steps1
        s2 = s - steps1
        nb = jnp.where(in1, nb1, nb2)
        seq = jnp.where(in1, s // nb1, n1 + s2 // nb2)
        n = jnp.where(in1, s % nb1, s2 % nb2)
        base = jnp.where(in1, (s // nb1) * nb1, steps1 + (s2 // nb2) * nb2)
        return seq, n, nb, base

    def fwd_blk(s):
        _, n, _, base = locate(s)
        return base + n

    def bwd_blk(s):
        _, n, nb, base = locate(s)
        return base + nb - 1 - n

    def seq_pos(s):
        _, n, nb, _ = locate(s)
        return n == 0, n == nb - 1, s < steps1

    def specs(blk):
        return [pl.BlockSpec((R, GLA_KTOT), lambda s: (blk(s), C_GQ // GLA_KTOT)),
                pl.BlockSpec((R, GLA_KTOT), lambda s: (blk(s), C_GK // GLA_KTOT)),
                pl.BlockSpec((R, GLA_WIDTH), lambda s: (blk(s), C_GV // GLA_WIDTH)),
                pl.BlockSpec((R, LANE), lambda s: (blk(s), C_GA // LANE))]

    s0_spec = pl.BlockSpec((None, None, 2, GLA_HEADS, GLA_DK, GLA_DV),
                           lambda s: (jnp.maximum(locate(s)[0] - n1, 0), layer, 0, 0, 0, 0))
    fin_spec = pl.BlockSpec((None, 2, GLA_HEADS, GLA_DK, GLA_DV),
                            lambda s: (jnp.minimum(locate(s)[0], n1 - 1), 0, 0, 0, 0))
    return pl.pallas_call(
        functools.partial(_gla_kernel, seq_pos=seq_pos),
        grid=(N // R,),
        in_specs=specs(fwd_blk) + specs(bwd_blk) + [
            pl.BlockSpec((1, 2, LANE, GLA_KTOT), lambda s: (0, 0, 0, 0)),
            pl.BlockSpec((1, 2, 1, GLA_KTOT), lambda s: (0, 0, 0, 0)),
            pl.BlockSpec((2, R, R), lambda s: (0, 0, 0)),
            s0_spec],
        out_specs=[pl.BlockSpec((R, GLA_WIDTH), lambda s: (fwd_blk(s), 0)),
                   pl.BlockSpec((R, GLA_WIDTH), lambda s: (bwd_blk(s), 0)),
                   fin_spec],
        out_shape=[jax.ShapeDtypeStruct((N, GLA_WIDTH), F32), jax.ShapeDtypeStruct((N, GLA_WIDTH), F32),
                   jax.ShapeDtypeStruct((n1, 2, GLA_HEADS, GLA_DK, GLA_DV), F32)],
        scratch_shapes=[pltpu.VMEM((2, GLA_HEADS, GLA_DV, GLA_DK), F32)],
        compiler_params=_cp(("arbitrary",)),
        name="gla_bidir",
    )(p, p, p, p, p, p, p, p, wa, ba, tri, state)


def _gla_post_kernel(of_ref, ob_ref, r_ref, g_ref, o_ref):
    for h in range(GLA_HEADS):
        hv = slice(h * GLA_DV, (h + 1) * GLA_DV)
        o = of_ref[:, hv] + ob_ref[:, hv]
        y = o * lax.rsqrt(jnp.mean(o * o, axis=-1, keepdims=True) + EPS) * g_ref[...]
        r = r_ref[:, hv]
        o_ref[:, hv] = (y * (r * jax.nn.sigmoid(r))).astype(o_ref.dtype)


def _gla_post(of, ob, p, g, tm):
    N = of.shape[0]
    return pl.pallas_call(
        _gla_post_kernel, grid=(N // tm,),
        in_specs=[pl.BlockSpec((tm, GLA_WIDTH), lambda i: (i, 0)), pl.BlockSpec((tm, GLA_WIDTH), lambda i: (i, 0)),
                  pl.BlockSpec((tm, GLA_WIDTH), lambda i: (i, C_GR // GLA_WIDTH)),
                  pl.BlockSpec((1, GLA_DV), lambda i: (0, 0))],
        out_specs=pl.BlockSpec((tm, GLA_WIDTH), lambda i: (i, 0)),
        out_shape=jax.ShapeDtypeStruct((N, GLA_WIDTH), BF16),
        compiler_params=_cp(("parallel",)), name="gla_norm_gate",
    )(of, ob, p, g)


def _s5_matrices(a_re, a_im, log_dt, b_re, b_im, c_re, c_im):
    hp = lax.Precision.HIGHEST
    L, G, P, CH = S5_L, S5_GROUPS, S5_STATE, S5_CH
    taus = jnp.arange(L + 1, dtype=F32)
    ii = jnp.arange(L)
    kerns, powers, bbars, carry, decay = [], [], [], [], []
    c_re_t, c_im_t = c_re.transpose(2, 0, 1), c_im.transpose(2, 0, 1)
    for d in range(2):
        dt = jnp.exp(log_dt[d])[:, None]
        ld_re, ld_im = a_re[d] * dt, a_im[d] * dt
        mag = jnp.exp(ld_re)
        lb_re, lb_im = mag * jnp.cos(ld_im), mag * jnp.sin(ld_im)
        den = a_re[d] * a_re[d] + a_im[d] * a_im[d]
        q_re = ((lb_re - 1.0) * a_re[d] + lb_im * a_im[d]) / den
        q_im = (lb_im * a_re[d] - (lb_re - 1.0) * a_im[d]) / den
        bb_re = q_re[..., None] * b_re - q_im[..., None] * b_im
        bb_im = q_re[..., None] * b_im + q_im[..., None] * b_re
        pmag = jnp.exp(ld_re[None] * taus[:, None, None])
        pw_re = pmag * jnp.cos(ld_im[None] * taus[:, None, None])
        pw_im = pmag * jnp.sin(ld_im[None] * taus[:, None, None])
        cp_re = c_re[None] * pw_re[:L, :, None, :] - c_im[None] * pw_im[:L, :, None, :]
        cp_im = c_re[None] * pw_im[:L, :, None, :] + c_im[None] * pw_re[:L, :, None, :]
        kern = (jnp.einsum('tgop,gpc->tcgo', cp_re, bb_re, precision=hp)
                - jnp.einsum('tgop,gpc->tcgo', cp_im, bb_im, precision=hp)).reshape(L, CH, G * CH)
        kerns.append(kern)
        powers.append(jnp.stack([pw_re.reshape(L + 1, G * P), pw_im.reshape(L + 1, G * P)]))
        bbars.append(jnp.stack([bb_re.transpose(2, 0, 1).reshape(CH, G * P), bb_im.transpose(2, 0, 1).reshape(CH, G * P)]))
        e = (ii + 1) if d == 0 else (L - ii)
        pt_re, pt_im = pw_re[e].transpose(0, 2, 1)[..., None], pw_im[e].transpose(0, 2, 1)[..., None]
        cw_re = c_re_t[None] * pt_re - c_im_t[None] * pt_im
        cw_im = c_re_t[None] * pt_im + c_im_t[None] * pt_re
        carry += [cw_re.reshape(L, P, G * CH), (-cw_im).reshape(L, P, G * CH)]
        decay.append((pw_re[L], pw_im[L]))
    kk = jnp.stack([kerns[0].at[0].add(kerns[1][0]), kerns[1]]).astype(BF16)
    pw = jnp.stack(powers)
    bb = jnp.stack(bbars)
    cw = jnp.stack(carry).astype(BF16)
    dec = jnp.stack([jnp.stack([x[0], x[1]]) for x in decay])
    dec = dec.reshape(4, S5_SG, S5_SW // 4).transpose(1, 0, 2)
    return kk, pw, bb, cw, dec


def _s5_slab(u_ref, rt):
    return jnp.concatenate([u_ref[pl.ds(j, rt, stride=S5_L), :] for j in range(S5_L)], axis=1).astype(BF16)


def _s5_group_masks(width, per_group):
    grp = (lax.broadcasted_iota(jnp.int32, (1, width), 1) // per_group) % S5_GB
    return [grp == a for a in range(S5_GB)]


def _s5_state_kernel(u_ref, pw_ref, bb_ref, z_ref, w_ref):
    @pl.when(pl.program_id(1) == 0)
    def _():
        Q = S5_SW // 4
        masks = _s5_group_masks(Q, S5_STATE)
        for d in range(2):
            b_re, b_im = bb_ref[d, 0], bb_ref[d, 1]
            for j in range(S5_L):
                e = S5_L - 1 - j if d == 0 else j
                p_re, p_im = pw_ref[d, 0, e:e + 1, :], pw_ref[d, 1, e:e + 1, :]
                parts = ((p_re * b_re - p_im * b_im).astype(BF16), (p_re * b_im + p_im * b_re).astype(BF16))
                for ri in range(2):
                    q = 2 * d + ri
                    for a in range(S5_GB):
                        r0 = j * LANE + a * S5_CH
                        w_ref[r0:r0 + S5_CH, q * Q:(q + 1) * Q] = jnp.where(masks[a], parts[ri],
                                                                            jnp.zeros_like(parts[ri]))

    z_ref[...] = jnp.dot(_s5_slab(u_ref, z_ref.shape[0]), w_ref[...], preferred_element_type=F32)


def _s5_out_kernel(u_ref, p_ref, k_ref, c_ref, y_ref, wi_ref, wc_ref):
    @pl.when(pl.program_id(1) == 0)
    def _():
        Q = S5_SW // 4
        masks = _s5_group_masks(LANE, S5_CH)
        for i in range(S5_L):
            cols = slice(i * LANE, (i + 1) * LANE)
            for j in range(S5_L):
                blk = k_ref[0 if i >= j else 1, abs(i - j)]
                for a in range(S5_GB):
                    r0 = j * LANE + a * S5_CH
                    wi_ref[r0:r0 + S5_CH, cols] = jnp.where(masks[a], blk, jnp.zeros_like(blk))
            for q in range(4):
                blk = c_ref[q, i]
                for b in range(S5_GB):
                    r0 = q * Q + b * S5_STATE
                    wc_ref[r0:r0 + S5_STATE, cols] = jnp.where(masks[b], blk, jnp.zeros_like(blk))

    rt = p_ref.shape[0]
    y = (jnp.dot(_s5_slab(u_ref, rt), wi_ref[...], preferred_element_type=F32)
         + jnp.dot(p_ref[...].astype(BF16), wc_ref[...], preferred_element_type=F32))
    for i in range(S5_L):
        y_ref[pl.ds(i, rt, stride=S5_L), :] = y[:, i * LANE:(i + 1) * LANE]


def _s5_scan_kernel(z_ref, dec_ref, x0_ref, p_ref, fin_ref, *, segments):
    Q = S5_SW // 4
    dec = [(dec_ref[0, 2 * d:2 * d + 1, :], dec_ref[0, 2 * d + 1:2 * d + 2, :]) for d in range(2)]
    seq_base = 0
    for (row0, n_seq, n_chunk) in segments:
        def seq_loop(sq, carry, row0=row0, n_chunk=n_chunk, seq_base=seq_base):
            r0 = row0 + sq * n_chunk
            sid = seq_base + sq
            x_init = tuple((x0_ref[0, pl.ds(4 * sid + 2 * d, 1), :], x0_ref[0, pl.ds(4 * sid + 2 * d + 1, 1), :])
                           for d in range(2))

            def step(n, xs):
                out = []
                for d in range(2):
                    xr, xi = xs[d]
                    ar, ai = dec[d]
                    r = r0 + (n if d == 0 else n_chunk - 1 - n)
                    cr = slice(2 * d * Q, (2 * d + 1) * Q)
                    ci = slice((2 * d + 1) * Q, (2 * d + 2) * Q)
                    zr = z_ref[pl.ds(r, 1), cr]
                    zi = z_ref[pl.ds(r, 1), ci]
                    p_ref[pl.ds(r, 1), cr] = xr
                    p_ref[pl.ds(r, 1), ci] = xi
                    out.append((ar * xr - ai * xi + zr, ar * xi + ai * xr + zi))
                return tuple(out)

            xs = lax.fori_loop(0, n_chunk, step, x_init)
            for d in range(2):
                fin_ref[0, pl.ds(4 * sid + 2 * d, 1), :] = xs[d][0]
                fin_ref[0, pl.ds(4 * sid + 2 * d + 1, 1), :] = xs[d][1]
            return carry

        lax.fori_loop(0, n_seq, seq_loop, 0)
        seq_base += n_seq


def _s5(p, weights, x0, segments):
    kk, pw, bb, cw, dec = weights
    N = p.shape[0]
    NC = N // S5_L
    SG = S5_SG
    Q = S5_SW // 4
    rt = _pick(256, NC)
    nrow = x0.shape[1]
    cb = C_S5 // LANE
    u_spec = pl.BlockSpec((rt * S5_L, LANE), lambda s, r: (r, cb + s))
    slab_spec = pl.BlockSpec((rt, S5_SW), lambda s, r: (r, s))
    z = pl.pallas_call(
        _s5_state_kernel, grid=(SG, NC // rt),
        in_specs=[u_spec, pl.BlockSpec((2, 2, S5_L + 1, Q), lambda s, r: (0, 0, 0, s)),
                  pl.BlockSpec((2, 2, S5_CH, Q), lambda s, r: (0, 0, 0, s))], out_specs=slab_spec,
        out_shape=jax.ShapeDtypeStruct((NC, SG * S5_SW), F32),
        scratch_shapes=[pltpu.VMEM((S5_W, S5_SW), BF16)],
        compiler_params=_cp(("parallel", "arbitrary")), name="s5_chunk_state",
    )(p, pw, bb)
    state_spec = pl.BlockSpec((1, nrow, Q), lambda s: (s, 0, 0))
    carry_in, fin = pl.pallas_call(
        functools.partial(_s5_scan_kernel, segments=segments), grid=(SG,),
        in_specs=[pl.BlockSpec((NC, S5_SW), lambda s: (0, s)), pl.BlockSpec((1, 4, Q), lambda s: (s, 0, 0)),
                  state_spec],
        out_specs=[pl.BlockSpec((NC, S5_SW), lambda s: (0, s)), state_spec],
        out_shape=[jax.ShapeDtypeStruct((NC, SG * S5_SW), F32), jax.ShapeDtypeStruct((SG, nrow, Q), F32)],
        compiler_params=_cp(("parallel",)), name="s5_chunk_scan",
    )(z, dec, x0)
    y = pl.pallas_call(
        _s5_out_kernel, grid=(SG, NC // rt),
        in_specs=[u_spec, slab_spec, pl.BlockSpec((2, S5_L, S5_CH, LANE), lambda s, r: (0, 0, 0, s)),
                  pl.BlockSpec((4, S5_L, S5_STATE, LANE), lambda s, r: (0, 0, 0, s))],
        out_specs=pl.BlockSpec((rt * S5_L, LANE), lambda s, r: (r, s)),
        out_shape=jax.ShapeDtypeStruct((N, S5_WIDTH), F32),
        scratch_shapes=[pltpu.VMEM((S5_W, S5_W), BF16), pltpu.VMEM((S5_SW, S5_W), BF16)],
        compiler_params=_cp(("parallel", "arbitrary"), 56), name="s5_chunk_out",
    )(p, carry_in, kk, cw)
    return y, fin


def _s5_post_kernel(y_ref, u_ref, d_ref, w_ref, b_ref, o_ref):
    yg = jax.nn.gelu(y_ref[...] + d_ref[...] * u_ref[...])
    z = jnp.dot(yg.astype(BF16), w_ref[...], preferred_element_type=F32) + b_ref[...]
    o_ref[...] = (yg * jax.nn.sigmoid(z)).astype(o_ref.dtype)


def _s5_post(y, p, d, w, b, tm):
    N = y.shape[0]
    W = S5_WIDTH
    return pl.pallas_call(
        _s5_post_kernel, grid=(N // tm,),
        in_specs=[pl.BlockSpec((tm, W), lambda i: (i, 0)), pl.BlockSpec((tm, W), lambda i: (i, C_S5 // W)),
                  pl.BlockSpec((1, W), lambda i: (0, 0)), pl.BlockSpec((W, W), lambda i: (0, 0)),
                  pl.BlockSpec((1, W), lambda i: (0, 0))],
        out_specs=pl.BlockSpec((tm, W), lambda i: (i, 0)),
        out_shape=jax.ShapeDtypeStruct((N, W), BF16),
        compiler_params=_cp(("parallel",)), name="s5_glu",
    )(y, p, d, w, b)


def _merge_kernel(h_ref, a_ref, b_ref, c_ref, wg0, wg1, wg2, bg0, bg1, bg2, wb_ref, o_ref):
    h = h_ref[...]
    acc = None
    for i, (br, wg, bg) in enumerate(((a_ref, wg0, bg0), (b_ref, wg1, bg1), (c_ref, wg2, bg2))):
        gate = jax.nn.sigmoid(jnp.dot(h, wg[...], preferred_element_type=F32) + bg[...])
        t = gate * jnp.dot(br[...], wb_ref[i], preferred_element_type=F32)
        acc = t if acc is None else acc + t
    o_ref[...] = acc.astype(o_ref.dtype)


def _merge(h, o_da, o_gla, o_s5, w_gate, b_gate, w_branch, tm, tn):
    N, D = h.shape
    W = o_da.shape[1]
    nt = D // tn
    act = pl.BlockSpec((tm, W), lambda j, i: (i, 0))
    in_specs = [pl.BlockSpec((tm, D), lambda j, i: (i, 0)), act, act, act]
    in_specs += [pl.BlockSpec((D, tn), lambda j, i, br=br: (0, br * nt + j)) for br in range(N_BRANCH)]
    in_specs += [pl.BlockSpec((1, tn), lambda j, i, br=br: (0, br * nt + j)) for br in range(N_BRANCH)]
    in_specs += [pl.BlockSpec((N_BRANCH, W, tn), lambda j, i: (0, 0, j))]
    return pl.pallas_call(
        _merge_kernel, grid=(nt, N // tm), in_specs=in_specs,
        out_specs=pl.BlockSpec((tm, tn), lambda j, i: (i, j)),
        out_shape=jax.ShapeDtypeStruct((N, D), BF16),
        compiler_params=_cp(("parallel", "parallel"), 56), name="gated_merge",
    )(h, o_da, o_gla, o_s5, w_gate, w_gate, w_gate, b_gate, b_gate, b_gate, w_branch)


def _wout_kernel(m_ref, w_ref, x_ref, g_ref, o_ref):
    o_ref[...] = x_ref[...] + g_ref[0] * jnp.dot(m_ref[...], w_ref[...], preferred_element_type=F32)


def _wout(merged, w, x, gate, midx, tm, tn):
    N, D = x.shape
    return pl.pallas_call(
        _wout_kernel, grid=(D // tn, N // tm),
        in_specs=[pl.BlockSpec((tm, D), lambda j, i: (i, 0)), pl.BlockSpec((D, tn), lambda j, i: (0, j)),
                  pl.BlockSpec((tm, tn), lambda j, i: (i, j)), pl.BlockSpec((1, 1, tn), lambda j, i: (midx(i), 0, j))],
        out_specs=pl.BlockSpec((tm, tn), lambda j, i: (i, j)),
        out_shape=jax.ShapeDtypeStruct((N, D), F32),
        compiler_params=_cp(("parallel", "parallel")), name="out_proj_residual",
    )(merged, w, x, gate)


def _route_kernel(lg_ref, bias_ref, asg_ref, tile_ref, rank_ref, *, tl):
    E, n_tok = lg_ref.shape
    EPG = EXPERTS_PER_GROUP
    ninf = -jnp.inf
    eio = lax.broadcasted_iota(jnp.int32, (E, tl), 0)
    upper = jnp.where(lax.broadcasted_iota(jnp.int32, (tl, tl), 0) < lax.broadcasted_iota(jnp.int32, (tl, tl), 1),
                      1.0, 0.0).astype(BF16)
    ecol = lax.broadcasted_iota(jnp.int32, (E, 1), 0)

    def select(j, carry):
        ls = pl.ds(pl.multiple_of(j * tl, tl), tl)
        s = jax.nn.sigmoid(lg_ref[:, ls])
        sel = s + bias_ref[...]
        rows = [sel[e:e + 1, :] for e in range(E)]
        srow = [s[e:e + 1, :] for e in range(E)]
        gi = jnp.zeros((1, tl), jnp.int32)
        best = None
        for g in range(N_EXPERT_GROUPS):
            a, b, c, d = rows[g * EPG:(g + 1) * EPG]
            hi1, lo1, hi2, lo2 = jnp.maximum(a, b), jnp.minimum(a, b), jnp.maximum(c, d), jnp.minimum(c, d)
            gs = jnp.maximum(hi1, hi2) + jnp.maximum(jnp.minimum(hi1, hi2), jnp.maximum(lo1, lo2))
            if best is None:
                best = gs
            else:
                upd = gs > best
                gi = jnp.where(upd, g, gi)
                best = jnp.where(upd, gs, best)

        def pick(rs, i):
            out = rs[(N_EXPERT_GROUPS - 1) * EPG + i]
            for g in range(N_EXPERT_GROUPS - 2, -1, -1):
                out = jnp.where(gi == g, rs[g * EPG + i], out)
            return out

        v = [pick(rows, i) for i in range(EPG)]
        sv = [pick(srow, i) for i in range(EPG)]
        i1, b1, s1 = jnp.zeros((1, tl), jnp.int32), v[0], sv[0]
        for i in range(1, EPG):
            upd = v[i] > b1
            i1, b1, s1 = jnp.where(upd, i, i1), jnp.where(upd, v[i], b1), jnp.where(upd, sv[i], s1)
        i2, b2, s2 = jnp.zeros((1, tl), jnp.int32), jnp.full((1, tl), ninf, F32), sv[0]
        for i in range(EPG):
            cand = jnp.where(i1 == i, ninf, v[i])
            upd = cand > b2
            i2, b2, s2 = jnp.where(upd, i, i2), jnp.where(upd, cand, b2), jnp.where(upd, sv[i], s2)
        e0 = gi * EPG + i1
        e1 = gi * EPG + i2
        den = s1 + s2
        asg_ref[2:3, ls] = s1 / den
        asg_ref[3:4, ls] = s2 / den
        asg_ref[4:5, ls] = e0.astype(F32)
        asg_ref[5:6, ls] = e1.astype(F32)
        m = jnp.where((eio == e0) | (eio == e1), 1.0, 0.0)
        rank_ref[:, ls] = jnp.dot(m.astype(BF16), upper, preferred_element_type=F32) + carry
        return carry + jnp.sum(m, axis=1, keepdims=True)

    counts = lax.fori_loop(0, n_tok // tl, select, jnp.zeros((E, 1), F32))
    ptiles = jnp.floor((counts + (MOE_TM - 1)) * (1.0 / MOE_TM))
    run = jnp.zeros((1, 1), F32)
    pstart_t = jnp.zeros((E, 1), F32)
    for e in range(E):
        pstart_t = jnp.where(ecol == e, run, pstart_t)
        run = run + ptiles[e:e + 1, :]
    pstart = pstart_t * MOE_TM

    def place(j, _):
        ls = pl.ds(pl.multiple_of(j * tl, tl), tl)
        dest = rank_ref[:, ls] + pstart
        e0 = asg_ref[4:5, ls].astype(jnp.int32)
        e1 = asg_ref[5:6, ls].astype(jnp.int32)
        asg_ref[0:1, ls] = jnp.sum(jnp.where(eio == e0, dest, 0.0), axis=0, keepdims=True)
        asg_ref[1:2, ls] = jnp.sum(jnp.where(eio == e1, dest, 0.0), axis=0, keepdims=True)
        asg_ref[6:8, ls] = jnp.zeros((2, tl), F32)
        return 0

    lax.fori_loop(0, n_tok // tl, place, 0)

    ntp = tile_ref.shape[1]
    pend_t = pstart_t + ptiles
    ti = lax.broadcasted_iota(jnp.int32, (1, ntp), 1).astype(F32)
    te = jnp.sum(jnp.where(pend_t <= ti, 1.0, 0.0), axis=0, keepdims=True)
    used = ti < run
    last_e = jnp.max(jnp.where(counts > 0.0, ecol.astype(F32), 0.0), axis=0, keepdims=True)
    te = jnp.where(used, jnp.minimum(te, E - 1.0), last_e)
    tile_ref[0:1, :] = te.astype(jnp.int32)
    tile_ref[1:2, :] = jnp.where(used, 1, 0).astype(jnp.int32)
    tile_ref[2:8, :] = jnp.zeros((6, ntp), jnp.int32)


def _route(logits_t, router_bias, ntile):
    E, N = logits_t.shape
    assert MOE_TM & (MOE_TM - 1) == 0
    ntp = -(-ntile // LANE) * LANE
    full = lambda shape: pl.BlockSpec(shape, lambda i: (0, 0))
    return pl.pallas_call(
        functools.partial(_route_kernel, tl=_pick(512, N)), grid=(1,),
        in_specs=[full((E, N)), full((E, 1))],
        out_specs=[full((8, N)), full((8, ntp))],
        out_shape=[jax.ShapeDtypeStruct((8, N), F32), jax.ShapeDtypeStruct((8, ntp), jnp.int32)],
        scratch_shapes=[pltpu.VMEM((E, N), F32)],
        compiler_params=_cp(("arbitrary",)), name="moe_route",
    )(logits_t, router_bias.astype(F32).reshape(E, 1))


def _moe_kernel(te_ref, tv_ref, x_ref, w1_ref, w3_ref, w2_ref, o_ref):
    valid = tv_ref[pl.program_id(0)] != 0

    @pl.when(valid)
    def _():
        x = x_ref[...]
        F = w1_ref.shape[2]
        acc = None
        for f0 in range(0, F, MOE_FC):
            fs = slice(f0, min(f0 + MOE_FC, F))
            a = jnp.dot(x, w1_ref[0, :, fs], preferred_element_type=F32)
            b = jnp.dot(x, w3_ref[0, :, fs], preferred_element_type=F32)
            hid = ((a * jax.nn.sigmoid(a)) * b).astype(BF16)
            t = jnp.dot(hid, w2_ref[0, fs, :], preferred_element_type=F32)
            acc = t if acc is None else acc + t
        o_ref[...] = acc.astype(o_ref.dtype)

    @pl.when(jnp.logical_not(valid))
    def _():
        o_ref[...] = jnp.zeros(o_ref.shape, o_ref.dtype)


def _moe(xin, w1, w3, w2, layer, tile_e, tile_valid):
    cap, D = xin.shape
    F = w1.shape[3]
    TM = MOE_TM
    grid_spec = pltpu.PrefetchScalarGridSpec(
        num_scalar_prefetch=2, grid=(cap // TM,),
        in_specs=[pl.BlockSpec((TM, D), lambda i, te, tv: (i, 0)),
                  pl.BlockSpec((None, 1, D, F), lambda i, te, tv: (layer, te[i], 0, 0)),
                  pl.BlockSpec((None, 1, D, F), lambda i, te, tv: (layer, te[i], 0, 0)),
                  pl.BlockSpec((None, 1, F, D), lambda i, te, tv: (layer, te[i], 0, 0))],
        out_specs=pl.BlockSpec((TM, D), lambda i, te, tv: (i, 0)))
    return pl.pallas_call(
        _moe_kernel, grid_spec=grid_spec,
        out_shape=jax.ShapeDtypeStruct((cap, D), BF16),
        compiler_params=_cp(("arbitrary",), 58), name="moe_experts",
    )(tile_e, tile_valid, xin, w1, w3, w2)


def _combine_kernel(x_ref, y0_ref, y1_ref, w0_ref, w1_ref, g_ref, o_ref):
    o_ref[...] = x_ref[...] + g_ref[0] * (y0_ref[...] * w0_ref[...] + y1_ref[...] * w1_ref[...])


def _combine(x, y0, y1, w0, w1, gate, midx, tm):
    N, D = x.shape
    blk = pl.BlockSpec((tm, D), lambda i: (i, 0))
    col = pl.BlockSpec((tm, 1), lambda i: (i, 0))
    return pl.pallas_call(
        _combine_kernel, grid=(N // tm,),
        in_specs=[blk, blk, blk, col, col, pl.BlockSpec((1, 1, D), lambda i: (midx(i), 0, 0))],
        out_specs=blk, out_shape=jax.ShapeDtypeStruct((N, D), F32),
        compiler_params=_cp(("parallel",)), name="moe_residual",
    )(x, y0, y1, w0, w1, gate)


def _rope_tables(n_tokens):
    rows = n_tokens // GRID_W
    row = jnp.repeat(jnp.arange(rows, dtype=F32), GRID_W)
    col = jnp.tile(jnp.arange(GRID_W, dtype=F32), rows)
    inv = ROPE_BASE ** (-jnp.arange(0, AX_DIM, 2, dtype=F32) / AX_DIM)
    ar = row[:, None] * inv[None, :]
    ac = col[:, None] * inv[None, :]
    ang = jnp.tile(jnp.concatenate([ar, ar, ac, ac], axis=-1), (1, LANE // DA_HEAD))
    first = (jnp.arange(LANE) % (AX_DIM)) < (AX_DIM // 2)
    return jnp.cos(ang), jnp.where(first[None, :], -jnp.sin(ang), jnp.sin(ang))


def kernel(x_prompt, x_sample, cache_k, cache_v, state_gla, state_s5_re, state_s5_im, c, c_ctx, w_mod, b_mod, g_norm, w_in, da_lambda_q, da_lambda_k, da_subln_g, gla_w_alpha, gla_b_alpha, gla_norm_g, s5_a_re, s5_a_im, s5_log_dt, s5_b_re, s5_b_im, s5_c_re, s5_c_im, s5_d, s5_w_glu, s5_b_glu, w_branch, w_gate, b_gate, w_out, router_w, router_bias, moe_w1, moe_w3, moe_w2, g_final):
    B, T, D = x_prompt.shape
    DB, DT, _ = x_sample.shape
    L = w_in.shape[0]
    PAST = cache_k.shape[2]
    NP, NS = B * T, DB * DT
    N = NP + NS
    nseq = B + DB
    assert D == D_MODEL and 1 + DB <= 8 and NP % DT == 0 and T % GLA_ROWS == 0 and DT % GLA_ROWS == 0
    tm = _pick(512, NP, DT)
    midx = functools.partial(_mod_index, tm=tm, n_ctx_tok=NP, dec_seq=DT)
    tm_mm = _pick(1024, NP, DT)

    x = jnp.concatenate([x_prompt.reshape(NP, D), x_sample.reshape(NS, D)], axis=0)
    cv = jnp.zeros((8, D), F32).at[0].set(c_ctx).at[1:1 + DB].set(c)
    mods = _mod(cv, w_mod, b_mod).reshape(L, 8, N_MOD, 1, D)
    zero_mod = jnp.zeros((8, 1, D), F32)

    rope_tabs = _rope_tables(DT)
    ck = cache_k.reshape(DB, L, PAST, DA_HEADS * 2 * DA_HEAD)
    cvv = cache_v.reshape(DB, L, PAST, DA_HEADS * DA_VDIM)
    r = jnp.arange(GLA_ROWS)
    same = (r[:, None] // GLA_CHUNK) == (r[None, :] // GLA_CHUNK)
    tri = jnp.stack([same & (r[None, :] <= r[:, None]), same & (r[None, :] >= r[:, None])]).astype(BF16)
    rw = router_w.astype(F32).T
    rw_hi = rw.astype(BF16)
    rw_lo = (rw - rw_hi.astype(F32)).astype(BF16)
    ntile = N * TOP_K // MOE_TM + N_EXPERTS
    tok2 = jnp.tile(jnp.arange(N, dtype=jnp.int32), TOP_K)
    moe_w1_bf, moe_w3_bf, moe_w2_bf = _cast_bf16(moe_w1), _cast_bf16(moe_w3), _cast_bf16(moe_w2)
    tq_ctx = _pick(256, T)
    tq_lat = _pick(256, DT)
    s5_segments = ((0, B, T // S5_L), (NP // S5_L, DB, DT // S5_L))

    g_list, re_list, im_list = [], [], []
    kbuf = jnp.zeros((B, L, T, DA_HEADS * DA_VDIM), F32)
    vbuf = jnp.zeros((B, L, T, DA_HEADS * DA_VDIM), F32)
    for l in range(L):
        m = mods[l]
        shift1, scale1, gate1, shift2, scale2, gate2 = (m[:, i] for i in range(N_MOD))
        wi = w_in[l]
        w_in_p = jnp.concatenate([wi[:, :C_S5], wi[:, C_S5 + GLA_RANK:], wi[:, C_S5:C_S5 + GLA_RANK],
                                  jnp.zeros((D, P_COLS - C_GA - GLA_RANK), F32)], axis=1).astype(BF16)
        h = _norm(x, g_norm[l, 0][None], scale1, shift1, midx, tm, BF16)
        p = _matmul(h, w_in_p, tm_mm, 1536, F32, "in_proj")

        lam_init = 0.8 - 0.6 * math.exp(-0.3 * l)
        lq, lk = da_lambda_q[l], da_lambda_k[l]
        lam = jnp.exp(jnp.sum(lq[0] * lk[0])) - jnp.exp(jnp.sum(lq[1] * lk[1])) + lam_init
        lam_row = jnp.full((1, LANE), lam, F32)
        g_row = (da_subln_g[l] * (1.0 - lam_init))[None]
        o_da = jnp.concatenate([
            _attention(p, lam_row, g_row, 0, B, T, tq_ctx, DA_HEADS, _pick(ATT_KB, T)),
            _attention_latent(p, lam_row, g_row, NP, DB, DT, tq_lat, rope_tabs, ck, cvv, l)], axis=0)

        wa = jnp.zeros((1, 2, LANE, GLA_KTOT), F32).at[0, :, :GLA_RANK].set(gla_w_alpha[l]).astype(BF16)
        ba = gla_b_alpha[l].reshape(1, 2, 1, GLA_KTOT)
        o_f, o_b, gla_fin = _gla(p, wa, ba, tri, state_gla, l, ((B, T), (DB, DT)))
        o_gla = _gla_post(o_f, o_b, p, gla_norm_g[l][None], tm)

        mats = _s5_matrices(s5_a_re[l], s5_a_im[l], s5_log_dt[l], s5_b_re[l], s5_b_im[l], s5_c_re[l], s5_c_im[l])
        x0 = jnp.stack([state_s5_re[:, l], state_s5_im[:, l]], axis=2)
        x0 = jnp.concatenate([jnp.zeros((B,) + x0.shape[1:], F32), x0], axis=0)
        x0 = x0.reshape(4 * nseq, S5_SG, S5_SW // 4).transpose(1, 0, 2)
        y, s5_fin = _s5(p, mats, x0, s5_segments)
        o_s5 = _s5_post(y, p, s5_d[l][None], s5_w_glu[l].astype(BF16), s5_b_glu[l][None], tm)
        s5_fin = s5_fin.transpose(1, 0, 2).reshape(nseq, 2, 2, S5_GROUPS, S5_STATE)[:B]

        merged = _merge(h, o_da, o_gla, o_s5, w_gate[l].astype(BF16), b_gate[l][None], w_branch[l].astype(BF16),
                        tm_mm, 512)
        x = _wout(merged, w_out[l].astype(BF16), x, gate1, midx, tm, 1024)

        h2, logits_t = _norm(x, g_norm[l, 1][None], scale2, shift2, midx, tm, BF16, router=(rw_hi, rw_lo))
        asg, tiles = _route(logits_t, router_bias, ntile)
        slot = asg[0:2].astype(jnp.int32)
        buf_tok = jnp.zeros((ntile * MOE_TM,), jnp.int32).at[slot.reshape(-1)].set(tok2, unique_indices=True)
        xin = jnp.take(h2, buf_tok, axis=0, mode="clip")
        ye = _moe(xin, moe_w1_bf, moe_w3_bf, moe_w2_bf, l, tiles[0, :ntile], tiles[1, :ntile])
        x = _combine(x, jnp.take(ye, slot[0], axis=0, mode="clip"), jnp.take(ye, slot[1], axis=0, mode="clip"),
                     asg[2].reshape(N, 1),
                     asg[3].reshape(N, 1), gate2, midx, tm)

        kbuf, vbuf = _kv_store(p, kbuf, vbuf, l, B, T)
        g_list.append(gla_fin[:B])
        re_list.append(s5_fin[:, :, 0])
        im_list.append(s5_fin[:, :, 1])

    y_ctx = _norm(x, g_final[None], zero_mod, zero_mod, midx, tm, F32, rows=(0, NP))
    y_lat = _norm(x, g_final[None], zero_mod, zero_mod, midx, tm, F32, rows=(NP, NS))
    return (y_ctx.reshape(B, T, D), y_lat.reshape(DB, DT, D),
            kbuf.reshape(B, L, T, DA_HEADS, 2 * DA_HEAD), vbuf.reshape(B, L, T, DA_HEADS, DA_VDIM),
            jnp.stack(g_list, axis=1),
            jnp.stack(re_list, axis=1), jnp.stack(im_list, axis=1))
```
